```python
import jax, jax.numpy as jnp
from jax import lax
import numpy as np

D_MODEL = 1024
BATCH = 8
SEQ = 2048
DEPTH = 2

PLE_DIM = 256
BRANCH_WIDTH = D_MODEL // 2
N_BRANCH = 3
FOX_HEAD_DIM = 64
FOX_HEADS = BRANCH_WIDTH // FOX_HEAD_DIM
FOX_WIDTH = FOX_HEADS * FOX_HEAD_DIM
FOX_BLOCK = 128
SC_WIDTH = BRANCH_WIDTH
SC_KERNEL = 3
DN_HEAD_DIM = 128
DN_HEADS = BRANCH_WIDTH // DN_HEAD_DIM
DN_WIDTH = DN_HEADS * DN_HEAD_DIM
DN_CONV = 4
DN_CHUNK = 64
D_FF = 128 * ((8 * D_MODEL // 3 + 127) // 128)
FFN_CONV = 3
EPS = 1e-6

IN_SIZES = (3 * FOX_WIDTH, FOX_HEADS, 3 * SC_WIDTH, 3 * DN_WIDTH, DN_HEADS, DN_HEADS, DN_WIDTH, N_BRANCH * D_MODEL)
IN_WIDTH = sum(IN_SIZES)

kernel_name = 'hybrid_fox_shortconv_gdn_parallel_block'


def split_cols(t, sizes):
    offs = []
    acc = 0
    for s in sizes[:-1]:
        acc += s
        offs.append(acc)
    return jnp.split(t, offs, axis=-1)


def rmsnorm(x, gain):
    xf = x.astype(jnp.float32)
    y = xf * lax.rsqrt(jnp.mean(xf * xf, axis=-1, keepdims=True) + EPS)
    return (y * gain.astype(jnp.float32)).astype(x.dtype)


def l2norm(x):
    xf = x.astype(jnp.float32)
    return xf * lax.rsqrt(jnp.sum(xf * xf, axis=-1, keepdims=True) + EPS)


def causal_dwconv(x, w):
    k_width, chans = w.shape
    return lax.conv_general_dilated(x, w[:, None, :].astype(x.dtype), window_strides=(1,),
                                    padding=[(k_width - 1, 0)],
                                    dimension_numbers=('NWC', 'WIO', 'NWC'),
                                    feature_group_count=chans)


def forgetting_attention(q, k, v, f_logit, b_f, q_gain, k_gain):
    seq = q.shape[1]
    dh = q.shape[-1]
    q = rmsnorm(q, q_gain).transpose(0, 2, 1, 3)
    k = rmsnorm(k, k_gain).transpose(0, 2, 1, 3)
    v = v.transpose(0, 2, 1, 3)
    log_f = jax.nn.log_sigmoid(f_logit.astype(jnp.float32) + b_f.astype(jnp.float32))
    cum_f = jnp.cumsum(log_f, axis=1).transpose(0, 2, 1)
    scale = dh ** -0.5
    outs = []
    for start in range(0, seq, FOX_BLOCK):
        end = start + FOX_BLOCK
        s = jnp.einsum('bhqd,bhkd->bhqk', q[:, :, start:end], k[:, :, :end],
                       preferred_element_type=jnp.float32) * scale
        s = s + cum_f[:, :, start:end, None] - cum_f[:, :, None, :end]
        causal = jnp.arange(start, end)[:, None] >= jnp.arange(end)[None, :]
        s = jnp.where(causal, s, -jnp.inf)
        pr = jax.nn.softmax(s, axis=-1).astype(v.dtype)
        outs.append(jnp.einsum('bhqk,bhkd->bqhd', pr, v[:, :, :end]))
    return jnp.concatenate(outs, axis=1)


def gated_delta_rule(q, k, v, g, beta):
    bsz, seq, heads, dk = q.shape
    dv = v.shape[-1]
    c = DN_CHUNK
    n_chunks = seq // c

    def to_chunks(t):
        return t.astype(jnp.float32).reshape(bsz, n_chunks, c, heads, -1).transpose(1, 0, 3, 2, 4)

    qc = to_chunks(q) * dk ** -0.5
    kc = to_chunks(k)
    vc = to_chunks(v)
    gc = to_chunks(g[..., None])[..., 0]
    bc = to_chunks(beta[..., None])[..., 0]
    gcum = jnp.cumsum(gc, axis=-1)
    incl = jnp.tril(jnp.ones((c, c), dtype=bool))
    strict = jnp.tril(jnp.ones((c, c), dtype=bool), k=-1)
    decay = jnp.exp(jnp.where(incl, gcum[..., :, None] - gcum[..., None, :], -jnp.inf))
    kb = kc * bc[..., None]
    a_mat = jnp.where(strict, jnp.einsum('nbhid,nbhjd->nbhij', kb, kc) * decay, 0.0) \
        + jnp.eye(c, dtype=jnp.float32)
    rhs = jnp.concatenate([vc * bc[..., None], kb * jnp.exp(gcum)[..., None]], axis=-1)
    sol = lax.linalg.triangular_solve(a_mat, rhs, left_side=True, lower=True, unit_diagonal=True)
    u_val, k_cum = sol[..., :dv], sol[..., dv:]
    qk = jnp.where(incl, jnp.einsum('nbhid,nbhjd->nbhij', qc, kc) * decay, 0.0)
    q_dec = qc * jnp.exp(gcum)[..., None]
    k_dec = kc * jnp.exp(gcum[..., -1:] - gcum)[..., None]
    g_tot = jnp.exp(gcum[..., -1])

    def step(state, xs):
        u_i, kcum_i, qk_i, qdec_i, kdec_i, gtot_i = xs
        v_new = u_i - jnp.einsum('bhck,bhkv->bhcv', kcum_i, state)
        out = jnp.einsum('bhck,bhkv->bhcv', qdec_i, state) + jnp.einsum('bhij,bhjv->bhiv', qk_i, v_new)
        state = state * gtot_i[..., None, None] + jnp.einsum('bhck,bhcv->bhkv', kdec_i, v_new)
        return state, out

    state0 = jnp.zeros((bsz, heads, dk, dv), jnp.float32)
    _, out = lax.scan(step, state0, (u_val, k_cum, qk, q_dec, k_dec, g_tot))
    return out.transpose(1, 0, 3, 2, 4).reshape(bsz, seq, heads, dv)


def hybrid_layer(x, p_i, g_mix, w_in, b_fox_f, fox_q_gain, fox_k_gain, sc_conv_w, dn_conv_w,
                 dn_a_log, dn_dt_bias, dn_norm_gain, w_branch, w_o, g_ffn, w_up, ffn_conv_w,
                 w_down, g_ple, w_ple_gate, w_ple):
    bsz, seq, _ = x.shape
    h = rmsnorm(x, g_mix)
    proj = h @ w_in
    fox_qkv, fox_f, sc_bcv, dn_qkv, dn_b, dn_a, dn_z, br_gate = split_cols(proj, IN_SIZES)

    fq, fk, fv = [t.reshape(bsz, seq, FOX_HEADS, FOX_HEAD_DIM) for t in jnp.split(fox_qkv, 3, axis=-1)]
    y_fox = forgetting_attention(fq, fk, fv, fox_f, b_fox_f, fox_q_gain, fox_k_gain)
    y_fox = y_fox.reshape(bsz, seq, FOX_WIDTH)

    sb, sc, sv = jnp.split(sc_bcv, 3, axis=-1)
    y_sc = sb * causal_dwconv(sc * sv, sc_conv_w)

    dn_qkv = jax.nn.silu(causal_dwconv(dn_qkv, dn_conv_w))
    dq, dk_, dv_ = [t.reshape(bsz, seq, DN_HEADS, DN_HEAD_DIM) for t in jnp.split(dn_qkv, 3, axis=-1)]
    beta = jax.nn.sigmoid(dn_b.astype(jnp.float32))
    g = -jnp.exp(dn_a_log.astype(jnp.float32)) * jax.nn.softplus(dn_a.astype(jnp.float32) + dn_dt_bias.astype(jnp.float32))
    o_dn = gated_delta_rule(l2norm(dq), l2norm(dk_), dv_, g, beta).astype(x.dtype)
    z = dn_z.reshape(bsz, seq, DN_HEADS, DN_HEAD_DIM)
    y_dn = (rmsnorm(o_dn, dn_norm_gain) * jax.nn.silu(z)).reshape(bsz, seq, DN_WIDTH)

    ys = jnp.stack([y_fox, y_sc, y_dn], axis=2)
    gates = jax.nn.sigmoid(br_gate).reshape(bsz, seq, N_BRANCH, D_MODEL)
    merged = jnp.sum(jnp.einsum('bsnc,ncd->bsnd', ys, w_branch) * gates, axis=2)
    x = x + merged @ w_o

    u = causal_dwconv(rmsnorm(x, g_ffn) @ w_up, ffn_conv_w)
    u_gate, u_val = jnp.split(u, 2, axis=-1)
    x = x + (jax.nn.silu(u_gate) * u_val) @ w_down

    x = x + jax.nn.sigmoid(rmsnorm(x, g_ple) @ w_ple_gate) * (p_i.astype(x.dtype) @ w_ple)
    return x


def setup_inputs(seed: int = 0) -> dict:
    key = jax.random.key(seed)
    ks = jax.random.split(key, 24)
    f32 = jnp.float32

    def nrm(k, shape, scale):
        return jax.random.normal(k, shape, f32) * scale

    def gain(k, shape):
        return 1.0 + 0.02 * jax.random.normal(k, shape, f32)

    x = nrm(ks[0], (BATCH, SEQ, D_MODEL), 1.0)
    p = nrm(ks[1], (DEPTH, BATCH, SEQ, PLE_DIM), 1.0)
    g_mix = gain(ks[2], (DEPTH, D_MODEL))
    w_in = nrm(ks[3], (DEPTH, D_MODEL, IN_WIDTH), D_MODEL ** -0.5)
    b_fox_f = jnp.linspace(1.0, 5.0, FOX_HEADS, dtype=f32)[None, :] + nrm(ks[4], (DEPTH, FOX_HEADS), 0.1)
    fox_q_gain = gain(ks[5], (DEPTH, FOX_HEAD_DIM))
    fox_k_gain = gain(ks[6], (DEPTH, FOX_HEAD_DIM))
    sc_conv_w = nrm(ks[7], (DEPTH, SC_KERNEL, SC_WIDTH), SC_KERNEL ** -0.5)
    dn_conv_w = nrm(ks[8], (DEPTH, DN_CONV, 3 * DN_WIDTH), DN_CONV ** -0.5)
    dn_a_log = jnp.log(jax.random.uniform(ks[9], (DEPTH, DN_HEADS), f32, 1.0, 16.0))
    dt = jnp.exp(jax.random.uniform(ks[10], (DEPTH, DN_HEADS), f32, float(np.log(1e-3)), float(np.log(1e-1))))
    dn_dt_bias = dt + jnp.log(-jnp.expm1(-dt))
    dn_norm_gain = gain(ks[11], (DEPTH, DN_HEAD_DIM))
    w_branch = nrm(ks[12], (DEPTH, N_BRANCH, BRANCH_WIDTH, D_MODEL), BRANCH_WIDTH ** -0.5)
    w_o = nrm(ks[13], (DEPTH, D_MODEL, D_MODEL), D_MODEL ** -0.5)
    g_ffn = gain(ks[14], (DEPTH, D_MODEL))
    w_up = nrm(ks[15], (DEPTH, D_MODEL, 2 * D_FF), D_MODEL ** -0.5)
    ffn_conv_w = nrm(ks[16], (DEPTH, FFN_CONV, 2 * D_FF), FFN_CONV ** -0.5)
    w_down = nrm(ks[17], (DEPTH, D_FF, D_MODEL), D_FF ** -0.5)
    g_ple = gain(ks[18], (DEPTH, D_MODEL))
    w_ple_gate = nrm(ks[19], (DEPTH, D_MODEL, D_MODEL), D_MODEL ** -0.5)
    w_ple = nrm(ks[20], (DEPTH, PLE_DIM, D_MODEL), PLE_DIM ** -0.5)
    return {'x': x, 'p': p, 'g_mix': g_mix, 'w_in': w_in, 'b_fox_f': b_fox_f,
            'fox_q_gain': fox_q_gain, 'fox_k_gain': fox_k_gain, 'sc_conv_w': sc_conv_w,
            'dn_conv_w': dn_conv_w, 'dn_a_log': dn_a_log, 'dn_dt_bias': dn_dt_bias,
            'dn_norm_gain': dn_norm_gain, 'w_branch': w_branch, 'w_o': w_o, 'g_ffn': g_ffn,
            'w_up': w_up, 'ffn_conv_w': ffn_conv_w, 'w_down': w_down, 'g_ple': g_ple,
            'w_ple_gate': w_ple_gate, 'w_ple': w_ple}


def reference(x, p, g_mix, w_in, b_fox_f, fox_q_gain, fox_k_gain, sc_conv_w, dn_conv_w,
              dn_a_log, dn_dt_bias, dn_norm_gain, w_branch, w_o, g_ffn, w_up, ffn_conv_w,
              w_down, g_ple, w_ple_gate, w_ple):
    for i in range(DEPTH):
        x = hybrid_layer(x, p[i], g_mix[i], w_in[i], b_fox_f[i], fox_q_gain[i], fox_k_gain[i],
                         sc_conv_w[i], dn_conv_w[i], dn_a_log[i], dn_dt_bias[i], dn_norm_gain[i],
                         w_branch[i], w_o[i], g_ffn[i], w_up[i], ffn_conv_w[i], w_down[i],
                         g_ple[i], w_ple_gate[i], w_ple[i])
    return x
```

```python
import functools

import jax
import jax.numpy as jnp
from jax import lax
from jax.experimental import pallas as pl
from jax.experimental.pallas import tpu as pltpu

F32 = jnp.float32
BF16 = jnp.bfloat16
HI = lax.Precision.HIGHEST

D_MODEL = 1024
BATCH = 8
SEQ = 2048
TOKENS = BATCH * SEQ
PLE_DIM = 256
BRANCH_WIDTH = 512
FOX_HEADS = 8
FOX_HEAD_DIM = 64
DN_HEADS = 4
DN_HEAD_DIM = 128
DN_CHUNK = 64
N_CHUNKS = SEQ // DN_CHUNK
D_FF = 2816
EPS = 1e-6

LANES = 128
HALO = 8
FFN_HALO = 16

PROJ_WIDTH = 8192
COL_FOX = 0
COL_SC = 1536
COL_DN = 3072
COL_DNZ = 4608
COL_GATE = 5120
SM_FOXF = 0
SM_BETA = 8
SM_A = 12

PROJ_TM = 1024
PROJ_TN = 1024
FOX_TQ = 256
FOX_TK = 256
MERGE_TM = 256
FFN_TM = 512
FFN_TF = 256
FFN_NF = D_FF // FFN_TF

VMEM_LIMIT = 48 * 1024 * 1024


def _cparams(sem):
    return pltpu.CompilerParams(dimension_semantics=sem, vmem_limit_bytes=VMEM_LIMIT)


def _sigmoid(x):
    return 1.0 / (1.0 + jnp.exp(-x))


def _softplus(x):
    return jnp.maximum(x, 0.0) + jnp.log1p(jnp.exp(-jnp.abs(x)))


def _dot(a, b, precision=None):
    return jnp.dot(a, b, preferred_element_type=F32, precision=precision)


def _dot_nt(a, b):
    return lax.dot_general(a, b, (((1,), (1,)), ((), ())), preferred_element_type=F32)


def _dot_tn(a, b):
    return lax.dot_general(a, b, (((0,), (0,)), ((), ())), preferred_element_type=F32)


def _rmsnorm(x, gain):
    ms = jnp.mean(x * x, axis=-1, keepdims=True)
    return x * lax.rsqrt(ms + EPS) * gain


def _shifted(cat, shift, halo):
    if shift == 0:
        return cat[halo:]
    return pltpu.roll(cat, shift, 0)[halo:]


def _proj_kernel(x_ref, g_ref, w_ref, ws_ref, o_ref, os_ref, hn_ref):
    @pl.when(pl.program_id(1) == 0)
    def _():
        hn = _rmsnorm(x_ref[...], g_ref[...]).astype(BF16)
        hn_ref[...] = hn
        os_ref[...] = _dot(hn, ws_ref[...])

    o_ref[...] = _dot(hn_ref[...], w_ref[...])


def _proj(x, gain, w_big, w_small):
    grid = (TOKENS // PROJ_TM, PROJ_WIDTH // PROJ_TN)
    return pl.pallas_call(
        _proj_kernel,
        grid=grid,
        in_specs=[
            pl.BlockSpec((PROJ_TM, D_MODEL), lambda i, j: (i, 0)),
            pl.BlockSpec((1, D_MODEL), lambda i, j: (0, 0)),
            pl.BlockSpec((D_MODEL, PROJ_TN), lambda i, j: (0, j)),
            pl.BlockSpec((D_MODEL, LANES), lambda i, j: (0, 0)),
        ],
        out_specs=[
            pl.BlockSpec((PROJ_TM, PROJ_TN), lambda i, j: (i, j)),
            pl.BlockSpec((PROJ_TM, LANES), lambda i, j: (i, 0)),
        ],
        out_shape=[
            jax.ShapeDtypeStruct((TOKENS, PROJ_WIDTH), F32),
            jax.ShapeDtypeStruct((TOKENS, LANES), F32),
        ],
        scratch_shapes=[pltpu.VMEM((PROJ_TM, D_MODEL), BF16)],
        compiler_params=_cparams(("parallel", "arbitrary")),
        name="proj",
    )(x, gain, w_big, w_small)


CUMF_BLK = 128


def _cumf_kernel(sm_ref, bf_ref, cfc_ref, cfr_ref):
    r = lax.broadcasted_iota(jnp.int32, (CUMF_BLK, CUMF_BLK), 0)
    c = lax.broadcasted_iota(jnp.int32, (CUMF_BLK, CUMF_BLK), 1)
    tri = (r >= c).astype(F32)
    lane = lax.broadcasted_iota(jnp.int32, (1, LANES), 1)
    first_head = lane < FOX_HEAD_DIM
    carry = jnp.zeros((1, LANES), F32)
    for blk in range(SEQ // CUMF_BLK):
        rows = slice(blk * CUMF_BLK, (blk + 1) * CUMF_BLK)
        log_f = -_softplus(-(sm_ref[rows, :] + bf_ref[...]))
        cum = _dot(tri, log_f, HI) + carry
        carry = cum[CUMF_BLK - 1:CUMF_BLK, :]
        cfr_ref[0, :, rows] = cum.T[0:FOX_HEADS, :]
        for hp in range(FOX_HEADS // 2):
            a = cum[:, SM_FOXF + 2 * hp:SM_FOXF + 2 * hp + 1]
            b = cum[:, SM_FOXF + 2 * hp + 1:SM_FOXF + 2 * hp + 2]
            cfc_ref[0, hp, rows, :] = jnp.where(first_head, a, b)


def _cumf(small, bias_row):
    return pl.pallas_call(
        _cumf_kernel,
        grid=(BATCH,),
        in_specs=[
            pl.BlockSpec((SEQ, LANES), lambda b: (b, 0)),
            pl.BlockSpec((1, LANES), lambda b: (0, 0)),
        ],
        out_specs=[
            pl.BlockSpec((1, FOX_HEADS // 2, SEQ, LANES), lambda b: (b, 0, 0, 0)),
            pl.BlockSpec((1, FOX_HEADS, SEQ), lambda b: (b, 0, 0)),
        ],
        out_shape=[
            jax.ShapeDtypeStruct((BATCH, FOX_HEADS // 2, SEQ, LANES), F32),
            jax.ShapeDtypeStruct((BATCH, FOX_HEADS, SEQ), F32),
        ],
        compiler_params=_cparams(("parallel",)),
        name="cumf",
    )(small, bias_row)


def _fox_kernel(q_ref, k_ref, v_ref, cfc_ref, cfr_ref, qg_ref, kg_ref, o_ref, kn_ref, vb_ref):
    hp = pl.program_id(1)
    qi = pl.program_id(2)
    lane = lax.broadcasted_iota(jnp.int32, (1, LANES), 1)
    first_head = lane < FOX_HEAD_DIM

    def headnorm(x, gain):
        x2 = x * x
        s0 = jnp.sum(jnp.where(first_head, x2, 0.0), axis=-1, keepdims=True)
        s1 = jnp.sum(jnp.where(first_head, 0.0, x2), axis=-1, keepdims=True)
        inv = jnp.where(first_head,
                        lax.rsqrt(s0 * (1.0 / FOX_HEAD_DIM) + EPS),
                        lax.rsqrt(s1 * (1.0 / FOX_HEAD_DIM) + EPS))
        return x * inv * gain

    @pl.when(qi == 0)
    def _():
        def body(c, carry):
            rows = pl.ds(pl.multiple_of(c * FOX_TK, FOX_TK), FOX_TK)
            kn_ref[rows, :] = headnorm(k_ref[rows, :], kg_ref[...]).astype(BF16)
            vb_ref[rows, :] = v_ref[rows, :].astype(BF16)
            return carry
        lax.fori_loop(0, SEQ // FOX_TK, body, 0)

    qn = headnorm(q_ref[...], qg_ref[...]) * (FOX_HEAD_DIM ** -0.5)
    qh = (jnp.where(first_head, qn, 0.0).astype(BF16), jnp.where(first_head, 0.0, qn).astype(BF16))
    cfq_all = cfc_ref[0, 0]
    cfq = (cfq_all[:, 0:1], cfq_all[:, FOX_HEAD_DIM:FOX_HEAD_DIM + 1])

    def step(j, carry, masked):
        off = pl.multiple_of(j * FOX_TK, FOX_TK)
        kb = kn_ref[pl.ds(off, FOX_TK), :]
        vv = vb_ref[pl.ds(off, FOX_TK), :]
        out = []
        for h in range(2):
            m, l, acc = carry[h]
            cfk = cfr_ref[0, pl.ds(2 * hp + h, 1), pl.ds(off, FOX_TK)]
            s = _dot_nt(qh[h], kb)
            s = s + cfq[h] - cfk
            if masked:
                row = lax.broadcasted_iota(jnp.int32, (FOX_TQ, FOX_TK), 0)
                col = lax.broadcasted_iota(jnp.int32, (FOX_TQ, FOX_TK), 1)
                s = jnp.where(row >= col, s, -jnp.inf)
            m_new = jnp.maximum(m, jnp.max(s, axis=-1, keepdims=True))
            alpha = jnp.exp(m - m_new)
            p = jnp.exp(s - m_new)
            l = alpha * l + jnp.sum(p, axis=-1, keepdims=True)
            acc = alpha * acc + _dot(p.astype(BF16), vv)
            out.append((m_new, l, acc))
        return tuple(out)

    init = tuple((jnp.full((FOX_TQ, 1), -jnp.inf, F32), jnp.zeros((FOX_TQ, 1), F32),
                  jnp.zeros((FOX_TQ, LANES), F32)) for _ in range(2))
    carry = lax.fori_loop(0, qi, lambda j, c: step(j, c, False), init)
    (_, l0, a0), (_, l1, a1) = step(qi, carry, True)
    o_ref[...] = jnp.where(first_head, a0 / l0, a1 / l1)


def _fox(proj, cfc, cfr, q_gain2, k_gain2):
    nq = SEQ // FOX_TQ
    cb = BRANCH_WIDTH // LANES
    return pl.pallas_call(
        _fox_kernel,
        grid=(BATCH, FOX_HEADS // 2, nq),
        in_specs=[
            pl.BlockSpec((FOX_TQ, LANES), lambda b, hp, qi: (b * nq + qi, COL_FOX // LANES + hp)),
            pl.BlockSpec((SEQ, LANES), lambda b, hp, qi: (b, COL_FOX // LANES + cb + hp)),
            pl.BlockSpec((SEQ, LANES), lambda b, hp, qi: (b, COL_FOX // LANES + 2 * cb + hp)),
            pl.BlockSpec((1, 1, FOX_TQ, LANES), lambda b, hp, qi: (b, hp, qi, 0)),
            pl.BlockSpec((1, FOX_HEADS, SEQ), lambda b, hp, qi: (b, 0, 0)),
            pl.BlockSpec((1, LANES), lambda b, hp, qi: (0, 0)),
            pl.BlockSpec((1, LANES), lambda b, hp, qi: (0, 0)),
        ],
        out_specs=pl.BlockSpec((FOX_TQ, LANES), lambda b, hp, qi: (b * nq + qi, hp)),
        out_shape=jax.ShapeDtypeStruct((TOKENS, BRANCH_WIDTH), F32),
        scratch_shapes=[pltpu.VMEM((SEQ, LANES), BF16), pltpu.VMEM((SEQ, LANES), BF16)],
        compiler_params=_cparams(("parallel", "parallel", "arbitrary")),
        name="fox",
    )(proj, proj, proj, cfc, cfr, q_gain2, k_gain2)


DN_HPS = 2
DN_W = DN_HPS * DN_HEAD_DIM


def _dn_kernel(q_ref, k_ref, v_ref, z_ref, sm_ref, wq_ref, wk_ref, wv_ref, alog_ref, dtb_ref,
               ng_ref, o_ref, u_ref, kcum_ref, qdec_ref, kdec_ref, qk_ref, gt_ref, st_ref):
    hh = pl.program_id(1)
    c = DN_CHUNK
    ri = lax.broadcasted_iota(jnp.int32, (c, c), 0)
    ci = lax.broadcasted_iota(jnp.int32, (c, c), 1)
    incl = ri >= ci
    strict = ri > ci
    eye = ri == ci
    eye_f = eye.astype(F32)
    tri = incl.astype(F32)
    lane = lax.broadcasted_iota(jnp.int32, (1, LANES), 1)

    def conv_silu(ref, w_ref, rows, prow, keep_prev, cols):
        cur = ref[rows, cols]
        prev = ref[prow, cols] * keep_prev
        cat = jnp.concatenate([prev, cur], axis=0)
        w = w_ref[:, cols]
        y = (w[3:4] * cur + w[2:3] * _shifted(cat, 1, HALO) + w[1:2] * _shifted(cat, 2, HALO)
             + w[0:1] * _shifted(cat, 3, HALO))
        return y * _sigmoid(y)

    def l2norm(x):
        return x * lax.rsqrt(jnp.sum(x * x, axis=-1, keepdims=True) + EPS)

    def prepare(n, carry):
        r0 = pl.multiple_of(n * c, c)
        rows = pl.ds(r0, c)
        prow = pl.ds(pl.multiple_of(jnp.maximum(r0 - HALO, 0), HALO), HALO)
        keep_prev = jnp.where(n > 0, 1.0, 0.0)
        sm = sm_ref[rows, :]
        beta_all = _sigmoid(sm)
        g_all = -jnp.exp(alog_ref[...]) * _softplus(sm + dtb_ref[...])
        gcum_all = _dot(tri, g_all, HI)
        for h in range(DN_HPS):
            cols = slice(h * DN_HEAD_DIM, (h + 1) * DN_HEAD_DIM)
            head = hh * DN_HPS + h
            gc = jnp.sum(jnp.where(lane == SM_A + head, gcum_all, 0.0), axis=-1, keepdims=True)
            beta = jnp.sum(jnp.where(lane == SM_BETA + head, beta_all, 0.0), axis=-1, keepdims=True)
            gc = jnp.broadcast_to(gc, (c, DN_HEAD_DIM))
            beta = jnp.broadcast_to(beta, (c, DN_HEAD_DIM))
            q = l2norm(conv_silu(q_ref, wq_ref, rows, prow, keep_prev, cols)) * (DN_HEAD_DIM ** -0.5)
            k = l2norm(conv_silu(k_ref, wk_ref, rows, prow, keep_prev, cols))
            v = conv_silu(v_ref, wv_ref, rows, prow, keep_prev, cols)
            kb = k * beta
            eg = jnp.exp(gc)
            gc_sq = gc[:, :c]
            g_row = jnp.sum(jnp.where(eye, gc_sq, 0.0), axis=0, keepdims=True)
            decay = jnp.exp(jnp.where(incl, gc_sq - g_row, -jnp.inf))
            k16 = k.astype(BF16)
            lower = jnp.where(strict, _dot_nt(kb.astype(BF16), k16) * decay, 0.0)
            pw = -lower
            inv = eye_f + pw
            for _ in range(5):
                pw = _dot(pw, pw, HI)
                inv = inv + _dot(inv, pw, HI)
            rhs = jnp.concatenate([v * beta, kb * eg], axis=1)
            sol = _dot(inv, rhs, HI)
            qk = jnp.where(incl, _dot_nt(q.astype(BF16), k16) * decay, 0.0)
            g_last = gc[c - 1:c, :]
            u_ref[rows, cols] = sol[:, :DN_HEAD_DIM]
            kcum_ref[rows, cols] = sol[:, DN_HEAD_DIM:].astype(BF16)
            qdec_ref[rows, cols] = (q * eg).astype(BF16)
            kdec_ref[rows, cols] = (k * jnp.exp(g_last - gc)).astype(BF16)
            qk_ref[rows, h * c:(h + 1) * c] = qk.astype(BF16)
            gt_ref[n, :, cols] = jnp.broadcast_to(jnp.exp(g_last), (HALO, DN_HEAD_DIM))
        return carry

    lax.fori_loop(0, N_CHUNKS, prepare, 0)

    st_ref[...] = jnp.zeros_like(st_ref)

    def scan(n, carry):
        rows = pl.ds(pl.multiple_of(n * c, c), c)
        for h in range(DN_HPS):
            cols = slice(h * DN_HEAD_DIM, (h + 1) * DN_HEAD_DIM)
            state = st_ref[h]
            state16 = state.astype(BF16)
            v_new = u_ref[rows, cols] - _dot(kcum_ref[rows, cols], state16)
            v16 = v_new.astype(BF16)
            out = _dot(qdec_ref[rows, cols], state16) + _dot(qk_ref[rows, h * c:(h + 1) * c], v16)
            st_ref[h] = state * gt_ref[n, 0:1, cols] + _dot_tn(kdec_ref[rows, cols], v16)
            z = z_ref[rows, cols]
            o_ref[rows, cols] = _rmsnorm(out, ng_ref[...]) * (z * _sigmoid(z))
        return carry

    lax.fori_loop(0, N_CHUNKS, scan, 0)


def _deltanet(proj, small, conv_w, alog_row, dtb_row, norm_gain):
    nh = DN_HEADS // DN_HPS
    qb = COL_DN // DN_W
    zb = COL_DNZ // DN_W

    def colspec(base):
        return pl.BlockSpec((SEQ, DN_W), lambda b, hh: (b, base + hh))

    def wspec(base):
        return pl.BlockSpec((4, DN_W), lambda b, hh: (0, base + hh))

    row = pl.BlockSpec((1, LANES), lambda b, hh: (0, 0))
    return pl.pallas_call(
        _dn_kernel,
        grid=(BATCH, nh),
        in_specs=[
            colspec(qb), colspec(qb + nh), colspec(qb + 2 * nh), colspec(zb),
            pl.BlockSpec((SEQ, LANES), lambda b, hh: (b, 0)),
            wspec(0), wspec(nh), wspec(2 * nh),
            row, row, row,
        ],
        out_specs=pl.BlockSpec((SEQ, DN_W), lambda b, hh: (b, hh)),
        out_shape=jax.ShapeDtypeStruct((TOKENS, BRANCH_WIDTH), F32),
        scratch_shapes=[
            pltpu.VMEM((SEQ, DN_W), F32),
            pltpu.VMEM((SEQ, DN_W), BF16),
            pltpu.VMEM((SEQ, DN_W), BF16),
            pltpu.VMEM((SEQ, DN_W), BF16),
            pltpu.VMEM((SEQ, DN_HPS * DN_CHUNK), BF16),
            pltpu.VMEM((N_CHUNKS, HALO, DN_W), F32),
            pltpu.VMEM((DN_HPS, DN_HEAD_DIM, DN_HEAD_DIM), F32),
        ],
        compiler_params=_cparams(("parallel", "arbitrary")),
        name="deltanet",
    )(proj, proj, proj, proj, small, conv_w, conv_w, conv_w, alog_row, dtb_row, norm_gain)


def _merge_kernel(yf_ref, sb_ref, sc_ref, sv_ref, scp_ref, svp_ref, yd_ref, g0_ref, g1_ref, g2_ref,
                  x_ref, wb_ref, wo_ref, cw_ref, o_ref):
    i = pl.program_id(0)
    keep_prev = jnp.where(i % (SEQ // MERGE_TM) != 0, 1.0, 0.0)
    cur = sc_ref[...] * sv_ref[...]
    prev = scp_ref[...] * svp_ref[...] * keep_prev
    cat = jnp.concatenate([prev, cur], axis=0)
    cw = cw_ref[...]
    conv = cw[2:3] * cur + cw[1:2] * _shifted(cat, 1, HALO) + cw[0:1] * _shifted(cat, 2, HALO)
    y_sc = sb_ref[...] * conv
    merged = (_sigmoid(g0_ref[...]) * _dot(yf_ref[...].astype(BF16), wb_ref[0])
              + _sigmoid(g1_ref[...]) * _dot(y_sc.astype(BF16), wb_ref[1])
              + _sigmoid(g2_ref[...]) * _dot(yd_ref[...].astype(BF16), wb_ref[2]))
    o_ref[...] = x_ref[...] + _dot(merged.astype(BF16), wo_ref[...])


def _merge(x, proj, y_fox, y_dn, w_branch, w_o, sc_conv_w):
    tm = MERGE_TM
    scb = COL_SC // BRANCH_WIDTH
    gb = COL_GATE // D_MODEL
    hpb = tm // HALO

    def prev_rows(i):
        return jnp.maximum(i * hpb - 1, 0)

    return pl.pallas_call(
        _merge_kernel,
        grid=(TOKENS // tm,),
        in_specs=[
            pl.BlockSpec((tm, BRANCH_WIDTH), lambda i: (i, 0)),
            pl.BlockSpec((tm, BRANCH_WIDTH), lambda i: (i, scb)),
            pl.BlockSpec((tm, BRANCH_WIDTH), lambda i: (i, scb + 1)),
            pl.BlockSpec((tm, BRANCH_WIDTH), lambda i: (i, scb + 2)),
            pl.BlockSpec((HALO, BRANCH_WIDTH), lambda i: (prev_rows(i), scb + 1)),
            pl.BlockSpec((HALO, BRANCH_WIDTH), lambda i: (prev_rows(i), scb + 2)),
            pl.BlockSpec((tm, BRANCH_WIDTH), lambda i: (i, 0)),
            pl.BlockSpec((tm, D_MODEL), lambda i: (i, gb)),
            pl.BlockSpec((tm, D_MODEL), lambda i: (i, gb + 1)),
            pl.BlockSpec((tm, D_MODEL), lambda i: (i, gb + 2)),
            pl.BlockSpec((tm, D_MODEL), lambda i: (i, 0)),
            pl.BlockSpec((3, BRANCH_WIDTH, D_MODEL), lambda i: (0, 0, 0)),
            pl.BlockSpec((D_MODEL, D_MODEL), lambda i: (0, 0)),
            pl.BlockSpec((3, BRANCH_WIDTH), lambda i: (0, 0)),
        ],
        out_specs=pl.BlockSpec((tm, D_MODEL), lambda i: (i, 0)),
        out_shape=jax.ShapeDtypeStruct((TOKENS, D_MODEL), F32),
        compiler_params=_cparams(("parallel",)),
        name="merge",
    )(y_fox, proj, proj, proj, proj, proj, y_dn, proj, proj, proj, x, w_branch, w_o, sc_conv_w)


def _ffn_kernel(x_ref, xp_ref, gf_ref, wg_ref, wv_ref, cg_ref, cv_ref, wd_ref, gp_ref, wpg_ref,
                p_ref, wple_ref, o_ref, hn_ref, acc_ref):
    i = pl.program_id(0)
    j = pl.program_id(1)

    @pl.when(j == 0)
    def _():
        keep_prev = jnp.where(i % (SEQ // FFN_TM) != 0, 1.0, 0.0)
        xx = jnp.concatenate([xp_ref[...] * keep_prev, x_ref[...]], axis=0)
        hn_ref[...] = _rmsnorm(xx, gf_ref[...]).astype(BF16)
        acc_ref[...] = jnp.zeros_like(acc_ref)

    hn = hn_ref[...]

    def branch(w_ref, c_ref):
        u = _dot(hn, w_ref[...])
        cw = c_ref[...]
        return (cw[2:3] * u[FFN_HALO:] + cw[1:2] * _shifted(u, 1, FFN_HALO)
                + cw[0:1] * _shifted(u, 2, FFN_HALO))

    gate = branch(wg_ref, cg_ref)
    val = branch(wv_ref, cv_ref)
    act = (gate * _sigmoid(gate) * val).astype(BF16)
    acc_ref[...] += _dot(act, wd_ref[...])

    @pl.when(j == FFN_NF - 1)
    def _():
        x1 = x_ref[...] + acc_ref[...]
        h2 = _rmsnorm(x1, gp_ref[...]).astype(BF16)
        pgate = _sigmoid(_dot(h2, wpg_ref[...]))
        emb = _dot(p_ref[...].astype(BF16), wple_ref[...])
        o_ref[...] = x1 + pgate * emb


def _ffn(x, p, g_ffn, w_up, conv_w, w_down, g_ple, w_pg, w_ple):
    tm = FFN_TM
    hpb = tm // FFN_HALO
    return pl.pallas_call(
        _ffn_kernel,
        grid=(TOKENS // tm, FFN_NF),
        in_specs=[
            pl.BlockSpec((tm, D_MODEL), lambda i, j: (i, 0)),
            pl.BlockSpec((FFN_HALO, D_MODEL), lambda i, j: (jnp.maximum(i * hpb - 1, 0), 0)),
            pl.BlockSpec((1, D_MODEL), lambda i, j: (0, 0)),
            pl.BlockSpec((D_MODEL, FFN_TF), lambda i, j: (0, j)),
            pl.BlockSpec((D_MODEL, FFN_TF), lambda i, j: (0, FFN_NF + j)),
            pl.BlockSpec((3, FFN_TF), lambda i, j: (0, j)),
            pl.BlockSpec((3, FFN_TF), lambda i, j: (0, FFN_NF + j)),
            pl.BlockSpec((FFN_TF, D_MODEL), lambda i, j: (j, 0)),
            pl.BlockSpec((1, D_MODEL), lambda i, j: (0, 0)),
            pl.BlockSpec((D_MODEL, D_MODEL), lambda i, j: (0, 0)),
            pl.BlockSpec((tm, PLE_DIM), lambda i, j: (i, 0)),
            pl.BlockSpec((PLE_DIM, D_MODEL), lambda i, j: (0, 0)),
        ],
        out_specs=pl.BlockSpec((tm, D_MODEL), lambda i, j: (i, 0)),
        out_shape=jax.ShapeDtypeStruct((TOKENS, D_MODEL), F32),
        scratch_shapes=[pltpu.VMEM((tm + FFN_HALO, D_MODEL), BF16), pltpu.VMEM((tm, D_MODEL), F32)],
        compiler_params=_cparams(("parallel", "arbitrary")),
        name="ffn",
    )(x, x, g_ffn, w_up, w_up, conv_w, conv_w, w_down, g_ple, w_pg, p, w_ple)


def _lane_row(values, offset):
    row = jnp.zeros((1, LANES), F32)
    return lax.dynamic_update_slice(row, values.astype(F32)[None, :], (0, offset))


def _layer(x, p_i, g_mix, w_in, b_fox_f, fox_q_gain, fox_k_gain, sc_conv_w, dn_conv_w, dn_a_log,
           dn_dt_bias, dn_norm_gain, w_branch, w_o, g_ffn, w_up, ffn_conv_w, w_down, g_ple,
           w_ple_gate, w_ple):
    o_f = 3 * BRANCH_WIDTH
    o_sc = o_f + FOX_HEADS
    o_dn = o_sc + 3 * BRANCH_WIDTH
    o_b = o_dn + 3 * BRANCH_WIDTH
    o_a = o_b + DN_HEADS
    o_z = o_a + DN_HEADS
    o_g = o_z + BRANCH_WIDTH
    w_big = jnp.concatenate([w_in[:, :o_f], w_in[:, o_sc:o_dn], w_in[:, o_dn:o_b], w_in[:, o_z:o_g],
                             w_in[:, o_g:]], axis=1).astype(BF16)
    w_small = jnp.concatenate([w_in[:, o_f:o_sc], w_in[:, o_b:o_a], w_in[:, o_a:o_z],
                               jnp.zeros((D_MODEL, LANES - FOX_HEADS - 2 * DN_HEADS), F32)],
                              axis=1).astype(BF16)

    proj, small = _proj(x, g_mix[None, :], w_big, w_small)
    cfc, cfr = _cumf(small, _lane_row(b_fox_f, SM_FOXF))
    y_fox = _fox(proj, cfc, cfr, jnp.tile(fox_q_gain, 2)[None, :], jnp.tile(fox_k_gain, 2)[None, :])
    y_dn = _deltanet(proj, small, dn_conv_w, _lane_row(dn_a_log, SM_A), _lane_row(dn_dt_bias, SM_A),
                     dn_norm_gain[None, :])
    x = _merge(x, proj, y_fox, y_dn, w_branch.astype(BF16), w_o.astype(BF16), sc_conv_w)
    x = _ffn(x, p_i, g_ffn[None, :], w_up.astype(BF16), ffn_conv_w, w_down.astype(BF16),
             g_ple[None, :], w_ple_gate.astype(BF16), w_ple.astype(BF16))
    return x


def kernel(x, p, g_mix, w_in, b_fox_f, fox_q_gain, fox_k_gain, sc_conv_w, dn_conv_w, dn_a_log,
           dn_dt_bias, dn_norm_gain, w_branch, w_o, g_ffn, w_up, ffn_conv_w, w_down, g_ple,
           w_ple_gate, w_ple):
    depth = p.shape[0]
    x = x.reshape(TOKENS, D_MODEL)
    p = p.reshape(depth, TOKENS, PLE_DIM)
    for i in range(depth):
        x = _layer(x, p[i], g_mix[i], w_in[i], b_fox_f[i], fox_q_gain[i], fox_k_gain[i],
                   sc_conv_w[i], dn_conv_w[i], dn_a_log[i], dn_dt_bias[i], dn_norm_gain[i],
                   w_branch[i], w_o[i], g_ffn[i], w_up[i], ffn_conv_w[i], w_down[i], g_ple[i],
                   w_ple_gate[i], w_ple[i])
    return x.reshape(BATCH, SEQ, D_MODEL)
```

```python
import jax
import jax.numpy as jnp
from jax import lax
from jax.experimental import pallas as pl
from jax.experimental.pallas import tpu as pltpu

F32 = jnp.float32
BF16 = jnp.bfloat16
HI = lax.Precision.HIGHEST

D_MODEL = 1024
BATCH = 8
SEQ = 2048
TOKENS = BATCH * SEQ
PLE_DIM = 256
BRANCH_WIDTH = 512
FOX_HEADS = 8
FOX_HEAD_DIM = 64
DN_HEADS = 4
DN_HEAD_DIM = 128
DN_CHUNK = 64
N_CHUNKS = SEQ // DN_CHUNK
D_FF = 2816
EPS = 1e-6

LANES = 128
SUBLANES = 8
HALO = 16

PROJ_WIDTH = 8192
COL_FOX = 0
COL_SC = 1536
COL_DN = 3072
COL_DNZ = 4608
COL_GATE = 5120
SM_FOXF = 0
SM_BETA = 8
SM_A = 12

PROJ_TM = 1024
PROJ_TN = 1024
FOX_TQ = 256
FOX_TK = 256
MERGE_TM = 512
FFN_TM = 512
FFN_TF = 256
FFN_NF = D_FF // FFN_TF

VMEM_LIMIT = 48 * 1024 * 1024


def _cparams(sem):
    return pltpu.CompilerParams(dimension_semantics=sem, vmem_limit_bytes=VMEM_LIMIT)


def _sigmoid(x):
    return 1.0 / (1.0 + jnp.exp(-x))


def _softplus(x):
    return jnp.maximum(x, 0.0) + jnp.log1p(jnp.exp(-jnp.abs(x)))


def _dot(a, b, precision=None):
    return jnp.dot(a, b, preferred_element_type=F32, precision=precision)


def _dot_nt(a, b):
    return lax.dot_general(a, b, (((1,), (1,)), ((), ())), preferred_element_type=F32)


def _dot_tn(a, b):
    return lax.dot_general(a, b, (((0,), (0,)), ((), ())), preferred_element_type=F32)


def _rmsnorm(x, gain):
    ms = jnp.mean(x * x, axis=-1, keepdims=True)
    return x * lax.rsqrt(ms + EPS) * gain


def _shifted(cat, shift):
    if shift == 0:
        return cat[HALO:]
    return pltpu.roll(cat, shift, 0)[HALO:]


def _bf16_split3(x):
    hi = x.astype(BF16).astype(F32)
    r = x - hi
    mid = r.astype(BF16).astype(F32)
    lo = (r - mid).astype(BF16).astype(F32)
    return hi, mid, lo


def _proj_kernel(x_ref, g_ref, w_ref, ws_ref, o_ref, os_ref, hn_ref):
    @pl.when(pl.program_id(1) == 0)
    def _():
        hn = _rmsnorm(x_ref[...], g_ref[...]).astype(BF16)
        hn_ref[...] = hn
        os_ref[...] = _dot(hn, ws_ref[...])

    o_ref[...] = _dot(hn_ref[...], w_ref[...]).astype(BF16)


def _proj(x, gain, w_big, w_small):
    grid = (TOKENS // PROJ_TM, PROJ_WIDTH // PROJ_TN)
    return pl.pallas_call(
        _proj_kernel,
        grid=grid,
        in_specs=[
            pl.BlockSpec((PROJ_TM, D_MODEL), lambda i, j: (i, 0)),
            pl.BlockSpec((1, D_MODEL), lambda i, j: (0, 0)),
            pl.BlockSpec((D_MODEL, PROJ_TN), lambda i, j: (0, j)),
            pl.BlockSpec((D_MODEL, LANES), lambda i, j: (0, 0)),
        ],
        out_specs=[
            pl.BlockSpec((PROJ_TM, PROJ_TN), lambda i, j: (i, j)),
            pl.BlockSpec((PROJ_TM, LANES), lambda i, j: (i, 0)),
        ],
        out_shape=[
            jax.ShapeDtypeStruct((TOKENS, PROJ_WIDTH), BF16),
            jax.ShapeDtypeStruct((TOKENS, LANES), F32),
        ],
        scratch_shapes=[pltpu.VMEM((PROJ_TM, D_MODEL), BF16)],
        compiler_params=_cparams(("parallel", "arbitrary")),
        name="proj",
    )(x, gain, w_big, w_small)


CUMF_BLK = 128


def _cumf_kernel(sm_ref, bf_ref, cf_ref):
    r = lax.broadcasted_iota(jnp.int32, (CUMF_BLK, CUMF_BLK), 0)
    c = lax.broadcasted_iota(jnp.int32, (CUMF_BLK, CUMF_BLK), 1)
    tri = (r >= c).astype(F32)
    carry = jnp.zeros((1, LANES), F32)
    for blk in range(SEQ // CUMF_BLK):
        rows = slice(blk * CUMF_BLK, (blk + 1) * CUMF_BLK)
        log_f = -_softplus(-(sm_ref[rows, :] + bf_ref[...]))
        cum = _dot(tri, log_f, HI) + carry
        carry = cum[CUMF_BLK - 1:CUMF_BLK, :]
        cf_ref[0, rows, :] = cum


def _cumf(small, bias_row):
    return pl.pallas_call(
        _cumf_kernel,
        grid=(BATCH,),
        in_specs=[
            pl.BlockSpec((SEQ, LANES), lambda b: (b, 0)),
            pl.BlockSpec((1, LANES), lambda b: (0, 0)),
        ],
        out_specs=pl.BlockSpec((1, SEQ, LANES), lambda b: (b, 0, 0)),
        out_shape=jax.ShapeDtypeStruct((BATCH, SEQ, LANES), F32),
        compiler_params=_cparams(("parallel",)),
        name="cumf",
    )(small, bias_row)


AUG = FOX_HEAD_DIM


def _fox_kernel(q_ref, k_ref, v_ref, cfq_ref, cfk_ref, qg_ref, kg_ref, o_ref, ka_ref, vt_ref, qa_ref,
                acc_ref):
    qi = pl.program_id(1)
    lane = lax.broadcasted_iota(jnp.int32, (1, LANES), 1)
    first_head = lane < FOX_HEAD_DIM

    def headnorm(x, gain):
        x2 = x * x
        s0 = jnp.sum(jnp.where(first_head, x2, 0.0), axis=-1, keepdims=True)
        s1 = jnp.sum(jnp.where(first_head, 0.0, x2), axis=-1, keepdims=True)
        inv = jnp.where(first_head,
                        lax.rsqrt(s0 * (1.0 / FOX_HEAD_DIM) + EPS),
                        lax.rsqrt(s1 * (1.0 / FOX_HEAD_DIM) + EPS))
        return x * inv * gain

    def augmented(xn, cf, head, key_side):
        feat = xn if head % 2 == 0 else pltpu.roll(xn, FOX_HEAD_DIM, 1)
        cum = jnp.sum(jnp.where(lane == SM_FOXF + head, cf, 0.0), axis=-1, keepdims=True)
        hi, mid, lo = _bf16_split3(cum)
        if key_side:
            parts = (1.0, 1.0, 1.0, -hi, -mid, -lo)
        else:
            parts = (hi, mid, lo, 1.0, 1.0, 1.0)
        out = jnp.where(first_head, feat, 0.0)
        for t, part in enumerate(parts):
            out = jnp.where(lane == AUG + t, part, out)
        return out.astype(BF16)

    @pl.when(qi == 0)
    def _():
        def body(c, carry):
            off = pl.multiple_of(c * FOX_TK, FOX_TK)
            rows = pl.ds(off, FOX_TK)
            cf = cfk_ref[0, rows, :]
            for hp in range(FOX_HEADS // 2):
                pair = slice(hp * LANES, (hp + 1) * LANES)
                kn = headnorm(k_ref[rows, pair].astype(F32), kg_ref[...])
                for head in (2 * hp, 2 * hp + 1):
                    ka_ref[head, rows, :] = augmented(kn, cf, head, True)
                vt_ref[pair, rows] = v_ref[rows, pair].astype(F32).T.astype(BF16)
            return carry
        lax.fori_loop(0, SEQ // FOX_TK, body, 0)

    cfq = cfq_ref[0]
    for hp in range(FOX_HEADS // 2):
        pair = slice(hp * LANES, (hp + 1) * LANES)
        qn = headnorm(q_ref[:, pair].astype(F32), qg_ref[...]) * (FOX_HEAD_DIM ** -0.5)
        for head in (2 * hp, 2 * hp + 1):
            qa_ref[head] = augmented(qn, cfq, head, False)
    acc_ref[...] = jnp.zeros_like(acc_ref)

    hs = range(FOX_HEADS)

    def hrows(h):
        return slice(h * FOX_HEAD_DIM, (h + 1) * FOX_HEAD_DIM)

    def step(j, carry, masked):
        off = pl.multiple_of(j * FOX_TK, FOX_TK)
        rows = pl.ds(off, FOX_TK)
        acc = [acc_ref[hrows(h), :] for h in hs]
        s = [_dot_nt(ka_ref[h, rows, :], qa_ref[h]) for h in hs]
        if masked:
            krow = lax.broadcasted_iota(jnp.int32, (FOX_TK, FOX_TQ), 0)
            qcol = lax.broadcasted_iota(jnp.int32, (FOX_TK, FOX_TQ), 1)
            s = [jnp.where(qcol >= krow, s[h], -jnp.inf) for h in hs]
        m_new = [jnp.maximum(carry[h][0], jnp.max(s[h], axis=0, keepdims=True)) for h in hs]
        alpha = [jnp.exp(carry[h][0] - m_new[h]) for h in hs]
        p = [jnp.exp(s[h] - m_new[h]) for h in hs]
        l_new = [alpha[h] * carry[h][1] + jnp.sum(p[h], axis=0, keepdims=True) for h in hs]
        pv = [_dot(vt_ref[hrows(h), rows], p[h].astype(BF16)) for h in hs]
        for h in hs:
            acc_ref[hrows(h), :] = alpha[h] * acc[h] + pv[h]
        return tuple((m_new[h], l_new[h]) for h in hs)

    init = tuple((jnp.full((1, FOX_TQ), -jnp.inf, F32), jnp.zeros((1, FOX_TQ), F32)) for _ in hs)
    carry = lax.fori_loop(0, qi, lambda j, c: step(j, c, False), init)
    final = step(qi, carry, True)
    sub = lax.broadcasted_iota(jnp.int32, (LANES, 1), 0)
    for hp in range(FOX_HEADS // 2):
        pair = slice(hp * LANES, (hp + 1) * LANES)
        denom = jnp.where(sub < FOX_HEAD_DIM, final[2 * hp][1], final[2 * hp + 1][1])
        o_ref[:, pair] = (acc_ref[pair, :] / denom).T.astype(BF16)


def _fox(proj, cf, q_gain2, k_gain2):
    nq = SEQ // FOX_TQ
    fb = COL_FOX // BRANCH_WIDTH
    return pl.pallas_call(
        _fox_kernel,
        grid=(BATCH, nq),
        in_specs=[
            pl.BlockSpec((FOX_TQ, BRANCH_WIDTH), lambda b, qi: (b * nq + qi, fb)),
            pl.BlockSpec((SEQ, BRANCH_WIDTH), lambda b, qi: (b, fb + 1)),
            pl.BlockSpec((SEQ, BRANCH_WIDTH), lambda b, qi: (b, fb + 2)),
            pl.BlockSpec((1, FOX_TQ, LANES), lambda b, qi: (b, qi, 0)),
            pl.BlockSpec((1, SEQ, LANES), lambda b, qi: (b, 0, 0)),
            pl.BlockSpec((1, LANES), lambda b, qi: (0, 0)),
            pl.BlockSpec((1, LANES), lambda b, qi: (0, 0)),
        ],
        out_specs=pl.BlockSpec((FOX_TQ, BRANCH_WIDTH), lambda b, qi: (b * nq + qi, 0)),
        out_shape=jax.ShapeDtypeStruct((TOKENS, BRANCH_WIDTH), BF16),
        scratch_shapes=[
            pltpu.VMEM((FOX_HEADS, SEQ, LANES), BF16),
            pltpu.VMEM((BRANCH_WIDTH, SEQ), BF16),
            pltpu.VMEM((FOX_HEADS, FOX_TQ, LANES), BF16),
            pltpu.VMEM((BRANCH_WIDTH, FOX_TQ), F32),
        ],
        compiler_params=_cparams(("parallel", "arbitrary")),
        name="fox",
    )(proj, proj, proj, cf, cf, q_gain2, k_gain2)


DN_STREAMS = 2

def _dn_kernel(q_ref, k_ref, v_ref, z_ref, sm_ref, wq_ref, wk_ref, wv_ref, alog_ref, dtb_ref,
               ng_ref, o_ref, u_ref, kcum_ref, qdec_ref, kdec_ref, qk_ref, gt_ref, st_ref):
    c = DN_CHUNK
    ri = lax.broadcasted_iota(jnp.int32, (c, c), 0)
    ci = lax.broadcasted_iota(jnp.int32, (c, c), 1)
    incl = ri >= ci
    strict = ri > ci
    eye = ri == ci
    eye_f = eye.astype(F32)
    tri = incl.astype(F32)
    lane = lax.broadcasted_iota(jnp.int32, (1, LANES), 1)

    def conv_silu(cur, prev, w):
        cat = jnp.concatenate([prev, cur], axis=0)
        y = (w[3:4] * cur + w[2:3] * _shifted(cat, 1) + w[1:2] * _shifted(cat, 2)
             + w[0:1] * _shifted(cat, 3))
        return y * _sigmoid(y)

    def l2norm(x):
        return x * lax.rsqrt(jnp.sum(x * x, axis=-1, keepdims=True) + EPS)

    def heads():
        return [slice(h * DN_HEAD_DIM, (h + 1) * DN_HEAD_DIM) for h in range(DN_HEADS)]

    def prepare(n, carry):
        units = []
        raw, gcum_all, beta_all = [], [], []
        for stream in range(DN_STREAMS):
            chunk = n + stream * (N_CHUNKS // DN_STREAMS)
            r0 = pl.multiple_of(chunk * c, c)
            rows = pl.ds(r0, c)
            prow = pl.ds(pl.multiple_of(jnp.maximum(r0 - HALO, 0), HALO), HALO)
            keep_prev = jnp.where(chunk > 0, 1.0, 0.0)
            sm = sm_ref[rows, :]
            bsig = _sigmoid(sm)
            gsum = _dot(tri, -jnp.exp(alog_ref[...]) * _softplus(sm + dtb_ref[...]), HI)
            for h, cols in enumerate(heads()):
                units.append((chunk, rows, h, cols))
                raw.append([(ref[rows, cols].astype(F32), ref[prow, cols].astype(F32) * keep_prev,
                             w_ref[:, cols])
                            for ref, w_ref in ((q_ref, wq_ref), (k_ref, wk_ref), (v_ref, wv_ref))])
                gcum_all.append(gsum)
                beta_all.append(bsig)
        hs = range(len(units))
        gc, beta, q, k, v, decay = [], [], [], [], [], []
        for u in hs:
            head = units[u][2]
            g1 = jnp.sum(jnp.where(lane == SM_A + head, gcum_all[u], 0.0), axis=-1, keepdims=True)
            b1 = jnp.sum(jnp.where(lane == SM_BETA + head, beta_all[u], 0.0), axis=-1, keepdims=True)
            gc.append(jnp.broadcast_to(g1, (c, DN_HEAD_DIM)))
            beta.append(jnp.broadcast_to(b1, (c, DN_HEAD_DIM)))
            q.append(l2norm(conv_silu(*raw[u][0])) * (DN_HEAD_DIM ** -0.5))
            k.append(l2norm(conv_silu(*raw[u][1])))
            v.append(conv_silu(*raw[u][2]))
            gc_sq = gc[u][:, :c]
            g_row = jnp.sum(jnp.where(eye, gc_sq, 0.0), axis=0, keepdims=True)
            decay.append(jnp.exp(jnp.where(incl, gc_sq - g_row, -jnp.inf)))
        kb = [k[h] * beta[h] for h in hs]
        k16 = [k[h].astype(BF16) for h in hs]
        kk = [_dot_nt(kb[h].astype(BF16), k16[h]) for h in hs]
        qk = [_dot_nt(q[h].astype(BF16), k16[h]) for h in hs]
        neg = [jnp.where(strict, -(kk[h] * decay[h]), 0.0) for h in hs]
        qsum = [eye_f + neg[h] for h in hs]
        neg16 = [neg[h].astype(BF16) for h in hs]
        power = [_dot(neg16[h], neg16[h]) for h in hs]
        for _ in range(4):
            both = [_dot(power[h].astype(BF16),
                         jnp.concatenate([qsum[h], power[h]], axis=1).astype(BF16)) for h in hs]
            qsum = [qsum[h] + both[h][:, :c] for h in hs]
            power = [both[h][:, c:] for h in hs]
        last = [_dot(power[h].astype(BF16), qsum[h].astype(BF16)) for h in hs]
        eg = [jnp.exp(gc[h]) for h in hs]
        rhs = [jnp.concatenate([v[h] * beta[h], kb[h] * eg[h]], axis=1) for h in hs]
        sol = [rhs[h] + _dot((qsum[h] + last[h] - eye_f).astype(BF16), rhs[h].astype(BF16)) for h in hs]
        for u, (chunk, rows, head, cols) in enumerate(units):
            g_last = gc[u][c - 1:c, :]
            u_ref[rows, cols] = sol[u][:, :DN_HEAD_DIM]
            kcum_ref[rows, cols] = sol[u][:, DN_HEAD_DIM:].astype(BF16)
            qdec_ref[rows, cols] = (q[u] * eg[u]).astype(BF16)
            kdec_ref[rows, cols] = (k[u] * jnp.exp(g_last - gc[u])).astype(BF16)
            qk_ref[rows, head * c:(head + 1) * c] = jnp.where(incl, qk[u] * decay[u], 0.0).astype(BF16)
            gt_ref[chunk, :, cols] = jnp.broadcast_to(jnp.exp(g_last), (SUBLANES, DN_HEAD_DIM))
        return carry

    lax.fori_loop(0, N_CHUNKS // DN_STREAMS, prepare, 0)

    st_ref[...] = jnp.zeros_like(st_ref)

    def scan(n, carry):
        rows = pl.ds(pl.multiple_of(n * c, c), c)
        loaded = [(st_ref[h], u_ref[rows, cols], kcum_ref[rows, cols], qdec_ref[rows, cols],
                   qk_ref[rows, h * c:(h + 1) * c], kdec_ref[rows, cols], gt_ref[n, 0:1, cols],
                   z_ref[rows, cols].astype(F32)) for h, cols in enumerate(heads())]
        hs = range(DN_HEADS)
        state16 = [loaded[h][0].astype(BF16) for h in hs]
        v16 = [(loaded[h][1] - _dot(loaded[h][2], state16[h])).astype(BF16) for h in hs]
        upd = [_dot_tn(loaded[h][5], v16[h]) for h in hs]
        out = [_dot(loaded[h][3], state16[h]) + _dot(loaded[h][4], v16[h]) for h in hs]
        for h, cols in enumerate(heads()):
            z = loaded[h][7]
            st_ref[h] = loaded[h][0] * loaded[h][6] + upd[h]
            o_ref[rows, cols] = (_rmsnorm(out[h], ng_ref[...]) * (z * _sigmoid(z))).astype(BF16)
        return carry

    lax.fori_loop(0, N_CHUNKS, scan, 0)


def _deltanet(proj, small, conv_w, alog_row, dtb_row, norm_gain):
    qb = COL_DN // BRANCH_WIDTH
    zb = COL_DNZ // BRANCH_WIDTH

    def colspec(blk):
        return pl.BlockSpec((SEQ, BRANCH_WIDTH), lambda b: (b, blk))

    def wspec(blk):
        return pl.BlockSpec((4, BRANCH_WIDTH), lambda b: (0, blk))

    row = pl.BlockSpec((1, LANES), lambda b: (0, 0))
    return pl.pallas_call(
        _dn_kernel,
        grid=(BATCH,),
        in_specs=[
            colspec(qb), colspec(qb + 1), colspec(qb + 2), colspec(zb),
            pl.BlockSpec((SEQ, LANES), lambda b: (b, 0)),
            wspec(0), wspec(1), wspec(2),
            row, row, row,
        ],
        out_specs=pl.BlockSpec((SEQ, BRANCH_WIDTH), lambda b: (b, 0)),
        out_shape=jax.ShapeDtypeStruct((TOKENS, BRANCH_WIDTH), BF16),
        scratch_shapes=[
            pltpu.VMEM((SEQ, BRANCH_WIDTH), F32),
            pltpu.VMEM((SEQ, BRANCH_WIDTH), BF16),
            pltpu.VMEM((SEQ, BRANCH_WIDTH), BF16),
            pltpu.VMEM((SEQ, BRANCH_WIDTH), BF16),
            pltpu.VMEM((SEQ, DN_HEADS * DN_CHUNK), BF16),
            pltpu.VMEM((N_CHUNKS, SUBLANES, BRANCH_WIDTH), F32),
            pltpu.VMEM((DN_HEADS, DN_HEAD_DIM, DN_HEAD_DIM), F32),
        ],
        compiler_params=_cparams(("parallel",)),
        name="deltanet",
    )(proj, proj, proj, proj, small, conv_w, conv_w, conv_w, alog_row, dtb_row, norm_gain)


def _merge_kernel(yf_ref, sb_ref, sc_ref, sv_ref, scp_ref, svp_ref, yd_ref, g0_ref, g1_ref, g2_ref,
                  x_ref, wb_ref, wo_ref, cw_ref, o_ref):
    i = pl.program_id(0)
    keep_prev = jnp.where(i % (SEQ // MERGE_TM) != 0, 1.0, 0.0)
    cur = sc_ref[...].astype(F32) * sv_ref[...].astype(F32)
    prev = scp_ref[...].astype(F32) * svp_ref[...].astype(F32) * keep_prev
    cat = jnp.concatenate([prev, cur], axis=0)
    cw = cw_ref[...]
    conv = cw[2:3] * cur + cw[1:2] * _shifted(cat, 1) + cw[0:1] * _shifted(cat, 2)
    y_sc = sb_ref[...].astype(F32) * conv
    merged = (_sigmoid(g0_ref[...].astype(F32)) * _dot(yf_ref[...], wb_ref[0])
              + _sigmoid(g1_ref[...].astype(F32)) * _dot(y_sc.astype(BF16), wb_ref[1])
              + _sigmoid(g2_ref[...].astype(F32)) * _dot(yd_ref[...], wb_ref[2]))
    o_ref[...] = x_ref[...] + _dot(merged.astype(BF16), wo_ref[...])


def _merge(x, proj, y_fox, y_dn, w_branch, w_o, sc_conv_w):
    tm = MERGE_TM
    scb = COL_SC // BRANCH_WIDTH
    gb = COL_GATE // D_MODEL
    hpb = tm // HALO

    def prev_rows(i):
        return jnp.maximum(i * hpb - 1, 0)

    return pl.pallas_call(
        _merge_kernel,
        grid=(TOKENS // tm,),
        in_specs=[
            pl.BlockSpec((tm, BRANCH_WIDTH), lambda i: (i, 0)),
            pl.BlockSpec((tm, BRANCH_WIDTH), lambda i: (i, scb)),
            pl.BlockSpec((tm, BRANCH_WIDTH), lambda i: (i, scb + 1)),
            pl.BlockSpec((tm, BRANCH_WIDTH), lambda i: (i, scb + 2)),
            pl.BlockSpec((HALO, BRANCH_WIDTH), lambda i: (prev_rows(i), scb + 1)),
            pl.BlockSpec((HALO, BRANCH_WIDTH), lambda i: (prev_rows(i), scb + 2)),
            pl.BlockSpec((tm, BRANCH_WIDTH), lambda i: (i, 0)),
            pl.BlockSpec((tm, D_MODEL), lambda i: (i, gb)),
            pl.BlockSpec((tm, D_MODEL), lambda i: (i, gb + 1)),
            pl.BlockSpec((tm, D_MODEL), lambda i: (i, gb + 2)),
            pl.BlockSpec((tm, D_MODEL), lambda i: (i, 0)),
            pl.BlockSpec((3, BRANCH_WIDTH, D_MODEL), lambda i: (0, 0, 0)),
            pl.BlockSpec((D_MODEL, D_MODEL), lambda i: (0, 0)),
            pl.BlockSpec((3, BRANCH_WIDTH), lambda i: (0, 0)),
        ],
        out_specs=pl.BlockSpec((tm, D_MODEL), lambda i: (i, 0)),
        out_shape=jax.ShapeDtypeStruct((TOKENS, D_MODEL), F32),
        compiler_params=_cparams(("parallel",)),
        name="merge",
    )(y_fox, proj, proj, proj, proj, proj, y_dn, proj, proj, proj, x, w_branch, w_o, sc_conv_w)


def _ffn_kernel(x_ref, xp_ref, gf_ref, wg_ref, wv_ref, cg_ref, cv_ref, wd_ref, gp_ref, wpg_ref,
                p_ref, wple_ref, o_ref, hn_ref, act_ref):
    i = pl.program_id(0)
    j = pl.program_id(1)

    @pl.when(j == 0)
    def _():
        keep_prev = jnp.where(i % (SEQ // FFN_TM) != 0, 1.0, 0.0)
        xx = jnp.concatenate([xp_ref[...] * keep_prev, x_ref[...]], axis=0)
        hn_ref[...] = _rmsnorm(xx, gf_ref[...]).astype(BF16)

    hn = hn_ref[...]

    def branch(w_ref, c_ref):
        u = _dot(hn, w_ref[...])
        cw = c_ref[...]
        return cw[2:3] * u[HALO:] + cw[1:2] * _shifted(u, 1) + cw[0:1] * _shifted(u, 2)

    gate = branch(wg_ref, cg_ref)
    val = branch(wv_ref, cv_ref)
    act_ref[:, pl.ds(pl.multiple_of(j * FFN_TF, FFN_TF), FFN_TF)] = (gate * _sigmoid(gate) * val).astype(BF16)

    @pl.when(j == FFN_NF - 1)
    def _():
        x1 = x_ref[...] + _dot(act_ref[...], wd_ref[...])
        h2 = _rmsnorm(x1, gp_ref[...]).astype(BF16)
        pgate = _sigmoid(_dot(h2, wpg_ref[...]))
        emb = _dot(p_ref[...].astype(BF16), wple_ref[...])
        o_ref[...] = x1 + pgate * emb


def _ffn(x, p, g_ffn, w_up, conv_w, w_down, g_ple, w_pg, w_ple):
    tm = FFN_TM
    hpb = tm // HALO
    return pl.pallas_call(
        _ffn_kernel,
        grid=(TOKENS // tm, FFN_NF),
        in_specs=[
            pl.BlockSpec((tm, D_MODEL), lambda i, j: (i, 0)),
            pl.BlockSpec((HALO, D_MODEL), lambda i, j: (jnp.maximum(i * hpb - 1, 0), 0)),
            pl.BlockSpec((1, D_MODEL), lambda i, j: (0, 0)),
            pl.BlockSpec((D_MODEL, FFN_TF), lambda i, j: (0, j)),
            pl.BlockSpec((D_MODEL, FFN_TF), lambda i, j: (0, FFN_NF + j)),
            pl.BlockSpec((3, FFN_TF), lambda i, j: (0, j)),
            pl.BlockSpec((3, FFN_TF), lambda i, j: (0, FFN_NF + j)),
            pl.BlockSpec((D_FF, D_MODEL), lambda i, j: (0, 0)),
            pl.BlockSpec((1, D_MODEL), lambda i, j: (0, 0)),
            pl.BlockSpec((D_MODEL, D_MODEL), lambda i, j: (0, 0)),
            pl.BlockSpec((tm, PLE_DIM), lambda i, j: (i, 0)),
            pl.BlockSpec((PLE_DIM, D_MODEL), lambda i, j: (0, 0)),
        ],
        out_specs=pl.BlockSpec((tm, D_MODEL), lambda i, j: (i, 0)),
        out_shape=jax.ShapeDtypeStruct((TOKENS, D_MODEL), F32),
        scratch_shapes=[pltpu.VMEM((tm + HALO, D_MODEL), BF16), pltpu.VMEM((tm, D_FF), BF16)],
        compiler_params=_cparams(("parallel", "arbitrary")),
        name="ffn",
    )(x, x, g_ffn, w_up, w_up, conv_w, conv_w, w_down, g_ple, w_pg, p, w_ple)


def _lane_row(values, offset):
    row = jnp.zeros((1, LANES), F32)
    return lax.dynamic_update_slice(row, values.astype(F32)[None, :], (0, offset))


def _layer(x, p_i, g_mix, w_in, b_fox_f, fox_q_gain, fox_k_gain, sc_conv_w, dn_conv_w, dn_a_log,
           dn_dt_bias, dn_norm_gain, w_branch, w_o, g_ffn, w_up, ffn_conv_w, w_down, g_ple,
           w_ple_gate, w_ple):
    o_f = 3 * BRANCH_WIDTH
    o_sc = o_f + FOX_HEADS
    o_dn = o_sc + 3 * BRANCH_WIDTH
    o_b = o_dn + 3 * BRANCH_WIDTH
    o_a = o_b + DN_HEADS
    o_z = o_a + DN_HEADS
    o_g = o_z + BRANCH_WIDTH
    w_big = jnp.concatenate([w_in[:, :o_f], w_in[:, o_sc:o_dn], w_in[:, o_dn:o_b], w_in[:, o_z:o_g],
                             w_in[:, o_g:]], axis=1).astype(BF16)
    w_small = jnp.concatenate([w_in[:, o_f:o_sc], w_in[:, o_b:o_a], w_in[:, o_a:o_z],
                               jnp.zeros((D_MODEL, LANES - FOX_HEADS - 2 * DN_HEADS), F32)],
                              axis=1).astype(BF16)

    proj, small = _proj(x, g_mix[None, :], w_big, w_small)
    cf = _cumf(small, _lane_row(b_fox_f, SM_FOXF))
    y_fox = _fox(proj, cf, jnp.tile(fox_q_gain, 2)[None, :], jnp.tile(fox_k_gain, 2)[None, :])
    y_dn = _deltanet(proj, small, dn_conv_w, _lane_row(dn_a_log, SM_A), _lane_row(dn_dt_bias, SM_A),
                     dn_norm_gain[None, :])
    x = _merge(x, proj, y_fox, y_dn, w_branch.astype(BF16), w_o.astype(BF16), sc_conv_w)
    x = _ffn(x, p_i, g_ffn[None, :], w_up.astype(BF16), ffn_conv_w, w_down.astype(BF16),
             g_ple[None, :], w_ple_gate.astype(BF16), w_ple.astype(BF16))
    return x


def kernel(x, p, g_mix, w_in, b_fox_f, fox_q_gain, fox_k_gain, sc_conv_w, dn_conv_w, dn_a_log,
           dn_dt_bias, dn_norm_gain, w_branch, w_o, g_ffn, w_up, ffn_conv_w, w_down, g_ple,
           w_ple_gate, w_ple):
    depth = p.shape[0]
    x = x.reshape(TOKENS, D_MODEL)
    p = p.reshape(depth, TOKENS, PLE_DIM)
    for i in range(depth):
        x = _layer(x, p[i], g_mix[i], w_in[i], b_fox_f[i], fox_q_gain[i], fox_k_gain[i],
                   sc_conv_w[i], dn_conv_w[i], dn_a_log[i], dn_dt_bias[i], dn_norm_gain[i],
                   w_branch[i], w_o[i], g_ffn[i], w_up[i], ffn_conv_w[i], w_down[i], g_ple[i],
                   w_ple_gate[i], w_ple[i])
    return x.reshape(BATCH, SEQ, D_MODEL)
```

```python
import jax
import jax.numpy as jnp
import numpy as np
from jax import lax
from jax.experimental import pallas as pl
from jax.experimental.pallas import tpu as pltpu

F32 = jnp.float32
BF16 = jnp.bfloat16
HI = lax.Precision.HIGHEST

D_MODEL = 1024
BATCH = 8
SEQ = 2048
TOKENS = BATCH * SEQ
PLE_DIM = 256
BRANCH_WIDTH = 512
FOX_HEADS = 8
FOX_HEAD_DIM = 64
DN_HEADS = 4
DN_HEAD_DIM = 128
DN_CHUNK = 64
N_CHUNKS = SEQ // DN_CHUNK
D_FF = 2816
EPS = 1e-6

LANES = 128
SUBLANES = 8
HALO = 16

PROJ_WIDTH = 8192
COL_FOX = 0
COL_SC = 1536
COL_DN = 3072
COL_DNZ = 4608
COL_GATE = 5120
SM_FOXF = 0
SM_BETA = 8
SM_A = 12

PROJ_TM = 1024
PROJ_TN = 2048
FOX_TQ = 512
FOX_TK = 512
MERGE_TM = 512
FFN_TM = 512
FFN_TF = 256
FFN_NF = D_FF // FFN_TF

VMEM_LIMIT = 48 * 1024 * 1024


def _cparams(sem):
    return pltpu.CompilerParams(dimension_semantics=sem, vmem_limit_bytes=VMEM_LIMIT)


def _sigmoid(x):
    return 1.0 / (1.0 + jnp.exp(-x))


def _softplus(x):
    return jnp.maximum(x, 0.0) + jnp.log1p(jnp.exp(-jnp.abs(x)))


def _dot(a, b, precision=None):
    return jnp.dot(a, b, preferred_element_type=F32, precision=precision)


def _dot_nt(a, b):
    return lax.dot_general(a, b, (((1,), (1,)), ((), ())), preferred_element_type=F32)


def _dot_tn(a, b):
    return lax.dot_general(a, b, (((0,), (0,)), ((), ())), preferred_element_type=F32)


def _rmsnorm(x, gain):
    ms = jnp.mean(x * x, axis=-1, keepdims=True)
    return x * lax.rsqrt(ms + EPS) * gain


def _shifted(cat, shift):
    if shift == 0:
        return cat[HALO:]
    return pltpu.roll(cat, shift, 0)[HALO:]


def _bf16_split3(x):
    hi = x.astype(BF16).astype(F32)
    r = x - hi
    mid = r.astype(BF16).astype(F32)
    lo = (r - mid).astype(BF16).astype(F32)
    return hi, mid, lo


def _proj_kernel(x_ref, g_ref, w_ref, ws_ref, o_ref, os_ref, hn_ref):
    @pl.when(pl.program_id(1) == 0)
    def _():
        hn = _rmsnorm(x_ref[...], g_ref[...]).astype(BF16)
        hn_ref[...] = hn
        os_ref[...] = _dot(hn, ws_ref[...])

    o_ref[...] = _dot(hn_ref[...], w_ref[...]).astype(BF16)


def _proj(x, gain, w_big, w_small):
    grid = (TOKENS // PROJ_TM, PROJ_WIDTH // PROJ_TN)
    return pl.pallas_call(
        _proj_kernel,
        grid=grid,
        in_specs=[
            pl.BlockSpec((PROJ_TM, D_MODEL), lambda i, j: (i, 0)),
            pl.BlockSpec((1, D_MODEL), lambda i, j: (0, 0)),
            pl.BlockSpec((D_MODEL, PROJ_TN), lambda i, j: (0, j)),
            pl.BlockSpec((D_MODEL, LANES), lambda i, j: (0, 0)),
        ],
        out_specs=[
            pl.BlockSpec((PROJ_TM, PROJ_TN), lambda i, j: (i, j)),
            pl.BlockSpec((PROJ_TM, LANES), lambda i, j: (i, 0)),
        ],
        out_shape=[
            jax.ShapeDtypeStruct((TOKENS, PROJ_WIDTH), BF16),
            jax.ShapeDtypeStruct((TOKENS, LANES), F32),
        ],
        scratch_shapes=[pltpu.VMEM((PROJ_TM, D_MODEL), BF16)],
        compiler_params=_cparams(("parallel", "arbitrary")),
        name="proj",
    )(x, gain, w_big, w_small)


CUMF_BLK = 128


def _cumf_kernel(sm_ref, bf_ref, cf_ref):
    r = lax.broadcasted_iota(jnp.int32, (CUMF_BLK, CUMF_BLK), 0)
    c = lax.broadcasted_iota(jnp.int32, (CUMF_BLK, CUMF_BLK), 1)
    tri = (r >= c).astype(F32)
    carry = jnp.zeros((1, LANES), F32)
    for blk in range(SEQ // CUMF_BLK):
        rows = slice(blk * CUMF_BLK, (blk + 1) * CUMF_BLK)
        log_f = -_softplus(-(sm_ref[rows, :] + bf_ref[...]))
        cum = _dot(tri, log_f, HI) + carry
        carry = cum[CUMF_BLK - 1:CUMF_BLK, :]
        cf_ref[0, rows, :] = cum


def _cumf(small, bias_row):
    return pl.pallas_call(
        _cumf_kernel,
        grid=(BATCH,),
        in_specs=[
            pl.BlockSpec((SEQ, LANES), lambda b: (b, 0)),
            pl.BlockSpec((1, LANES), lambda b: (0, 0)),
        ],
        out_specs=pl.BlockSpec((1, SEQ, LANES), lambda b: (b, 0, 0)),
        out_shape=jax.ShapeDtypeStruct((BATCH, SEQ, LANES), F32),
        compiler_params=_cparams(("parallel",)),
        name="cumf",
    )(small, bias_row)


AUG = FOX_HEAD_DIM


def _fox_selectors():
    sel = np.zeros((2, FOX_HEADS // 2, 4 * LANES, 2 * LANES), np.float32)
    ones = np.zeros((2, 1, 2 * LANES), np.float32)
    for hp in range(FOX_HEADS // 2):
        for e in range(2):
            base = e * LANES
            for d in range(FOX_HEAD_DIM):
                sel[:, hp, e * FOX_HEAD_DIM + d, base + d] = 1.0
            for part in range(3):
                src = (1 + part) * LANES + SM_FOXF + 2 * hp + e
                sel[0, hp, src, base + AUG + part] = 1.0
                sel[1, hp, src, base + AUG + 3 + part] = -1.0
    for e in range(2):
        ones[0, 0, e * LANES + AUG + 3:e * LANES + AUG + 6] = 1.0
        ones[1, 0, e * LANES + AUG:e * LANES + AUG + 3] = 1.0
    return jnp.asarray(sel, BF16), jnp.asarray(ones, F32)


def _fox_kernel(q_ref, k_ref, v_ref, cfq_ref, cfk_ref, qg_ref, kg_ref, sel_ref, ones_ref, o_ref, ka_ref,
                vt_ref, qa_ref, acc_ref):
    qi = pl.program_id(1)
    lane = lax.broadcasted_iota(jnp.int32, (1, LANES), 1)
    first_head = lane < FOX_HEAD_DIM

    def headnorm(x, gain):
        x2 = x * x
        s0 = jnp.sum(jnp.where(first_head, x2, 0.0), axis=-1, keepdims=True)
        s1 = jnp.sum(jnp.where(first_head, 0.0, x2), axis=-1, keepdims=True)
        inv = jnp.where(first_head,
                        lax.rsqrt(s0 * (1.0 / FOX_HEAD_DIM) + EPS),
                        lax.rsqrt(s1 * (1.0 / FOX_HEAD_DIM) + EPS))
        return x * inv * gain

    def cum_parts(cf):
        return [part.astype(BF16) for part in _bf16_split3(cf)]

    def augmented(xn, parts, hp, side):
        packed = jnp.concatenate([xn.astype(BF16)] + parts, axis=1)
        return (_dot(packed, sel_ref[side, hp]) + ones_ref[side]).astype(BF16)

    @pl.when(qi == 0)
    def _():
        def body(c, carry):
            off = pl.multiple_of(c * FOX_TK, FOX_TK)
            rows = pl.ds(off, FOX_TK)
            parts = cum_parts(cfk_ref[0, rows, :])
            for hp in range(FOX_HEADS // 2):
                pair = slice(hp * LANES, (hp + 1) * LANES)
                kn = headnorm(k_ref[rows, pair].astype(F32), kg_ref[...])
                ka = augmented(kn, parts, hp, 1)
                ka_ref[2 * hp, rows, :] = ka[:, :LANES]
                ka_ref[2 * hp + 1, rows, :] = ka[:, LANES:]
                vt_ref[pair, rows] = v_ref[rows, pair].astype(F32).T.astype(BF16)
            return carry
        lax.fori_loop(0, SEQ // FOX_TK, body, 0)

    parts = cum_parts(cfq_ref[0])
    for hp in range(FOX_HEADS // 2):
        pair = slice(hp * LANES, (hp + 1) * LANES)
        qn = headnorm(q_ref[:, pair].astype(F32), qg_ref[...]) * (FOX_HEAD_DIM ** -0.5)
        qa = augmented(qn, parts, hp, 0)
        qa_ref[2 * hp] = qa[:, :LANES]
        qa_ref[2 * hp + 1] = qa[:, LANES:]
    acc_ref[...] = jnp.zeros_like(acc_ref)

    hs = range(FOX_HEADS)

    def hrows(h):
        return slice(h * FOX_HEAD_DIM, (h + 1) * FOX_HEAD_DIM)

    def step(j, carry, masked):
        off = pl.multiple_of(j * FOX_TK, FOX_TK)
        rows = pl.ds(off, FOX_TK)
        acc = [acc_ref[hrows(h), :] for h in hs]
        s = [_dot_nt(ka_ref[h, rows, :], qa_ref[h]) for h in hs]
        if masked:
            krow = lax.broadcasted_iota(jnp.int32, (FOX_TK, FOX_TQ), 0)
            qcol = lax.broadcasted_iota(jnp.int32, (FOX_TK, FOX_TQ), 1)
            s = [jnp.where(qcol >= krow, s[h], -jnp.inf) for h in hs]
        m_new = [jnp.maximum(carry[h][0], jnp.max(s[h], axis=0, keepdims=True)) for h in hs]
        alpha = [jnp.exp(carry[h][0] - m_new[h]) for h in hs]
        p = [jnp.exp(s[h] - m_new[h]) for h in hs]
        l_new = [alpha[h] * carry[h][1] + jnp.sum(p[h], axis=0, keepdims=True) for h in hs]
        pv = [_dot(vt_ref[hrows(h), rows], p[h].astype(BF16)) for h in hs]
        for h in hs:
            acc_ref[hrows(h), :] = alpha[h] * acc[h] + pv[h]
        return tuple((m_new[h], l_new[h]) for h in hs)

    init = tuple((jnp.full((1, FOX_TQ), -jnp.inf, F32), jnp.zeros((1, FOX_TQ), F32)) for _ in hs)
    carry = lax.fori_loop(0, qi, lambda j, c: step(j, c, False), init)
    final = step(qi, carry, True)
    sub = lax.broadcasted_iota(jnp.int32, (LANES, 1), 0)
    for hp in range(FOX_HEADS // 2):
        pair = slice(hp * LANES, (hp + 1) * LANES)
        denom = jnp.where(sub < FOX_HEAD_DIM, final[2 * hp][1], final[2 * hp + 1][1])
        o_ref[:, pair] = (acc_ref[pair, :] / denom).T.astype(BF16)


def _fox(proj, cf, q_gain2, k_gain2):
    nq = SEQ // FOX_TQ
    fb = COL_FOX // BRANCH_WIDTH
    sel, ones = _fox_selectors()
    return pl.pallas_call(
        _fox_kernel,
        grid=(BATCH, nq),
        in_specs=[
            pl.BlockSpec((FOX_TQ, BRANCH_WIDTH), lambda b, qi: (b * nq + qi, fb)),
            pl.BlockSpec((SEQ, BRANCH_WIDTH), lambda b, qi: (b, fb + 1)),
            pl.BlockSpec((SEQ, BRANCH_WIDTH), lambda b, qi: (b, fb + 2)),
            pl.BlockSpec((1, FOX_TQ, LANES), lambda b, qi: (b, qi, 0)),
            pl.BlockSpec((1, SEQ, LANES), lambda b, qi: (b, 0, 0)),
            pl.BlockSpec((1, LANES), lambda b, qi: (0, 0)),
            pl.BlockSpec((1, LANES), lambda b, qi: (0, 0)),
            pl.BlockSpec(sel.shape, lambda b, qi: (0, 0, 0, 0)),
            pl.BlockSpec(ones.shape, lambda b, qi: (0, 0, 0)),
        ],
        out_specs=pl.BlockSpec((FOX_TQ, BRANCH_WIDTH), lambda b, qi: (b * nq + qi, 0)),
        out_shape=jax.ShapeDtypeStruct((TOKENS, BRANCH_WIDTH), BF16),
        scratch_shapes=[
            pltpu.VMEM((FOX_HEADS, SEQ, LANES), BF16),
            pltpu.VMEM((BRANCH_WIDTH, SEQ), BF16),
            pltpu.VMEM((FOX_HEADS, FOX_TQ, LANES), BF16),
            pltpu.VMEM((BRANCH_WIDTH, FOX_TQ), F32),
        ],
        compiler_params=_cparams(("parallel", "arbitrary")),
        name="fox",
    )(proj, proj, proj, cf, cf, q_gain2, k_gain2, sel, ones)


DN_STREAMS = 2

def _dn_kernel(q_ref, k_ref, v_ref, z_ref, sm_ref, wq_ref, wk_ref, wv_ref, alog_ref, dtb_ref,
               ng_ref, o_ref, u_ref, kcum_ref, qdec_ref, kdec_ref, qk_ref, gt_ref, st_ref):
    c = DN_CHUNK
    ri = lax.broadcasted_iota(jnp.int32, (c, c), 0)
    ci = lax.broadcasted_iota(jnp.int32, (c, c), 1)
    incl = ri >= ci
    strict = ri > ci
    eye = ri == ci
    eye_f = eye.astype(F32)
    tri = incl.astype(F32)
    lane = lax.broadcasted_iota(jnp.int32, (1, LANES), 1)

    def conv_silu(cur, prev, w):
        cat = jnp.concatenate([prev, cur], axis=0)
        y = (w[3:4] * cur + w[2:3] * _shifted(cat, 1) + w[1:2] * _shifted(cat, 2)
             + w[0:1] * _shifted(cat, 3))
        return y * _sigmoid(y)

    def l2norm(x):
        return x * lax.rsqrt(jnp.sum(x * x, axis=-1, keepdims=True) + EPS)

    def heads():
        return [slice(h * DN_HEAD_DIM, (h + 1) * DN_HEAD_DIM) for h in range(DN_HEADS)]

    def prepare(n, carry):
        units = []
        raw, gcum_all, beta_all = [], [], []
        for stream in range(DN_STREAMS):
            chunk = n + stream * (N_CHUNKS // DN_STREAMS)
            r0 = pl.multiple_of(chunk * c, c)
            rows = pl.ds(r0, c)
            prow = pl.ds(pl.multiple_of(jnp.maximum(r0 - HALO, 0), HALO), HALO)
            keep_prev = jnp.where(chunk > 0, 1.0, 0.0)
            sm = sm_ref[rows, :]
            bsig = _sigmoid(sm)
            gsum = _dot(tri, -jnp.exp(alog_ref[...]) * _softplus(sm + dtb_ref[...]), HI)
            for h, cols in enumerate(heads()):
                units.append((chunk, rows, h, cols))
                raw.append([(ref[rows, cols].astype(F32), ref[prow, cols].astype(F32) * keep_prev,
                             w_ref[:, cols])
                            for ref, w_ref in ((q_ref, wq_ref), (k_ref, wk_ref), (v_ref, wv_ref))])
                gcum_all.append(gsum)
                beta_all.append(bsig)
        hs = range(len(units))
        gc, beta, q, k, v, decay = [], [], [], [], [], []
        for u in hs:
            head = units[u][2]
            g1 = jnp.sum(jnp.where(lane == SM_A + head, gcum_all[u], 0.0), axis=-1, keepdims=True)
            b1 = jnp.sum(jnp.where(lane == SM_BETA + head, beta_all[u], 0.0), axis=-1, keepdims=True)
            gc.append(jnp.broadcast_to(g1, (c, DN_HEAD_DIM)))
            beta.append(jnp.broadcast_to(b1, (c, DN_HEAD_DIM)))
            q.append(l2norm(conv_silu(*raw[u][0])) * (DN_HEAD_DIM ** -0.5))
            k.append(l2norm(conv_silu(*raw[u][1])))
            v.append(conv_silu(*raw[u][2]))
            gc_sq = gc[u][:, :c]
            g_row = jnp.sum(jnp.where(eye, gc_sq, 0.0), axis=0, keepdims=True)
            decay.append(jnp.exp(jnp.where(incl, gc_sq - g_row, -jnp.inf)))
        kb = [k[h] * beta[h] for h in hs]
        k16 = [k[h].astype(BF16) for h in hs]
        kk = [_dot_nt(kb[h].astype(BF16), k16[h]) for h in hs]
        qk = [_dot_nt(q[h].astype(BF16), k16[h]) for h in hs]
        neg = [jnp.where(strict, -(kk[h] * decay[h]), 0.0) for h in hs]
        qsum = [eye_f + neg[h] for h in hs]
        neg16 = [neg[h].astype(BF16) for h in hs]
        power = [_dot(neg16[h], neg16[h]) for h in hs]
        for _ in range(4):
            both = [_dot(power[h].astype(BF16),
                         jnp.concatenate([qsum[h], power[h]], axis=1).astype(BF16)) for h in hs]
            qsum = [qsum[h] + both[h][:, :c] for h in hs]
            power = [both[h][:, c:] for h in hs]
        last = [_dot(power[h].astype(BF16), qsum[h].astype(BF16)) for h in hs]
        eg = [jnp.exp(gc[h]) for h in hs]
        rhs = [jnp.concatenate([v[h] * beta[h], kb[h] * eg[h]], axis=1) for h in hs]
        sol = [rhs[h] + _dot((qsum[h] + last[h] - eye_f).astype(BF16), rhs[h].astype(BF16)) for h in hs]
        for u, (chunk, rows, head, cols) in enumerate(units):
            g_last = gc[u][c - 1:c, :]
            u_ref[rows, cols] = sol[u][:, :DN_HEAD_DIM]
            kcum_ref[rows, cols] = sol[u][:, DN_HEAD_DIM:].astype(BF16)
            qdec_ref[rows, cols] = (q[u] * eg[u]).astype(BF16)
            kdec_ref[rows, cols] = (k[u] * jnp.exp(g_last - gc[u])).astype(BF16)
            qk_ref[rows, head * c:(head + 1) * c] = jnp.where(incl, qk[u] * decay[u], 0.0).astype(BF16)
            gt_ref[chunk, :, cols] = jnp.broadcast_to(jnp.exp(g_last), (SUBLANES, DN_HEAD_DIM))
        return carry

    lax.fori_loop(0, N_CHUNKS // DN_STREAMS, prepare, 0)

    st_ref[...] = jnp.zeros_like(st_ref)

    def scan(n, carry):
        rows = pl.ds(pl.multiple_of(n * c, c), c)
        loaded = [(st_ref[h], u_ref[rows, cols], kcum_ref[rows, cols], qdec_ref[rows, cols],
                   qk_ref[rows, h * c:(h + 1) * c], kdec_ref[rows, cols], gt_ref[n, 0:1, cols],
                   z_ref[rows, cols].astype(F32)) for h, cols in enumerate(heads())]
        hs = range(DN_HEADS)
        state16 = [loaded[h][0].astype(BF16) for h in hs]
        v16 = [(loaded[h][1] - _dot(loaded[h][2], state16[h])).astype(BF16) for h in hs]
        upd = [_dot_tn(loaded[h][5], v16[h]) for h in hs]
        out = [_dot(loaded[h][3], state16[h]) + _dot(loaded[h][4], v16[h]) for h in hs]
        for h, cols in enumerate(heads()):
            z = loaded[h][7]
            st_ref[h] = loaded[h][0] * loaded[h][6] + upd[h]
            o_ref[rows, cols] = (_rmsnorm(out[h], ng_ref[...]) * (z * _sigmoid(z))).astype(BF16)
        return carry

    lax.fori_loop(0, N_CHUNKS, scan, 0)


def _deltanet(proj, small, conv_w, alog_row, dtb_row, norm_gain):
    qb = COL_DN // BRANCH_WIDTH
    zb = COL_DNZ // BRANCH_WIDTH

    def colspec(blk):
        return pl.BlockSpec((SEQ, BRANCH_WIDTH), lambda b: (b, blk))

    def wspec(blk):
        return pl.BlockSpec((4, BRANCH_WIDTH), lambda b: (0, blk))

    row = pl.BlockSpec((1, LANES), lambda b: (0, 0))
    return pl.pallas_call(
        _dn_kernel,
        grid=(BATCH,),
        in_specs=[
            colspec(qb), colspec(qb + 1), colspec(qb + 2), colspec(zb),
            pl.BlockSpec((SEQ, LANES), lambda b: (b, 0)),
            wspec(0), wspec(1), wspec(2),
            row, row, row,
        ],
        out_specs=pl.BlockSpec((SEQ, BRANCH_WIDTH), lambda b: (b, 0)),
        out_shape=jax.ShapeDtypeStruct((TOKENS, BRANCH_WIDTH), BF16),
        scratch_shapes=[
            pltpu.VMEM((SEQ, BRANCH_WIDTH), F32),
            pltpu.VMEM((SEQ, BRANCH_WIDTH), BF16),
            pltpu.VMEM((SEQ, BRANCH_WIDTH), BF16),
            pltpu.VMEM((SEQ, BRANCH_WIDTH), BF16),
            pltpu.VMEM((SEQ, DN_HEADS * DN_CHUNK), BF16),
            pltpu.VMEM((N_CHUNKS, SUBLANES, BRANCH_WIDTH), F32),
            pltpu.VMEM((DN_HEADS, DN_HEAD_DIM, DN_HEAD_DIM), F32),
        ],
        compiler_params=_cparams(("parallel",)),
        name="deltanet",
    )(proj, proj, proj, proj, small, conv_w, conv_w, conv_w, alog_row, dtb_row, norm_gain)


def _merge_kernel(yf_ref, sb_ref, sc_ref, sv_ref, scp_ref, svp_ref, yd_ref, g0_ref, g1_ref, g2_ref,
                  x_ref, wb_ref, wo_ref, cw_ref, o_ref):
    i = pl.program_id(0)
    keep_prev = jnp.where(i % (SEQ // MERGE_TM) != 0, 1.0, 0.0)
    cur = sc_ref[...].astype(F32) * sv_ref[...].astype(F32)
    prev = scp_ref[...].astype(F32) * svp_ref[...].astype(F32) * keep_prev
    cat = jnp.concatenate([prev, cur], axis=0)
    cw = cw_ref[...]
    conv = cw[2:3] * cur + cw[1:2] * _shifted(cat, 1) + cw[0:1] * _shifted(cat, 2)
    y_sc = sb_ref[...].astype(F32) * conv
    merged = (_sigmoid(g0_ref[...].astype(F32)) * _dot(yf_ref[...], wb_ref[0])
              + _sigmoid(g1_ref[...].astype(F32)) * _dot(y_sc.astype(BF16), wb_ref[1])
              + _sigmoid(g2_ref[...].astype(F32)) * _dot(yd_ref[...], wb_ref[2]))
    o_ref[...] = x_ref[...] + _dot(merged.astype(BF16), wo_ref[...])


def _merge(x, proj, y_fox, y_dn, w_branch, w_o, sc_conv_w):
    tm = MERGE_TM
    scb = COL_SC // BRANCH_WIDTH
    gb = COL_GATE // D_MODEL
    hpb = tm // HALO

    def prev_rows(i):
        return jnp.maximum(i * hpb - 1, 0)

    return pl.pallas_call(
        _merge_kernel,
        grid=(TOKENS // tm,),
        in_specs=[
            pl.BlockSpec((tm, BRANCH_WIDTH), lambda i: (i, 0)),
            pl.BlockSpec((tm, BRANCH_WIDTH), lambda i: (i, scb)),
            pl.BlockSpec((tm, BRANCH_WIDTH), lambda i: (i, scb + 1)),
            pl.BlockSpec((tm, BRANCH_WIDTH), lambda i: (i, scb + 2)),
            pl.BlockSpec((HALO, BRANCH_WIDTH), lambda i: (prev_rows(i), scb + 1)),
            pl.BlockSpec((HALO, BRANCH_WIDTH), lambda i: (prev_rows(i), scb + 2)),
            pl.BlockSpec((tm, BRANCH_WIDTH), lambda i: (i, 0)),
            pl.BlockSpec((tm, D_MODEL), lambda i: (i, gb)),
            pl.BlockSpec((tm, D_MODEL), lambda i: (i, gb + 1)),
            pl.BlockSpec((tm, D_MODEL), lambda i: (i, gb + 2)),
            pl.BlockSpec((tm, D_MODEL), lambda i: (i, 0)),
            pl.BlockSpec((3, BRANCH_WIDTH, D_MODEL), lambda i: (0, 0, 0)),
            pl.BlockSpec((D_MODEL, D_MODEL), lambda i: (0, 0)),
            pl.BlockSpec((3, BRANCH_WIDTH), lambda i: (0, 0)),
        ],
        out_specs=pl.BlockSpec((tm, D_MODEL), lambda i: (i, 0)),
        out_shape=jax.ShapeDtypeStruct((TOKENS, D_MODEL), F32),
        compiler_params=_cparams(("parallel",)),
        name="merge",
    )(y_fox, proj, proj, proj, proj, proj, y_dn, proj, proj, proj, x, w_branch, w_o, sc_conv_w)


def _ffn_kernel(x_ref, xp_ref, gf_ref, wup_ref, cw_ref, wd_ref, gp_ref, wpg_ref, p_ref, wple_ref, o_ref,
                act_ref):
    i = pl.program_id(0)
    keep_prev = jnp.where(i % (SEQ // FFN_TM) != 0, 1.0, 0.0)
    xx = jnp.concatenate([xp_ref[...] * keep_prev, x_ref[...]], axis=0)
    hn = _rmsnorm(xx, gf_ref[...]).astype(BF16)

    def branch(cols):
        u = _dot(hn, wup_ref[:, cols])
        cw = cw_ref[:, cols]
        return cw[2:3] * u[HALO:] + cw[1:2] * _shifted(u, 1) + cw[0:1] * _shifted(u, 2)

    for j in range(FFN_NF):
        gate = branch(slice(j * FFN_TF, (j + 1) * FFN_TF))
        val = branch(slice(D_FF + j * FFN_TF, D_FF + (j + 1) * FFN_TF))
        act_ref[:, j * FFN_TF:(j + 1) * FFN_TF] = (gate * _sigmoid(gate) * val).astype(BF16)

    x1 = x_ref[...] + _dot(act_ref[...], wd_ref[...])
    h2 = _rmsnorm(x1, gp_ref[...]).astype(BF16)
    pgate = _sigmoid(_dot(h2, wpg_ref[...]))
    emb = _dot(p_ref[...].astype(BF16), wple_ref[...])
    o_ref[...] = x1 + pgate * emb


def _ffn(x, p, g_ffn, w_up, conv_w, w_down, g_ple, w_pg, w_ple):
    tm = FFN_TM
    hpb = tm // HALO

    def whole(shape):
        return pl.BlockSpec(shape, lambda i: (0,) * len(shape))

    return pl.pallas_call(
        _ffn_kernel,
        grid=(TOKENS // tm,),
        in_specs=[
            pl.BlockSpec((tm, D_MODEL), lambda i: (i, 0)),
            pl.BlockSpec((HALO, D_MODEL), lambda i: (jnp.maximum(i * hpb - 1, 0), 0)),
            whole((1, D_MODEL)),
            whole((D_MODEL, 2 * D_FF)),
            whole((3, 2 * D_FF)),
            whole((D_FF, D_MODEL)),
            whole((1, D_MODEL)),
            whole((D_MODEL, D_MODEL)),
            pl.BlockSpec((tm, PLE_DIM), lambda i: (i, 0)),
            whole((PLE_DIM, D_MODEL)),
        ],
        out_specs=pl.BlockSpec((tm, D_MODEL), lambda i: (i, 0)),
        out_shape=jax.ShapeDtypeStruct((TOKENS, D_MODEL), F32),
        scratch_shapes=[pltpu.VMEM((tm, D_FF), BF16)],
        compiler_params=_cparams(("parallel",)),
        name="ffn",
    )(x, x, g_ffn, w_up, conv_w, w_down, g_ple, w_pg, p, w_ple)


def _lane_row(values, offset):
    row = jnp.zeros((1, LANES), F32)
    return lax.dynamic_update_slice(row, values.astype(F32)[None, :], (0, offset))


def _layer(x, p_i, g_mix, w_in, b_fox_f, fox_q_gain, fox_k_gain, sc_conv_w, dn_conv_w, dn_a_log,
           dn_dt_bias, dn_norm_gain, w_branch, w_o, g_ffn, w_up, ffn_conv_w, w_down, g_ple,
           w_ple_gate, w_ple):
    o_f = 3 * BRANCH_WIDTH
    o_sc = o_f + FOX_HEADS
    o_dn = o_sc + 3 * BRANCH_WIDTH
    o_b = o_dn + 3 * BRANCH_WIDTH
    o_a = o_b + DN_HEADS
    o_z = o_a + DN_HEADS
    o_g = o_z + BRANCH_WIDTH
    w_big = jnp.concatenate([w_in[:, :o_f], w_in[:, o_sc:o_dn], w_in[:, o_dn:o_b], w_in[:, o_z:o_g],
                             w_in[:, o_g:]], axis=1).astype(BF16)
    w_small = jnp.concatenate([w_in[:, o_f:o_sc], w_in[:, o_b:o_a], w_in[:, o_a:o_z],
                               jnp.zeros((D_MODEL, LANES - FOX_HEADS - 2 * DN_HEADS), F32)],
                              axis=1).astype(BF16)

    proj, small = _proj(x, g_mix[None, :], w_big, w_small)
    cf = _cumf(small, _lane_row(b_fox_f, SM_FOXF))
    y_fox = _fox(proj, cf, jnp.tile(fox_q_gain, 2)[None, :], jnp.tile(fox_k_gain, 2)[None, :])
    y_dn = _deltanet(proj, small, dn_conv_w, _lane_row(dn_a_log, SM_A), _lane_row(dn_dt_bias, SM_A),
                     dn_norm_gain[None, :])
    x = _merge(x, proj, y_fox, y_dn, w_branch.astype(BF16), w_o.astype(BF16), sc_conv_w)
    x = _ffn(x, p_i, g_ffn[None, :], w_up.astype(BF16), ffn_conv_w, w_down.astype(BF16),
             g_ple[None, :], w_ple_gate.astype(BF16), w_ple.astype(BF16))
    return x


def kernel(x, p, g_mix, w_in, b_fox_f, fox_q_gain, fox_k_gain, sc_conv_w, dn_conv_w, dn_a_log,
           dn_dt_bias, dn_norm_gain, w_branch, w_o, g_ffn, w_up, ffn_conv_w, w_down, g_ple,
           w_ple_gate, w_ple):
    depth = p.shape[0]
    x = x.reshape(TOKENS, D_MODEL)
    p = p.reshape(depth, TOKENS, PLE_DIM)
    for i in range(depth):
        x = _layer(x, p[i], g_mix[i], w_in[i], b_fox_f[i], fox_q_gain[i], fox_k_gain[i],
                   sc_conv_w[i], dn_conv_w[i], dn_a_log[i], dn_dt_bias[i], dn_norm_gain[i],
                   w_branch[i], w_o[i], g_ffn[i], w_up[i], ffn_conv_w[i], w_down[i], g_ple[i],
                   w_ple_gate[i], w_ple[i])
    return x.reshape(BATCH, SEQ, D_MODEL)
```

```python
import jax
import jax.numpy as jnp
import numpy as np
from jax import lax
from jax.experimental import pallas as pl
from jax.experimental.pallas import tpu as pltpu

F32 = jnp.float32
BF16 = jnp.bfloat16
HI = lax.Precision.HIGHEST

D_MODEL = 1024
BATCH = 8
SEQ = 2048
TOKENS = BATCH * SEQ
PLE_DIM = 256
BRANCH_WIDTH = 512
FOX_HEADS = 8
FOX_HEAD_DIM = 64
DN_HEADS = 4
DN_HEAD_DIM = 128
DN_CHUNK = 64
N_CHUNKS = SEQ // DN_CHUNK
D_FF = 2816
EPS = 1e-6

LANES = 128
SUBLANES = 8
HALO = 16

PROJ_WIDTH = 8192
COL_FOX = 0
COL_SC = 1536
COL_DN = 3072
COL_DNZ = 4608
COL_GATE = 5120
SM_FOXF = 0
SM_BETA = 8
SM_A = 12

PROJ_TM = 1024
PROJ_TN = 2048
FOX_TQ = 512
FOX_TK = 512
MERGE_TM = 512
FFN_TM = 512
FFN_TF = 256
FFN_NF = D_FF // FFN_TF

VMEM_LIMIT = 48 * 1024 * 1024


def _cparams(sem):
    return pltpu.CompilerParams(dimension_semantics=sem, vmem_limit_bytes=VMEM_LIMIT)


def _sigmoid(x):
    return 1.0 / (1.0 + jnp.exp(-x))


def _softplus(x):
    return jnp.maximum(x, 0.0) + jnp.log1p(jnp.exp(-jnp.abs(x)))


def _dot(a, b, precision=None):
    return jnp.dot(a, b, preferred_element_type=F32, precision=precision)


def _dot_nt(a, b):
    return lax.dot_general(a, b, (((1,), (1,)), ((), ())), preferred_element_type=F32)


def _dot_tn(a, b):
    return lax.dot_general(a, b, (((0,), (0,)), ((), ())), preferred_element_type=F32)


def _rmsnorm(x, gain):
    ms = jnp.mean(x * x, axis=-1, keepdims=True)
    return x * lax.rsqrt(ms + EPS) * gain


def _shifted(cat, shift):
    if shift == 0:
        return cat[HALO:]
    return pltpu.roll(cat, shift, 0)[HALO:]


def _bf16_split3(x):
    hi = x.astype(BF16).astype(F32)
    r = x - hi
    mid = r.astype(BF16).astype(F32)
    lo = (r - mid).astype(BF16).astype(F32)
    return hi, mid, lo


def _proj_kernel(x_ref, g_ref, w_ref, ws_ref, o_ref, os_ref, hn_ref):
    @pl.when(pl.program_id(1) == 0)
    def _():
        hn = _rmsnorm(x_ref[...], g_ref[...]).astype(BF16)
        hn_ref[...] = hn
        os_ref[...] = _dot(hn, ws_ref[...])

    o_ref[...] = _dot(hn_ref[...], w_ref[...]).astype(BF16)


def _of_layer(layer, block, index_map=None, resident=False):
    if index_map is None:
        index_map = lambda *grid: (0,) * len(block)
    mode = dict(pipeline_mode=pl.Buffered(1)) if resident else {}
    return pl.BlockSpec((None,) + tuple(block), lambda *grid: (layer,) + tuple(index_map(*grid)), **mode)


def _proj(x, gain, w_big, w_small, layer):
    grid = (TOKENS // PROJ_TM, PROJ_WIDTH // PROJ_TN)
    return pl.pallas_call(
        _proj_kernel,
        grid=grid,
        in_specs=[
            pl.BlockSpec((PROJ_TM, D_MODEL), lambda i, j: (i, 0)),
            _of_layer(layer, (1, D_MODEL)),
            _of_layer(layer, (D_MODEL, PROJ_TN), lambda i, j: (0, j)),
            _of_layer(layer, (D_MODEL, LANES)),
        ],
        out_specs=[
            pl.BlockSpec((PROJ_TM, PROJ_TN), lambda i, j: (i, j)),
            pl.BlockSpec((PROJ_TM, LANES), lambda i, j: (i, 0)),
        ],
        out_shape=[
            jax.ShapeDtypeStruct((TOKENS, PROJ_WIDTH), BF16),
            jax.ShapeDtypeStruct((TOKENS, LANES), F32),
        ],
        scratch_shapes=[pltpu.VMEM((PROJ_TM, D_MODEL), BF16)],
        compiler_params=_cparams(("parallel", "arbitrary")),
        name="proj",
    )(x, gain, w_big, w_small)


CUMF_BLK = 128


def _cumf_kernel(sm_ref, bf_ref, cf_ref):
    r = lax.broadcasted_iota(jnp.int32, (CUMF_BLK, CUMF_BLK), 0)
    c = lax.broadcasted_iota(jnp.int32, (CUMF_BLK, CUMF_BLK), 1)
    tri = (r >= c).astype(F32)
    carry = jnp.zeros((1, LANES), F32)
    for blk in range(SEQ // CUMF_BLK):
        rows = slice(blk * CUMF_BLK, (blk + 1) * CUMF_BLK)
        log_f = -_softplus(-(sm_ref[rows, :] + bf_ref[...]))
        cum = _dot(tri, log_f, HI) + carry
        carry = cum[CUMF_BLK - 1:CUMF_BLK, :]
        cf_ref[0, rows, :] = cum


def _cumf(small, bias_row, layer):
    return pl.pallas_call(
        _cumf_kernel,
        grid=(BATCH,),
        in_specs=[
            pl.BlockSpec((SEQ, LANES), lambda b: (b, 0)),
            _of_layer(layer, (1, LANES)),
        ],
        out_specs=pl.BlockSpec((1, SEQ, LANES), lambda b: (b, 0, 0)),
        out_shape=jax.ShapeDtypeStruct((BATCH, SEQ, LANES), F32),
        compiler_params=_cparams(("parallel",)),
        name="cumf",
    )(small, bias_row)


AUG = FOX_HEAD_DIM


def _fox_selectors():
    sel = np.zeros((2, FOX_HEADS // 2, 4 * LANES, 2 * LANES), np.float32)
    ones = np.zeros((2, 1, 2 * LANES), np.float32)
    for hp in range(FOX_HEADS // 2):
        for e in range(2):
            base = e * LANES
            for d in range(FOX_HEAD_DIM):
                sel[:, hp, e * FOX_HEAD_DIM + d, base + d] = 1.0
            for part in range(3):
                src = (1 + part) * LANES + SM_FOXF + 2 * hp + e
                sel[0, hp, src, base + AUG + part] = 1.0
                sel[1, hp, src, base + AUG + 3 + part] = -1.0
    for e in range(2):
        ones[0, 0, e * LANES + AUG + 3:e * LANES + AUG + 6] = 1.0
        ones[1, 0, e * LANES + AUG:e * LANES + AUG + 3] = 1.0
    return jnp.asarray(sel, BF16), jnp.asarray(ones, F32)


def _fox_kernel(q_ref, k_ref, v_ref, cfq_ref, cfk_ref, qg_ref, kg_ref, sel_ref, ones_ref, o_ref, ka_ref,
                vt_ref, qa_ref, acc_ref):
    qi = pl.program_id(1)
    lane = lax.broadcasted_iota(jnp.int32, (1, LANES), 1)
    first_head = lane < FOX_HEAD_DIM

    def headnorm(x, gain):
        x2 = x * x
        s0 = jnp.sum(jnp.where(first_head, x2, 0.0), axis=-1, keepdims=True)
        s1 = jnp.sum(jnp.where(first_head, 0.0, x2), axis=-1, keepdims=True)
        inv = jnp.where(first_head,
                        lax.rsqrt(s0 * (1.0 / FOX_HEAD_DIM) + EPS),
                        lax.rsqrt(s1 * (1.0 / FOX_HEAD_DIM) + EPS))
        return x * inv * gain

    def cum_parts(cf):
        return [part.astype(BF16) for part in _bf16_split3(cf)]

    def augmented(xn, parts, hp, side):
        packed = jnp.concatenate([xn.astype(BF16)] + parts, axis=1)
        return (_dot(packed, sel_ref[side, hp]) + ones_ref[side]).astype(BF16)

    @pl.when(qi == 0)
    def _():
        def body(c, carry):
            off = pl.multiple_of(c * FOX_TK, FOX_TK)
            rows = pl.ds(off, FOX_TK)
            parts = cum_parts(cfk_ref[0, rows, :])
            for hp in range(FOX_HEADS // 2):
                pair = slice(hp * LANES, (hp + 1) * LANES)
                kn = headnorm(k_ref[rows, pair].astype(F32), kg_ref[...])
                ka = augmented(kn, parts, hp, 1)
                ka_ref[2 * hp, rows, :] = ka[:, :LANES]
                ka_ref[2 * hp + 1, rows, :] = ka[:, LANES:]
                vt_ref[pair, rows] = v_ref[rows, pair].astype(F32).T.astype(BF16)
            return carry
        lax.fori_loop(0, SEQ // FOX_TK, body, 0)

    parts = cum_parts(cfq_ref[0])
    for hp in range(FOX_HEADS // 2):
        pair = slice(hp * LANES, (hp + 1) * LANES)
        qn = headnorm(q_ref[:, pair].astype(F32), qg_ref[...]) * (FOX_HEAD_DIM ** -0.5)
        qa = augmented(qn, parts, hp, 0)
        qa_ref[2 * hp] = qa[:, :LANES]
        qa_ref[2 * hp + 1] = qa[:, LANES:]
    acc_ref[...] = jnp.zeros_like(acc_ref)

    hs = range(FOX_HEADS)

    def hrows(h):
        return slice(h * FOX_HEAD_DIM, (h + 1) * FOX_HEAD_DIM)

    def step(j, carry, masked):
        off = pl.multiple_of(j * FOX_TK, FOX_TK)
        rows = pl.ds(off, FOX_TK)
        acc = [acc_ref[hrows(h), :] for h in hs]
        s = [_dot_nt(ka_ref[h, rows, :], qa_ref[h]) for h in hs]
        if masked:
            krow = lax.broadcasted_iota(jnp.int32, (FOX_TK, FOX_TQ), 0)
            qcol = lax.broadcasted_iota(jnp.int32, (FOX_TK, FOX_TQ), 1)
            s = [jnp.where(qcol >= krow, s[h], -jnp.inf) for h in hs]
        m_new = [jnp.maximum(carry[h][0], jnp.max(s[h], axis=0, keepdims=True)) for h in hs]
        alpha = [jnp.exp(carry[h][0] - m_new[h]) for h in hs]
        p = [jnp.exp(s[h] - m_new[h]) for h in hs]
        l_new = [alpha[h] * carry[h][1] + jnp.sum(p[h], axis=0, keepdims=True) for h in hs]
        pv = [_dot(vt_ref[hrows(h), rows], p[h].astype(BF16)) for h in hs]
        for h in hs:
            acc_ref[hrows(h), :] = alpha[h] * acc[h] + pv[h]
        return tuple((m_new[h], l_new[h]) for h in hs)

    init = tuple((jnp.full((1, FOX_TQ), -jnp.inf, F32), jnp.zeros((1, FOX_TQ), F32)) for _ in hs)
    carry = lax.fori_loop(0, qi, lambda j, c: step(j, c, False), init)
    final = step(qi, carry, True)
    sub = lax.broadcasted_iota(jnp.int32, (LANES, 1), 0)
    for hp in range(FOX_HEADS // 2):
        pair = slice(hp * LANES, (hp + 1) * LANES)
        denom = jnp.where(sub < FOX_HEAD_DIM, final[2 * hp][1], final[2 * hp + 1][1])
        o_ref[:, pair] = (acc_ref[pair, :] / denom).T.astype(BF16)


def _fox(proj, cf, q_gain2, k_gain2, layer):
    nq = SEQ // FOX_TQ
    fb = COL_FOX // BRANCH_WIDTH
    sel, ones = _fox_selectors()
    return pl.pallas_call(
        _fox_kernel,
        grid=(BATCH, nq),
        in_specs=[
            pl.BlockSpec((FOX_TQ, BRANCH_WIDTH), lambda b, qi: (b * nq + qi, fb)),
            pl.BlockSpec((SEQ, BRANCH_WIDTH), lambda b, qi: (b, fb + 1)),
            pl.BlockSpec((SEQ, BRANCH_WIDTH), lambda b, qi: (b, fb + 2)),
            pl.BlockSpec((1, FOX_TQ, LANES), lambda b, qi: (b, qi, 0)),
            pl.BlockSpec((1, SEQ, LANES), lambda b, qi: (b, 0, 0)),
            _of_layer(layer, (1, LANES)),
            _of_layer(layer, (1, LANES)),
            pl.BlockSpec(sel.shape, lambda b, qi: (0, 0, 0, 0)),
            pl.BlockSpec(ones.shape, lambda b, qi: (0, 0, 0)),
        ],
        out_specs=pl.BlockSpec((FOX_TQ, BRANCH_WIDTH), lambda b, qi: (b * nq + qi, 0)),
        out_shape=jax.ShapeDtypeStruct((TOKENS, BRANCH_WIDTH), BF16),
        scratch_shapes=[
            pltpu.VMEM((FOX_HEADS, SEQ, LANES), BF16),
            pltpu.VMEM((BRANCH_WIDTH, SEQ), BF16),
            pltpu.VMEM((FOX_HEADS, FOX_TQ, LANES), BF16),
            pltpu.VMEM((BRANCH_WIDTH, FOX_TQ), F32),
        ],
        compiler_params=_cparams(("parallel", "arbitrary")),
        name="fox",
    )(proj, proj, proj, cf, cf, q_gain2, k_gain2, sel, ones)


DN_STREAMS = 4

def _dn_kernel(q_ref, k_ref, v_ref, z_ref, sm_ref, wq_ref, wk_ref, wv_ref, alog_ref, dtb_ref,
               ng_ref, o_ref, u_ref, kcum_ref, qdec_ref, kdec_ref, qk_ref, gt_ref, st_ref):
    c = DN_CHUNK
    ri = lax.broadcasted_iota(jnp.int32, (c, c), 0)
    ci = lax.broadcasted_iota(jnp.int32, (c, c), 1)
    incl = ri >= ci
    strict = ri > ci
    eye = ri == ci
    eye_f = eye.astype(F32)
    tri = incl.astype(F32)
    lane = lax.broadcasted_iota(jnp.int32, (1, LANES), 1)

    def conv_silu(cur, prev, w):
        cat = jnp.concatenate([prev, cur], axis=0)
        y = (w[3:4] * cur + w[2:3] * _shifted(cat, 1) + w[1:2] * _shifted(cat, 2)
             + w[0:1] * _shifted(cat, 3))
        return y * _sigmoid(y)

    def l2norm(x):
        return x * lax.rsqrt(jnp.sum(x * x, axis=-1, keepdims=True) + EPS)

    def heads():
        return [slice(h * DN_HEAD_DIM, (h + 1) * DN_HEAD_DIM) for h in range(DN_HEADS)]

    def rows_of(chunk):
        if isinstance(chunk, int):
            return chunk * c, max(chunk * c - HALO, 0)
        r0 = pl.multiple_of(chunk * c, c)
        return r0, pl.multiple_of(jnp.maximum(r0 - HALO, 0), HALO)

    def prepare(chunks):
        units = []
        raw, gcum_all, beta_all = [], [], []
        for chunk in chunks:
            r0, p0 = rows_of(chunk)
            rows = pl.ds(r0, c)
            prow = pl.ds(p0, HALO)
            keep_prev = jnp.where(chunk > 0, 1.0, 0.0)
            sm = sm_ref[rows, :]
            bsig = _sigmoid(sm)
            gsum = _dot(tri, -jnp.exp(alog_ref[...]) * _softplus(sm + dtb_ref[...]), HI)
            for h, cols in enumerate(heads()):
                units.append((chunk, rows, h, cols))
                raw.append([(ref[rows, cols].astype(F32), ref[prow, cols].astype(F32) * keep_prev,
                             w_ref[:, cols])
                            for ref, w_ref in ((q_ref, wq_ref), (k_ref, wk_ref), (v_ref, wv_ref))])
                gcum_all.append(gsum)
                beta_all.append(bsig)
        hs = range(len(units))
        gc, beta, q, k, v, decay = [], [], [], [], [], []
        for u in hs:
            head = units[u][2]
            g1 = jnp.sum(jnp.where(lane == SM_A + head, gcum_all[u], 0.0), axis=-1, keepdims=True)
            b1 = jnp.sum(jnp.where(lane == SM_BETA + head, beta_all[u], 0.0), axis=-1, keepdims=True)
            gc.append(jnp.broadcast_to(g1, (c, DN_HEAD_DIM)))
            beta.append(jnp.broadcast_to(b1, (c, DN_HEAD_DIM)))
            q.append(l2norm(conv_silu(*raw[u][0])) * (DN_HEAD_DIM ** -0.5))
            k.append(l2norm(conv_silu(*raw[u][1])))
            v.append(conv_silu(*raw[u][2]))
            gc_sq = gc[u][:, :c]
            g_row = jnp.sum(jnp.where(eye, gc_sq, 0.0), axis=0, keepdims=True)
            decay.append(jnp.exp(jnp.where(incl, gc_sq - g_row, -jnp.inf)))
        kb = [k[h] * beta[h] for h in hs]
        k16 = [k[h].astype(BF16) for h in hs]
        kk = [_dot_nt(kb[h].astype(BF16), k16[h]) for h in hs]
        qk = [_dot_nt(q[h].astype(BF16), k16[h]) for h in hs]
        yield
        neg = [jnp.where(strict, -(kk[h] * decay[h]), 0.0) for h in hs]
        qsum = [eye_f + neg[h] for h in hs]
        neg16 = [neg[h].astype(BF16) for h in hs]
        power = [_dot(neg16[h], neg16[h]) for h in hs]
        yield
        for _ in range(4):
            both = [_dot(power[h].astype(BF16),
                         jnp.concatenate([qsum[h], power[h]], axis=1).astype(BF16)) for h in hs]
            yield
            qsum = [qsum[h] + both[h][:, :c] for h in hs]
            power = [both[h][:, c:] for h in hs]
        last = [_dot(power[h].astype(BF16), qsum[h].astype(BF16)) for h in hs]
        yield
        eg = [jnp.exp(gc[h]) for h in hs]
        rhs = [jnp.concatenate([v[h] * beta[h], kb[h] * eg[h]], axis=1) for h in hs]
        sol = [rhs[h] + _dot((qsum[h] + last[h] - eye_f).astype(BF16), rhs[h].astype(BF16)) for h in hs]
        yield
        for u, (chunk, rows, head, cols) in enumerate(units):
            g_last = gc[u][c - 1:c, :]
            u_ref[rows, cols] = sol[u][:, :DN_HEAD_DIM]
            kcum_ref[rows, cols] = sol[u][:, DN_HEAD_DIM:].astype(BF16)
            qdec_ref[rows, cols] = (q[u] * eg[u]).astype(BF16)
            kdec_ref[rows, cols] = (k[u] * jnp.exp(g_last - gc[u])).astype(BF16)
            qk_ref[rows, head * c:(head + 1) * c] = jnp.where(incl, qk[u] * decay[u], 0.0).astype(BF16)
            gt_ref[chunk, :, cols] = jnp.broadcast_to(jnp.exp(g_last), (SUBLANES, DN_HEAD_DIM))

    def scan(chunks):
        hs = range(DN_HEADS)
        loaded = []
        for chunk in chunks:
            rows = pl.ds(rows_of(chunk)[0], c)
            loaded.append((rows, [(u_ref[rows, cols], kcum_ref[rows, cols], qdec_ref[rows, cols],
                                   qk_ref[rows, h * c:(h + 1) * c], kdec_ref[rows, cols],
                                   gt_ref[chunk, 0:1, cols], z_ref[rows, cols].astype(F32))
                                  for h, cols in enumerate(heads())]))
        state = [st_ref[h] for h in hs]
        outs = []
        for rows, per_head in loaded:
            state16 = [state[h].astype(BF16) for h in hs]
            v16 = [(per_head[h][0] - _dot(per_head[h][1], state16[h])).astype(BF16) for h in hs]
            yield
            upd = [_dot_tn(per_head[h][4], v16[h]) for h in hs]
            out = [_dot(per_head[h][2], state16[h]) + _dot(per_head[h][3], v16[h]) for h in hs]
            yield
            state = [state[h] * per_head[h][5] + upd[h] for h in hs]
            outs.append((rows, [(_rmsnorm(out[h], ng_ref[...])
                                 * (per_head[h][6] * _sigmoid(per_head[h][6]))).astype(BF16) for h in hs]))
        for h in hs:
            st_ref[h] = state[h]
        for rows, out in outs:
            for h, cols in enumerate(heads()):
                o_ref[rows, cols] = out[h]

    def interleave(*stages):
        active = list(stages)
        while active:
            for gen in list(active):
                if next(gen, StopIteration) is StopIteration:
                    active.remove(gen)

    def pair(n):
        return [DN_STREAMS * n + t for t in range(DN_STREAMS)]

    n_iter = N_CHUNKS // DN_STREAMS
    st_ref[...] = jnp.zeros_like(st_ref)
    interleave(prepare(pair(0)))

    def body(n, carry):
        interleave(scan(pair(n - 1)), prepare(pair(n)))
        return carry

    lax.fori_loop(1, n_iter, body, 0)
    interleave(scan(pair(n_iter - 1)))


def _deltanet(proj, small, conv_w, alog_row, dtb_row, norm_gain, layer):
    qb = COL_DN // BRANCH_WIDTH
    zb = COL_DNZ // BRANCH_WIDTH

    def colspec(blk):
        return pl.BlockSpec((SEQ, BRANCH_WIDTH), lambda b: (b, blk))

    def wspec(blk):
        return _of_layer(layer, (4, BRANCH_WIDTH), lambda b: (0, blk))

    row = _of_layer(layer, (1, LANES))
    return pl.pallas_call(
        _dn_kernel,
        grid=(BATCH,),
        in_specs=[
            colspec(qb), colspec(qb + 1), colspec(qb + 2), colspec(zb),
            pl.BlockSpec((SEQ, LANES), lambda b: (b, 0)),
            wspec(0), wspec(1), wspec(2),
            row, row, row,
        ],
        out_specs=pl.BlockSpec((SEQ, BRANCH_WIDTH), lambda b: (b, 0)),
        out_shape=jax.ShapeDtypeStruct((TOKENS, BRANCH_WIDTH), BF16),
        scratch_shapes=[
            pltpu.VMEM((SEQ, BRANCH_WIDTH), F32),
            pltpu.VMEM((SEQ, BRANCH_WIDTH), BF16),
            pltpu.VMEM((SEQ, BRANCH_WIDTH), BF16),
            pltpu.VMEM((SEQ, BRANCH_WIDTH), BF16),
            pltpu.VMEM((SEQ, DN_HEADS * DN_CHUNK), BF16),
            pltpu.VMEM((N_CHUNKS, SUBLANES, BRANCH_WIDTH), F32),
            pltpu.VMEM((DN_HEADS, DN_HEAD_DIM, DN_HEAD_DIM), F32),
        ],
        compiler_params=_cparams(("parallel",)),
        name="deltanet",
    )(proj, proj, proj, proj, small, conv_w, conv_w, conv_w, alog_row, dtb_row, norm_gain)


def _merge_kernel(yf_ref, sb_ref, sc_ref, sv_ref, scp_ref, svp_ref, yd_ref, g0_ref, g1_ref, g2_ref,
                  x_ref, wb_ref, wo_ref, cw_ref, o_ref):
    i = pl.program_id(0)
    keep_prev = jnp.where(i % (SEQ // MERGE_TM) != 0, 1.0, 0.0)
    cur = sc_ref[...].astype(F32) * sv_ref[...].astype(F32)
    prev = scp_ref[...].astype(F32) * svp_ref[...].astype(F32) * keep_prev
    cat = jnp.concatenate([prev, cur], axis=0)
    cw = cw_ref[...]
    conv = cw[2:3] * cur + cw[1:2] * _shifted(cat, 1) + cw[0:1] * _shifted(cat, 2)
    y_sc = sb_ref[...].astype(F32) * conv
    merged = (_sigmoid(g0_ref[...].astype(F32)) * _dot(yf_ref[...], wb_ref[0])
              + _sigmoid(g1_ref[...].astype(F32)) * _dot(y_sc.astype(BF16), wb_ref[1])
              + _sigmoid(g2_ref[...].astype(F32)) * _dot(yd_ref[...], wb_ref[2]))
    o_ref[...] = x_ref[...] + _dot(merged.astype(BF16), wo_ref[...])


def _merge(x, proj, y_fox, y_dn, w_branch, w_o, sc_conv_w, layer):
    tm = MERGE_TM
    scb = COL_SC // BRANCH_WIDTH
    gb = COL_GATE // D_MODEL
    hpb = tm // HALO

    def prev_rows(i):
        return jnp.maximum(i * hpb - 1, 0)

    return pl.pallas_call(
        _merge_kernel,
        grid=(TOKENS // tm,),
        in_specs=[
            pl.BlockSpec((tm, BRANCH_WIDTH), lambda i: (i, 0)),
            pl.BlockSpec((tm, BRANCH_WIDTH), lambda i: (i, scb)),
            pl.BlockSpec((tm, BRANCH_WIDTH), lambda i: (i, scb + 1)),
            pl.BlockSpec((tm, BRANCH_WIDTH), lambda i: (i, scb + 2)),
            pl.BlockSpec((HALO, BRANCH_WIDTH), lambda i: (prev_rows(i), scb + 1)),
            pl.BlockSpec((HALO, BRANCH_WIDTH), lambda i: (prev_rows(i), scb + 2)),
            pl.BlockSpec((tm, BRANCH_WIDTH), lambda i: (i, 0)),
            pl.BlockSpec((tm, D_MODEL), lambda i: (i, gb)),
            pl.BlockSpec((tm, D_MODEL), lambda i: (i, gb + 1)),
            pl.BlockSpec((tm, D_MODEL), lambda i: (i, gb + 2)),
            pl.BlockSpec((tm, D_MODEL), lambda i: (i, 0)),
            _of_layer(layer, (3, BRANCH_WIDTH, D_MODEL), resident=True),
            _of_layer(layer, (D_MODEL, D_MODEL), resident=True),
            _of_layer(layer, (3, BRANCH_WIDTH)),
        ],
        out_specs=pl.BlockSpec((tm, D_MODEL), lambda i: (i, 0)),
        out_shape=jax.ShapeDtypeStruct((TOKENS, D_MODEL), F32),
        compiler_params=_cparams(("parallel",)),
        name="merge",
    )(y_fox, proj, proj, proj, proj, proj, y_dn, proj, proj, proj, x, w_branch, w_o, sc_conv_w)


def _ffn_kernel(x_ref, xp_ref, gf_ref, wup_ref, cw_ref, wd_ref, gp_ref, wpg_ref, p_ref, wple_ref, o_ref,
                act_ref):
    i = pl.program_id(0)
    keep_prev = jnp.where(i % (SEQ // FFN_TM) != 0, 1.0, 0.0)
    xx = jnp.concatenate([xp_ref[...] * keep_prev, x_ref[...]], axis=0)
    hn = _rmsnorm(xx, gf_ref[...]).astype(BF16)

    def branch(cols):
        u = _dot(hn, wup_ref[:, cols])
        cw = cw_ref[:, cols]
        return cw[2:3] * u[HALO:] + cw[1:2] * _shifted(u, 1) + cw[0:1] * _shifted(u, 2)

    for j in range(FFN_NF):
        gate = branch(slice(j * FFN_TF, (j + 1) * FFN_TF))
        val = branch(slice(D_FF + j * FFN_TF, D_FF + (j + 1) * FFN_TF))
        act_ref[:, j * FFN_TF:(j + 1) * FFN_TF] = (gate * _sigmoid(gate) * val).astype(BF16)

    x1 = x_ref[...] + _dot(act_ref[...], wd_ref[...])
    h2 = _rmsnorm(x1, gp_ref[...]).astype(BF16)
    pgate = _sigmoid(_dot(h2, wpg_ref[...]))
    emb = _dot(p_ref[...].astype(BF16), wple_ref[...])
    o_ref[...] = x1 + pgate * emb


def _ffn(x, p, g_ffn, w_up, conv_w, w_down, g_ple, w_pg, w_ple, layer):
    tm = FFN_TM
    hpb = tm // HALO

    def whole(shape):
        return _of_layer(layer, shape, resident=True)

    return pl.pallas_call(
        _ffn_kernel,
        grid=(TOKENS // tm,),
        in_specs=[
            pl.BlockSpec((tm, D_MODEL), lambda i: (i, 0)),
            pl.BlockSpec((HALO, D_MODEL), lambda i: (jnp.maximum(i * hpb - 1, 0), 0)),
            whole((1, D_MODEL)),
            whole((D_MODEL, 2 * D_FF)),
            whole((3, 2 * D_FF)),
            whole((D_FF, D_MODEL)),
            whole((1, D_MODEL)),
            whole((D_MODEL, D_MODEL)),
            _of_layer(layer, (tm, PLE_DIM), lambda i: (i, 0)),
            whole((PLE_DIM, D_MODEL)),
        ],
        out_specs=pl.BlockSpec((tm, D_MODEL), lambda i: (i, 0)),
        out_shape=jax.ShapeDtypeStruct((TOKENS, D_MODEL), F32),
        scratch_shapes=[pltpu.VMEM((tm, D_FF), BF16)],
        compiler_params=_cparams(("parallel",)),
        name="ffn",
    )(x, x, g_ffn, w_up, conv_w, w_down, g_ple, w_pg, p, w_ple)


def _lane_rows(values, offset):
    depth, n = values.shape
    return jnp.pad(values.astype(F32), ((0, 0), (offset, LANES - offset - n)))[:, None, :]


def kernel(x, p, g_mix, w_in, b_fox_f, fox_q_gain, fox_k_gain, sc_conv_w, dn_conv_w, dn_a_log,
           dn_dt_bias, dn_norm_gain, w_branch, w_o, g_ffn, w_up, ffn_conv_w, w_down, g_ple,
           w_ple_gate, w_ple):
    depth = p.shape[0]
    x = x.reshape(TOKENS, D_MODEL)
    p = p.reshape(depth, TOKENS, PLE_DIM)

    o_f = 3 * BRANCH_WIDTH
    o_sc = o_f + FOX_HEADS
    o_dn = o_sc + 3 * BRANCH_WIDTH
    o_b = o_dn + 3 * BRANCH_WIDTH
    o_a = o_b + DN_HEADS
    o_z = o_a + DN_HEADS
    o_g = o_z + BRANCH_WIDTH
    w_big = jnp.concatenate([w_in[..., :o_f], w_in[..., o_sc:o_dn], w_in[..., o_dn:o_b],
                             w_in[..., o_z:o_g], w_in[..., o_g:]], axis=-1).astype(BF16)
    w_small = jnp.concatenate([w_in[..., o_f:o_sc], w_in[..., o_b:o_a], w_in[..., o_a:o_z],
                               jnp.zeros((depth, D_MODEL, LANES - FOX_HEADS - 2 * DN_HEADS), F32)],
                              axis=-1).astype(BF16)
    w_branch16, w_o16, w_up16, w_down16 = (w.astype(BF16) for w in (w_branch, w_o, w_up, w_down))
    w_pg16, w_ple16 = w_ple_gate.astype(BF16), w_ple.astype(BF16)
    g_mix3, g_ffn3, g_ple3, dn_gain3 = (g[:, None, :] for g in (g_mix, g_ffn, g_ple, dn_norm_gain))
    q_gain3 = jnp.tile(fox_q_gain, (1, 2))[:, None, :]
    k_gain3 = jnp.tile(fox_k_gain, (1, 2))[:, None, :]
    fox_bias = _lane_rows(b_fox_f, SM_FOXF)
    a_log = _lane_rows(dn_a_log, SM_A)
    dt_bias = _lane_rows(dn_dt_bias, SM_A)

    for layer in range(depth):
        proj, small = _proj(x, g_mix3, w_big, w_small, layer)
        cf = _cumf(small, fox_bias, layer)
        y_fox = _fox(proj, cf, q_gain3, k_gain3, layer)
        y_dn = _deltanet(proj, small, dn_conv_w, a_log, dt_bias, dn_gain3, layer)
        x = _merge(x, proj, y_fox, y_dn, w_branch16, w_o16, sc_conv_w, layer)
        x = _ffn(x, p, g_ffn3, w_up16, ffn_conv_w, w_down16, g_ple3, w_pg16, w_ple16, layer)
    return x.reshape(BATCH, SEQ, D_MODEL)
```

```python
import jax
import jax.numpy as jnp
import numpy as np
from jax import lax
from jax.experimental import pallas as pl
from jax.experimental.pallas import tpu as pltpu

F32 = jnp.float32
BF16 = jnp.bfloat16
HI = lax.Precision.HIGHEST

D_MODEL = 1024
BATCH = 8
SEQ = 2048
TOKENS = BATCH * SEQ
PLE_DIM = 256
BRANCH_WIDTH = 512
FOX_HEADS = 8
FOX_HEAD_DIM = 64
DN_HEADS = 4
DN_HEAD_DIM = 128
DN_CHUNK = 64
N_CHUNKS = SEQ // DN_CHUNK
D_FF = 2816
EPS = 1e-6

LANES = 128
SUBLANES = 8
HALO = 16

PROJ_WIDTH = 8192
COL_FOX = 0
COL_SC = 1536
COL_DN = 3072
COL_DNZ = 4608
COL_GATE = 5120
SM_FOXF = 0
SM_BETA = 8
SM_A = 12

PROJ_TM = 1024
PROJ_TN = 2048
FOX_TQ = 512
FOX_TK = 512
MERGE_TM = 512
FFN_TM = 512
FFN_TF = 256
FFN_NF = D_FF // FFN_TF

VMEM_LIMIT = 48 * 1024 * 1024


def _cparams(sem):
    return pltpu.CompilerParams(dimension_semantics=sem, vmem_limit_bytes=VMEM_LIMIT)


def _sigmoid(x):
    return 1.0 / (1.0 + jnp.exp(-x))


def _softplus(x):
    return jnp.maximum(x, 0.0) + jnp.log1p(jnp.exp(-jnp.abs(x)))


def _dot(a, b, precision=None):
    return jnp.dot(a, b, preferred_element_type=F32, precision=precision)


def _dot_nt(a, b):
    return lax.dot_general(a, b, (((1,), (1,)), ((), ())), preferred_element_type=F32)


def _dot_tn(a, b):
    return lax.dot_general(a, b, (((0,), (0,)), ((), ())), preferred_element_type=F32)


def _rmsnorm(x, gain):
    ms = jnp.mean(x * x, axis=-1, keepdims=True)
    return x * lax.rsqrt(ms + EPS) * gain


def _shifted(cat, shift):
    if shift == 0:
        return cat[HALO:]
    return pltpu.roll(cat, shift, 0)[HALO:]


def _bf16_split3(x):
    hi = x.astype(BF16).astype(F32)
    r = x - hi
    mid = r.astype(BF16).astype(F32)
    lo = (r - mid).astype(BF16).astype(F32)
    return hi, mid, lo


IN_F = 3 * BRANCH_WIDTH
IN_SC = IN_F + FOX_HEADS
IN_DN = IN_SC + 3 * BRANCH_WIDTH
IN_B = IN_DN + 3 * BRANCH_WIDTH
IN_Z = IN_B + 2 * DN_HEADS
IN_G = IN_Z + BRANCH_WIDTH
IN_WIDTH = IN_G + 3 * D_MODEL
REGROUP_ROWS = 256
assert IN_F % LANES == SM_FOXF and SM_FOXF + FOX_HEADS == SM_BETA
assert IN_B % LANES == SM_BETA and SM_BETA + DN_HEADS == SM_A


def _regroup_kernel(w_ref, big_ref, small_ref):
    col = 0
    for a, b in ((0, IN_F), (IN_SC, IN_DN), (IN_DN, IN_B), (IN_Z, IN_G), (IN_G, IN_WIDTH)):
        big_ref[:, col:col + (b - a)] = w_ref[:, a:b].astype(BF16)
        col += b - a
    lane = lax.broadcasted_iota(jnp.int32, (1, LANES), 1)
    f_tile = w_ref[:, IN_F - SM_FOXF:IN_F - SM_FOXF + LANES]
    b_tile = w_ref[:, IN_B - SM_BETA:IN_B - SM_BETA + LANES]
    small_ref[...] = jnp.where(lane < SM_BETA, f_tile,
                               jnp.where(lane < SM_A + DN_HEADS, b_tile, 0.0)).astype(BF16)


def _regroup(w_in):
    depth = w_in.shape[0]
    return pl.pallas_call(
        _regroup_kernel,
        grid=(depth, D_MODEL // REGROUP_ROWS),
        in_specs=[pl.BlockSpec((None, REGROUP_ROWS, IN_WIDTH), lambda l, i: (l, i, 0))],
        out_specs=[
            pl.BlockSpec((None, REGROUP_ROWS, PROJ_WIDTH), lambda l, i: (l, i, 0)),
            pl.BlockSpec((None, REGROUP_ROWS, LANES), lambda l, i: (l, i, 0)),
        ],
        out_shape=[
            jax.ShapeDtypeStruct((depth, D_MODEL, PROJ_WIDTH), BF16),
            jax.ShapeDtypeStruct((depth, D_MODEL, LANES), BF16),
        ],
        compiler_params=_cparams(("parallel", "parallel")),
        name="regroup",
    )(w_in)


def _proj_kernel(x_ref, g_ref, w_ref, ws_ref, o_ref, os_ref, hn_ref):
    @pl.when(pl.program_id(1) == 0)
    def _():
        hn = _rmsnorm(x_ref[...], g_ref[...]).astype(BF16)
        hn_ref[...] = hn
        os_ref[...] = _dot(hn, ws_ref[...])

    o_ref[...] = _dot(hn_ref[...], w_ref[...]).astype(BF16)


def _of_layer(layer, block, index_map=None, resident=False):
    if index_map is None:
        index_map = lambda *grid: (0,) * len(block)
    mode = dict(pipeline_mode=pl.Buffered(1)) if resident else {}
    return pl.BlockSpec((None,) + tuple(block), lambda *grid: (layer,) + tuple(index_map(*grid)), **mode)


def _proj(x, gain, w_big, w_small, layer):
    grid = (TOKENS // PROJ_TM, PROJ_WIDTH // PROJ_TN)
    return pl.pallas_call(
        _proj_kernel,
        grid=grid,
        in_specs=[
            pl.BlockSpec((PROJ_TM, D_MODEL), lambda i, j: (i, 0)),
            _of_layer(layer, (1, D_MODEL)),
            _of_layer(layer, (D_MODEL, PROJ_TN), lambda i, j: (0, j)),
            _of_layer(layer, (D_MODEL, LANES)),
        ],
        out_specs=[
            pl.BlockSpec((PROJ_TM, PROJ_TN), lambda i, j: (i, j)),
            pl.BlockSpec((PROJ_TM, LANES), lambda i, j: (i, 0)),
        ],
        out_shape=[
            jax.ShapeDtypeStruct((TOKENS, PROJ_WIDTH), BF16),
            jax.ShapeDtypeStruct((TOKENS, LANES), F32),
        ],
        scratch_shapes=[pltpu.VMEM((PROJ_TM, D_MODEL), BF16)],
        compiler_params=_cparams(("parallel", "arbitrary")),
        name="proj",
    )(x, gain, w_big, w_small)


CUMF_BLK = 128


def _cumf_kernel(sm_ref, bf_ref, cf_ref):
    r = lax.broadcasted_iota(jnp.int32, (CUMF_BLK, CUMF_BLK), 0)
    c = lax.broadcasted_iota(jnp.int32, (CUMF_BLK, CUMF_BLK), 1)
    tri = (r >= c).astype(F32)
    carry = jnp.zeros((1, LANES), F32)
    for blk in range(SEQ // CUMF_BLK):
        rows = slice(blk * CUMF_BLK, (blk + 1) * CUMF_BLK)
        log_f = -_softplus(-(sm_ref[rows, :] + bf_ref[...]))
        cum = _dot(tri, log_f, HI) + carry
        carry = cum[CUMF_BLK - 1:CUMF_BLK, :]
        cf_ref[0, rows, :] = cum


def _cumf(small, bias_row, layer):
    return pl.pallas_call(
        _cumf_kernel,
        grid=(BATCH,),
        in_specs=[
            pl.BlockSpec((SEQ, LANES), lambda b: (b, 0)),
            _of_layer(layer, (1, LANES)),
        ],
        out_specs=pl.BlockSpec((1, SEQ, LANES), lambda b: (b, 0, 0)),
        out_shape=jax.ShapeDtypeStruct((BATCH, SEQ, LANES), F32),
        compiler_params=_cparams(("parallel",)),
        name="cumf",
    )(small, bias_row)


AUG = FOX_HEAD_DIM


def _fox_selectors():
    sel = np.zeros((2, FOX_HEADS // 2, 4 * LANES, 2 * LANES), np.float32)
    ones = np.zeros((2, 1, 2 * LANES), np.float32)
    for hp in range(FOX_HEADS // 2):
        for e in range(2):
            base = e * LANES
            for d in range(FOX_HEAD_DIM):
                sel[:, hp, e * FOX_HEAD_DIM + d, base + d] = 1.0
            for part in range(3):
                src = (1 + part) * LANES + SM_FOXF + 2 * hp + e
                sel[0, hp, src, base + AUG + part] = 1.0
                sel[1, hp, src, base + AUG + 3 + part] = -1.0
    for e in range(2):
        ones[0, 0, e * LANES + AUG + 3:e * LANES + AUG + 6] = 1.0
        ones[1, 0, e * LANES + AUG:e * LANES + AUG + 3] = 1.0
    return jnp.asarray(sel, BF16), jnp.asarray(ones, F32)


def _fox_kernel(q_ref, k_ref, v_ref, cfq_ref, cfk_ref, qg_ref, kg_ref, sel_ref, ones_ref, o_ref, ka_ref,
                vt_ref, qa_ref, acc_ref, s_ref):
    qi = pl.program_id(1)
    lane = lax.broadcasted_iota(jnp.int32, (1, LANES), 1)
    first_head = lane < FOX_HEAD_DIM

    def headnorm(x, gain):
        x2 = x * x
        s0 = jnp.sum(jnp.where(first_head, x2, 0.0), axis=-1, keepdims=True)
        s1 = jnp.sum(jnp.where(first_head, 0.0, x2), axis=-1, keepdims=True)
        inv = jnp.where(first_head,
                        lax.rsqrt(s0 * (1.0 / FOX_HEAD_DIM) + EPS),
                        lax.rsqrt(s1 * (1.0 / FOX_HEAD_DIM) + EPS))
        return x * inv * gain

    def cum_parts(cf):
        return [part.astype(BF16) for part in _bf16_split3(cf)]

    def augmented(xn, parts, hp, side):
        packed = jnp.concatenate([xn.astype(BF16)] + parts, axis=1)
        return (_dot(packed, sel_ref[side, hp]) + ones_ref[side]).astype(BF16)

    @pl.when(qi == 0)
    def _():
        def body(c, carry):
            off = pl.multiple_of(c * FOX_TK, FOX_TK)
            rows = pl.ds(off, FOX_TK)
            parts = cum_parts(cfk_ref[0, rows, :])
            for hp in range(FOX_HEADS // 2):
                pair = slice(hp * LANES, (hp + 1) * LANES)
                kn = headnorm(k_ref[rows, pair].astype(F32), kg_ref[...])
                ka = augmented(kn, parts, hp, 1)
                ka_ref[2 * hp, rows, :] = ka[:, :LANES]
                ka_ref[2 * hp + 1, rows, :] = ka[:, LANES:]
                vt_ref[pair, rows] = v_ref[rows, pair].astype(F32).T.astype(BF16)
            return carry
        lax.fori_loop(0, SEQ // FOX_TK, body, 0)

    hs = range(FOX_HEADS)

    def hrows(h):
        return slice(h * FOX_HEAD_DIM, (h + 1) * FOX_HEAD_DIM)

    def scores(h, j):
        rows = pl.ds(pl.multiple_of(j * FOX_TK, FOX_TK), FOX_TK)
        return _dot_nt(ka_ref[h, rows, :], qa_ref[h])

    parts = cum_parts(cfq_ref[0])
    for hp in range(FOX_HEADS // 2):
        pair = slice(hp * LANES, (hp + 1) * LANES)
        qn = headnorm(q_ref[:, pair].astype(F32), qg_ref[...]) * (FOX_HEAD_DIM ** -0.5)
        qa = augmented(qn, parts, hp, 0)
        qa_ref[2 * hp] = qa[:, :LANES]
        qa_ref[2 * hp + 1] = qa[:, LANES:]
    acc_ref[...] = jnp.zeros_like(acc_ref)

    def step(j, carry, last):
        rows = pl.ds(pl.multiple_of(j * FOX_TK, FOX_TK), FOX_TK)
        out = []
        for h in hs:
            s = s_ref[h]
            if last:
                krow = lax.broadcasted_iota(jnp.int32, (FOX_TK, FOX_TQ), 0)
                qcol = lax.broadcasted_iota(jnp.int32, (FOX_TK, FOX_TQ), 1)
                s = jnp.where(qcol >= krow, s, -jnp.inf)
            m_new = jnp.maximum(carry[h][0], jnp.max(s, axis=0, keepdims=True))
            alpha = jnp.exp(carry[h][0] - m_new)
            p = jnp.exp(s - m_new)
            l_new = alpha * carry[h][1] + jnp.sum(p, axis=0, keepdims=True)
            if not last:
                s_ref[h] = scores(h, j + 1)
            pv = _dot(vt_ref[hrows(h), rows], p.astype(BF16))
            acc_ref[hrows(h), :] = alpha * acc_ref[hrows(h), :] + pv
            out.append((m_new, l_new))
        return tuple(out)

    for h in hs:
        s_ref[h] = scores(h, 0)
    init = tuple((jnp.full((1, FOX_TQ), -jnp.inf, F32), jnp.zeros((1, FOX_TQ), F32)) for _ in hs)
    carry = lax.fori_loop(0, qi, lambda j, c: step(j, c, False), init)
    final = step(qi, carry, True)
    sub = lax.broadcasted_iota(jnp.int32, (LANES, 1), 0)
    for hp in range(FOX_HEADS // 2):
        pair = slice(hp * LANES, (hp + 1) * LANES)
        denom = jnp.where(sub < FOX_HEAD_DIM, final[2 * hp][1], final[2 * hp + 1][1])
        o_ref[:, pair] = (acc_ref[pair, :] / denom).T.astype(BF16)


def _fox(proj, cf, q_gain2, k_gain2, layer):
    nq = SEQ // FOX_TQ
    fb = COL_FOX // BRANCH_WIDTH
    sel, ones = _fox_selectors()
    return pl.pallas_call(
        _fox_kernel,
        grid=(BATCH, nq),
        in_specs=[
            pl.BlockSpec((FOX_TQ, BRANCH_WIDTH), lambda b, qi: (b * nq + qi, fb)),
            pl.BlockSpec((SEQ, BRANCH_WIDTH), lambda b, qi: (b, fb + 1)),
            pl.BlockSpec((SEQ, BRANCH_WIDTH), lambda b, qi: (b, fb + 2)),
            pl.BlockSpec((1, FOX_TQ, LANES), lambda b, qi: (b, qi, 0)),
            pl.BlockSpec((1, SEQ, LANES), lambda b, qi: (b, 0, 0)),
            _of_layer(layer, (1, LANES)),
            _of_layer(layer, (1, LANES)),
            pl.BlockSpec(sel.shape, lambda b, qi: (0, 0, 0, 0)),
            pl.BlockSpec(ones.shape, lambda b, qi: (0, 0, 0)),
        ],
        out_specs=pl.BlockSpec((FOX_TQ, BRANCH_WIDTH), lambda b, qi: (b * nq + qi, 0)),
        out_shape=jax.ShapeDtypeStruct((TOKENS, BRANCH_WIDTH), BF16),
        scratch_shapes=[
            pltpu.VMEM((FOX_HEADS, SEQ, LANES), BF16),
            pltpu.VMEM((BRANCH_WIDTH, SEQ), BF16),
            pltpu.VMEM((FOX_HEADS, FOX_TQ, LANES), BF16),
            pltpu.VMEM((BRANCH_WIDTH, FOX_TQ), F32),
            pltpu.VMEM((FOX_HEADS, FOX_TK, FOX_TQ), F32),
        ],
        compiler_params=_cparams(("parallel", "arbitrary")),
        name="fox",
    )(proj, proj, proj, cf, cf, q_gain2, k_gain2, sel, ones)


DN_STREAMS = 4

def _dn_kernel(q_ref, k_ref, v_ref, z_ref, sm_ref, wq_ref, wk_ref, wv_ref, alog_ref, dtb_ref,
               ng_ref, o_ref, u_ref, kcum_ref, qdec_ref, kdec_ref, qk_ref, gt_ref, st_ref):
    c = DN_CHUNK
    ri = lax.broadcasted_iota(jnp.int32, (c, c), 0)
    ci = lax.broadcasted_iota(jnp.int32, (c, c), 1)
    incl = ri >= ci
    strict = ri > ci
    eye = ri == ci
    eye_f = eye.astype(F32)
    tri = incl.astype(F32)
    lane = lax.broadcasted_iota(jnp.int32, (1, LANES), 1)

    def conv_silu(cur, prev, w):
        cat = jnp.concatenate([prev, cur], axis=0)
        y = (w[3:4] * cur + w[2:3] * _shifted(cat, 1) + w[1:2] * _shifted(cat, 2)
             + w[0:1] * _shifted(cat, 3))
        return y * _sigmoid(y)

    def l2norm(x):
        return x * lax.rsqrt(jnp.sum(x * x, axis=-1, keepdims=True) + EPS)

    def heads():
        return [slice(h * DN_HEAD_DIM, (h + 1) * DN_HEAD_DIM) for h in range(DN_HEADS)]

    def rows_of(chunk):
        if isinstance(chunk, int):
            return chunk * c, max(chunk * c - HALO, 0)
        r0 = pl.multiple_of(chunk * c, c)
        return r0, pl.multiple_of(jnp.maximum(r0 - HALO, 0), HALO)

    def prepare(chunks):
        units = []
        raw, gcum_all, beta_all = [], [], []
        for chunk in chunks:
            r0, p0 = rows_of(chunk)
            rows = pl.ds(r0, c)
            prow = pl.ds(p0, HALO)
            keep_prev = jnp.where(chunk > 0, 1.0, 0.0)
            sm = sm_ref[rows, :]
            bsig = _sigmoid(sm)
            gsum = _dot(tri, -jnp.exp(alog_ref[...]) * _softplus(sm + dtb_ref[...]), HI)
            for h, cols in enumerate(heads()):
                units.append((chunk, rows, h, cols))
                raw.append([(ref[rows, cols].astype(F32), ref[prow, cols].astype(F32) * keep_prev,
                             w_ref[:, cols])
                            for ref, w_ref in ((q_ref, wq_ref), (k_ref, wk_ref), (v_ref, wv_ref))])
                gcum_all.append(gsum)
                beta_all.append(bsig)
        hs = range(len(units))
        gc, beta, q, k, v, decay = [], [], [], [], [], []
        for u in hs:
            head = units[u][2]
            g1 = jnp.sum(jnp.where(lane == SM_A + head, gcum_all[u], 0.0), axis=-1, keepdims=True)
            b1 = jnp.sum(jnp.where(lane == SM_BETA + head, beta_all[u], 0.0), axis=-1, keepdims=True)
            gc.append(jnp.broadcast_to(g1, (c, DN_HEAD_DIM)))
            beta.append(jnp.broadcast_to(b1, (c, DN_HEAD_DIM)))
            q.append(l2norm(conv_silu(*raw[u][0])) * (DN_HEAD_DIM ** -0.5))
            k.append(l2norm(conv_silu(*raw[u][1])))
            v.append(conv_silu(*raw[u][2]))
            gc_sq = gc[u][:, :c]
            g_row = jnp.sum(jnp.where(eye, gc_sq, 0.0), axis=0, keepdims=True)
            decay.append(jnp.exp(jnp.where(incl, gc_sq - g_row, -jnp.inf)))
        kb = [k[h] * beta[h] for h in hs]
        k16 = [k[h].astype(BF16) for h in hs]
        kk = [_dot_nt(kb[h].astype(BF16), k16[h]) for h in hs]
        qk = [_dot_nt(q[h].astype(BF16), k16[h]) for h in hs]
        yield
        neg = [jnp.where(strict, -(kk[h] * decay[h]), 0.0) for h in hs]
        qsum = [eye_f + neg[h] for h in hs]
        neg16 = [neg[h].astype(BF16) for h in hs]
        power = [_dot(neg16[h], neg16[h]) for h in hs]
        yield
        for _ in range(4):
            both = [_dot(power[h].astype(BF16),
                         jnp.concatenate([qsum[h], power[h]], axis=1).astype(BF16)) for h in hs]
            yield
            qsum = [qsum[h] + both[h][:, :c] for h in hs]
            power = [both[h][:, c:] for h in hs]
        last = [_dot(power[h].astype(BF16), qsum[h].astype(BF16)) for h in hs]
        yield
        eg = [jnp.exp(gc[h]) for h in hs]
        rhs = [jnp.concatenate([v[h] * beta[h], kb[h] * eg[h]], axis=1) for h in hs]
        sol = [rhs[h] + _dot((qsum[h] + last[h] - eye_f).astype(BF16), rhs[h].astype(BF16)) for h in hs]
        yield
        for u, (chunk, rows, head, cols) in enumerate(units):
            g_last = gc[u][c - 1:c, :]
            u_ref[rows, cols] = sol[u][:, :DN_HEAD_DIM]
            kcum_ref[rows, cols] = sol[u][:, DN_HEAD_DIM:].astype(BF16)
            qdec_ref[rows, cols] = (q[u] * eg[u]).astype(BF16)
            kdec_ref[rows, cols] = (k[u] * jnp.exp(g_last - gc[u])).astype(BF16)
            qk_ref[rows, head * c:(head + 1) * c] = jnp.where(incl, qk[u] * decay[u], 0.0).astype(BF16)
            gt_ref[chunk, :, cols] = jnp.broadcast_to(jnp.exp(g_last), (SUBLANES, DN_HEAD_DIM))

    def scan(chunks):
        hs = range(DN_HEADS)
        loaded = []
        for chunk in chunks:
            rows = pl.ds(rows_of(chunk)[0], c)
            loaded.append((rows, [(u_ref[rows, cols], kcum_ref[rows, cols], qdec_ref[rows, cols],
                                   qk_ref[rows, h * c:(h + 1) * c], kdec_ref[rows, cols],
                                   gt_ref[chunk, 0:1, cols], z_ref[rows, cols].astype(F32))
                                  for h, cols in enumerate(heads())]))
        state = [st_ref[h] for h in hs]
        outs = []
        for rows, per_head in loaded:
            state16 = [state[h].astype(BF16) for h in hs]
            v16 = [(per_head[h][0] - _dot(per_head[h][1], state16[h])).astype(BF16) for h in hs]
            yield
            upd = [_dot_tn(per_head[h][4], v16[h]) for h in hs]
            out = [_dot(per_head[h][2], state16[h]) + _dot(per_head[h][3], v16[h]) for h in hs]
            yield
            state = [state[h] * per_head[h][5] + upd[h] for h in hs]
            outs.append((rows, [(_rmsnorm(out[h], ng_ref[...])
                                 * (per_head[h][6] * _sigmoid(per_head[h][6]))).astype(BF16) for h in hs]))
        for h in hs:
            st_ref[h] = state[h]
        for rows, out in outs:
            for h, cols in enumerate(heads()):
                o_ref[rows, cols] = out[h]

    def interleave(*stages):
        active = list(stages)
        while active:
            for gen in list(active):
                if next(gen, StopIteration) is StopIteration:
                    active.remove(gen)

    def pair(n):
        return [DN_STREAMS * n + t for t in range(DN_STREAMS)]

    n_iter = N_CHUNKS // DN_STREAMS
    st_ref[...] = jnp.zeros_like(st_ref)
    interleave(prepare(pair(0)))

    def body(n, carry):
        interleave(scan(pair(n - 1)), prepare(pair(n)))
        return carry

    lax.fori_loop(1, n_iter, body, 0)
    interleave(scan(pair(n_iter - 1)))


def _deltanet(proj, small, conv_w, alog_row, dtb_row, norm_gain, layer):
    qb = COL_DN // BRANCH_WIDTH
    zb = COL_DNZ // BRANCH_WIDTH

    def colspec(blk):
        return pl.BlockSpec((SEQ, BRANCH_WIDTH), lambda b: (b, blk))

    def wspec(blk):
        return _of_layer(layer, (4, BRANCH_WIDTH), lambda b: (0, blk))

    row = _of_layer(layer, (1, LANES))
    return pl.pallas_call(
        _dn_kernel,
        grid=(BATCH,),
        in_specs=[
            colspec(qb), colspec(qb + 1), colspec(qb + 2), colspec(zb),
            pl.BlockSpec((SEQ, LANES), lambda b: (b, 0)),
            wspec(0), wspec(1), wspec(2),
            row, row, row,
        ],
        out_specs=pl.BlockSpec((SEQ, BRANCH_WIDTH), lambda b: (b, 0)),
        out_shape=jax.ShapeDtypeStruct((TOKENS, BRANCH_WIDTH), BF16),
        scratch_shapes=[
            pltpu.VMEM((SEQ, BRANCH_WIDTH), F32),
            pltpu.VMEM((SEQ, BRANCH_WIDTH), BF16),
            pltpu.VMEM((SEQ, BRANCH_WIDTH), BF16),
            pltpu.VMEM((SEQ, BRANCH_WIDTH), BF16),
            pltpu.VMEM((SEQ, DN_HEADS * DN_CHUNK), BF16),
            pltpu.VMEM((N_CHUNKS, SUBLANES, BRANCH_WIDTH), F32),
            pltpu.VMEM((DN_HEADS, DN_HEAD_DIM, DN_HEAD_DIM), F32),
        ],
        compiler_params=_cparams(("parallel",)),
        name="deltanet",
    )(proj, proj, proj, proj, small, conv_w, conv_w, conv_w, alog_row, dtb_row, norm_gain)


def _merge_kernel(yf_ref, sb_ref, sc_ref, sv_ref, scp_ref, svp_ref, yd_ref, g0_ref, g1_ref, g2_ref,
                  x_ref, wb_ref, wo_ref, cw_ref, o_ref):
    i = pl.program_id(0)
    keep_prev = jnp.where(i % (SEQ // MERGE_TM) != 0, 1.0, 0.0)
    cur = sc_ref[...].astype(F32) * sv_ref[...].astype(F32)
    prev = scp_ref[...].astype(F32) * svp_ref[...].astype(F32) * keep_prev
    cat = jnp.concatenate([prev, cur], axis=0)
    cw = cw_ref[...]
    conv = cw[2:3] * cur + cw[1:2] * _shifted(cat, 1) + cw[0:1] * _shifted(cat, 2)
    y_sc = sb_ref[...].astype(F32) * conv
    merged = (_sigmoid(g0_ref[...].astype(F32)) * _dot(yf_ref[...], wb_ref[0])
              + _sigmoid(g1_ref[...].astype(F32)) * _dot(y_sc.astype(BF16), wb_ref[1])
              + _sigmoid(g2_ref[...].astype(F32)) * _dot(yd_ref[...], wb_ref[2]))
    o_ref[...] = x_ref[...] + _dot(merged.astype(BF16), wo_ref[...])


def _merge(x, proj, y_fox, y_dn, w_branch, w_o, sc_conv_w, layer):
    tm = MERGE_TM
    scb = COL_SC // BRANCH_WIDTH
    gb = COL_GATE // D_MODEL
    hpb = tm // HALO

    def prev_rows(i):
        return jnp.maximum(i * hpb - 1, 0)

    return pl.pallas_call(
        _merge_kernel,
        grid=(TOKENS // tm,),
        in_specs=[
            pl.BlockSpec((tm, BRANCH_WIDTH), lambda i: (i, 0)),
            pl.BlockSpec((tm, BRANCH_WIDTH), lambda i: (i, scb)),
            pl.BlockSpec((tm, BRANCH_WIDTH), lambda i: (i, scb + 1)),
            pl.BlockSpec((tm, BRANCH_WIDTH), lambda i: (i, scb + 2)),
            pl.BlockSpec((HALO, BRANCH_WIDTH), lambda i: (prev_rows(i), scb + 1)),
            pl.BlockSpec((HALO, BRANCH_WIDTH), lambda i: (prev_rows(i), scb + 2)),
            pl.BlockSpec((tm, BRANCH_WIDTH), lambda i: (i, 0)),
            pl.BlockSpec((tm, D_MODEL), lambda i: (i, gb)),
            pl.BlockSpec((tm, D_MODEL), lambda i: (i, gb + 1)),
            pl.BlockSpec((tm, D_MODEL), lambda i: (i, gb + 2)),
            pl.BlockSpec((tm, D_MODEL), lambda i: (i, 0)),
            _of_layer(layer, (3, BRANCH_WIDTH, D_MODEL), resident=True),
            _of_layer(layer, (D_MODEL, D_MODEL), resident=True),
            _of_layer(layer, (3, BRANCH_WIDTH)),
        ],
        out_specs=pl.BlockSpec((tm, D_MODEL), lambda i: (i, 0)),
        out_shape=jax.ShapeDtypeStruct((TOKENS, D_MODEL), F32),
        compiler_params=_cparams(("parallel",)),
        name="merge",
    )(y_fox, proj, proj, proj, proj, proj, y_dn, proj, proj, proj, x, w_branch, w_o, sc_conv_w)


def _ffn_kernel(x_ref, xp_ref, gf_ref, wup_ref, cw_ref, wd_ref, gp_ref, wpg_ref, p_ref, wple_ref, o_ref,
                act_ref):
    i = pl.program_id(0)
    keep_prev = jnp.where(i % (SEQ // FFN_TM) != 0, 1.0, 0.0)
    xx = jnp.concatenate([xp_ref[...] * keep_prev, x_ref[...]], axis=0)
    hn = _rmsnorm(xx, gf_ref[...]).astype(BF16)

    def branch(cols):
        u = _dot(hn, wup_ref[:, cols])
        cw = cw_ref[:, cols]
        return cw[2:3] * u[HALO:] + cw[1:2] * _shifted(u, 1) + cw[0:1] * _shifted(u, 2)

    for j in range(FFN_NF):
        gate = branch(slice(j * FFN_TF, (j + 1) * FFN_TF))
        val = branch(slice(D_FF + j * FFN_TF, D_FF + (j + 1) * FFN_TF))
        act_ref[:, j * FFN_TF:(j + 1) * FFN_TF] = (gate * _sigmoid(gate) * val).astype(BF16)

    x1 = x_ref[...] + _dot(act_ref[...], wd_ref[...])
    h2 = _rmsnorm(x1, gp_ref[...]).astype(BF16)
    pgate = _sigmoid(_dot(h2, wpg_ref[...]))
    emb = _dot(p_ref[...].astype(BF16), wple_ref[...])
    o_ref[...] = x1 + pgate * emb


def _ffn(x, p, g_ffn, w_up, conv_w, w_down, g_ple, w_pg, w_ple, layer):
    tm = FFN_TM
    hpb = tm // HALO

    def whole(shape):
        return _of_layer(layer, shape, resident=True)

    return pl.pallas_call(
        _ffn_kernel,
        grid=(TOKENS // tm,),
        in_specs=[
            pl.BlockSpec((tm, D_MODEL), lambda i: (i, 0)),
            pl.BlockSpec((HALO, D_MODEL), lambda i: (jnp.maximum(i * hpb - 1, 0), 0)),
            whole((1, D_MODEL)),
            whole((D_MODEL, 2 * D_FF)),
            whole((3, 2 * D_FF)),
            whole((D_FF, D_MODEL)),
            whole((1, D_MODEL)),
            whole((D_MODEL, D_MODEL)),
            _of_layer(layer, (tm, PLE_DIM), lambda i: (i, 0)),
            whole((PLE_DIM, D_MODEL)),
        ],
        out_specs=pl.BlockSpec((tm, D_MODEL), lambda i: (i, 0)),
        out_shape=jax.ShapeDtypeStruct((TOKENS, D_MODEL), F32),
        scratch_shapes=[pltpu.VMEM((tm, D_FF), BF16)],
        compiler_params=_cparams(("parallel",)),
        name="ffn",
    )(x, x, g_ffn, w_up, conv_w, w_down, g_ple, w_pg, p, w_ple)


def _lane_rows(values, offset):
    depth, n = values.shape
    return jnp.pad(values.astype(F32), ((0, 0), (offset, LANES - offset - n)))[:, None, :]


def kernel(x, p, g_mix, w_in, b_fox_f, fox_q_gain, fox_k_gain, sc_conv_w, dn_conv_w, dn_a_log,
           dn_dt_bias, dn_norm_gain, w_branch, w_o, g_ffn, w_up, ffn_conv_w, w_down, g_ple,
           w_ple_gate, w_ple):
    depth = p.shape[0]
    x = x.reshape(TOKENS, D_MODEL)
    p = p.reshape(depth, TOKENS, PLE_DIM)

    w_big, w_small = _regroup(w_in)
    w_branch16, w_o16, w_up16, w_down16 = (w.astype(BF16) for w in (w_branch, w_o, w_up, w_down))
    w_pg16, w_ple16 = w_ple_gate.astype(BF16), w_ple.astype(BF16)
    g_mix3, g_ffn3, g_ple3, dn_gain3 = (g[:, None, :] for g in (g_mix, g_ffn, g_ple, dn_norm_gain))
    q_gain3 = jnp.tile(fox_q_gain, (1, 2))[:, None, :]
    k_gain3 = jnp.tile(fox_k_gain, (1, 2))[:, None, :]
    fox_bias = _lane_rows(b_fox_f, SM_FOXF)
    a_log = _lane_rows(dn_a_log, SM_A)
    dt_bias = _lane_rows(dn_dt_bias, SM_A)

    for layer in range(depth):
        proj, small = _proj(x, g_mix3, w_big, w_small, layer)
        cf = _cumf(small, fox_bias, layer)
        y_fox = _fox(proj, cf, q_gain3, k_gain3, layer)
        y_dn = _deltanet(proj, small, dn_conv_w, a_log, dt_bias, dn_gain3, layer)
        x = _merge(x, proj, y_fox, y_dn, w_branch16, w_o16, sc_conv_w, layer)
        x = _ffn(x, p, g_ffn3, w_up16, ffn_conv_w, w_down16, g_ple3, w_pg16, w_ple16, layer)
    return x.reshape(BATCH, SEQ, D_MODEL)
```

```python
import jax
import jax.numpy as jnp
import numpy as np
from jax import lax
from jax.experimental import pallas as pl
from jax.experimental.pallas import tpu as pltpu

F32 = jnp.float32
BF16 = jnp.bfloat16
HI = lax.Precision.HIGHEST

D_MODEL = 1024
BATCH = 8
SEQ = 2048
TOKENS = BATCH * SEQ
PLE_DIM = 256
BRANCH_WIDTH = 512
FOX_HEADS = 8
FOX_HEAD_DIM = 64
DN_HEADS = 4
DN_HEAD_DIM = 128
DN_CHUNK = 64
N_CHUNKS = SEQ // DN_CHUNK
D_FF = 2816
EPS = 1e-6

LANES = 128
SUBLANES = 8
HALO = 16

PROJ_WIDTH = 8192
COL_FOX = 0
COL_SC = 1536
COL_DN = 3072
COL_DNZ = 4608
COL_GATE = 5120
SM_FOXF = 0
SM_BETA = 8
SM_A = 12

PROJ_TM = 1024
PROJ_TN = 2048
FOX_TQ = 512
FOX_TK = 512
MERGE_TM = 512
FFN_TM = 512
FFN_TF = 256
FFN_NF = D_FF // FFN_TF

VMEM_LIMIT = 48 * 1024 * 1024


def _cparams(sem):
    return pltpu.CompilerParams(dimension_semantics=sem, vmem_limit_bytes=VMEM_LIMIT)


def _sigmoid(x):
    return 1.0 / (1.0 + jnp.exp(-x))


def _silu(x):
    half = 0.5 * x
    return half + half * jnp.tanh(half)


def _softplus(x):
    return jnp.maximum(x, 0.0) + jnp.log(1.0 + jnp.exp(-jnp.abs(x)))


def _dot(a, b, precision=None):
    return jnp.dot(a, b, preferred_element_type=F32, precision=precision)


def _dot_nt(a, b):
    return lax.dot_general(a, b, (((1,), (1,)), ((), ())), preferred_element_type=F32)


def _dot_tn(a, b):
    return lax.dot_general(a, b, (((0,), (0,)), ((), ())), preferred_element_type=F32)


def _rmsnorm(x, gain):
    ms = jnp.mean(x * x, axis=-1, keepdims=True)
    return x * lax.rsqrt(ms + EPS) * gain


def _shifted(cat, shift):
    if shift == 0:
        return cat[HALO:]
    return pltpu.roll(cat, shift, 0)[HALO:]


def _bf16_split3(x):
    hi = x.astype(BF16).astype(F32)
    r = x - hi
    mid = r.astype(BF16).astype(F32)
    lo = (r - mid).astype(BF16).astype(F32)
    return hi, mid, lo


IN_F = 3 * BRANCH_WIDTH
IN_SC = IN_F + FOX_HEADS
IN_DN = IN_SC + 3 * BRANCH_WIDTH
IN_B = IN_DN + 3 * BRANCH_WIDTH
IN_Z = IN_B + 2 * DN_HEADS
IN_G = IN_Z + BRANCH_WIDTH
IN_WIDTH = IN_G + 3 * D_MODEL
REGROUP_ROWS = 256
assert IN_F % LANES == SM_FOXF and SM_FOXF + FOX_HEADS == SM_BETA
assert IN_B % LANES == SM_BETA and SM_BETA + DN_HEADS == SM_A


def _regroup_kernel(w_ref, big_ref, small_ref):
    col = 0
    for a, b in ((0, IN_F), (IN_SC, IN_DN), (IN_DN, IN_B), (IN_Z, IN_G), (IN_G, IN_WIDTH)):
        big_ref[:, col:col + (b - a)] = w_ref[:, a:b].astype(BF16)
        col += b - a
    lane = lax.broadcasted_iota(jnp.int32, (1, LANES), 1)
    f_tile = w_ref[:, IN_F - SM_FOXF:IN_F - SM_FOXF + LANES]
    b_tile = w_ref[:, IN_B - SM_BETA:IN_B - SM_BETA + LANES]
    small_ref[...] = jnp.where(lane < SM_BETA, f_tile,
                               jnp.where(lane < SM_A + DN_HEADS, b_tile, 0.0)).astype(BF16)


def _regroup(w_in):
    depth = w_in.shape[0]
    return pl.pallas_call(
        _regroup_kernel,
        grid=(depth, D_MODEL // REGROUP_ROWS),
        in_specs=[pl.BlockSpec((None, REGROUP_ROWS, IN_WIDTH), lambda l, i: (l, i, 0))],
        out_specs=[
            pl.BlockSpec((None, REGROUP_ROWS, PROJ_WIDTH), lambda l, i: (l, i, 0)),
            pl.BlockSpec((None, REGROUP_ROWS, LANES), lambda l, i: (l, i, 0)),
        ],
        out_shape=[
            jax.ShapeDtypeStruct((depth, D_MODEL, PROJ_WIDTH), BF16),
            jax.ShapeDtypeStruct((depth, D_MODEL, LANES), BF16),
        ],
        compiler_params=_cparams(("parallel", "parallel")),
        name="regroup",
    )(w_in)


def _proj_kernel(x_ref, g_ref, w_ref, ws_ref, o_ref, os_ref, hn_ref):
    @pl.when(pl.program_id(1) == 0)
    def _():
        hn = _rmsnorm(x_ref[...], g_ref[...]).astype(BF16)
        hn_ref[...] = hn
        os_ref[...] = _dot(hn, ws_ref[...])

    o_ref[...] = _dot(hn_ref[...], w_ref[...]).astype(BF16)


def _of_layer(layer, block, index_map=None, resident=False):
    if index_map is None:
        index_map = lambda *grid: (0,) * len(block)
    mode = dict(pipeline_mode=pl.Buffered(1)) if resident else {}
    return pl.BlockSpec((None,) + tuple(block), lambda *grid: (layer,) + tuple(index_map(*grid)), **mode)


def _proj(x, gain, w_big, w_small, layer):
    grid = (TOKENS // PROJ_TM, PROJ_WIDTH // PROJ_TN)
    return pl.pallas_call(
        _proj_kernel,
        grid=grid,
        in_specs=[
            pl.BlockSpec((PROJ_TM, D_MODEL), lambda i, j: (i, 0)),
            _of_layer(layer, (1, D_MODEL)),
            _of_layer(layer, (D_MODEL, PROJ_TN), lambda i, j: (0, j)),
            _of_layer(layer, (D_MODEL, LANES)),
        ],
        out_specs=[
            pl.BlockSpec((PROJ_TM, PROJ_TN), lambda i, j: (i, j)),
            pl.BlockSpec((PROJ_TM, LANES), lambda i, j: (i, 0)),
        ],
        out_shape=[
            jax.ShapeDtypeStruct((TOKENS, PROJ_WIDTH), BF16),
            jax.ShapeDtypeStruct((TOKENS, LANES), F32),
        ],
        scratch_shapes=[pltpu.VMEM((PROJ_TM, D_MODEL), BF16)],
        compiler_params=_cparams(("parallel", "arbitrary")),
        name="proj",
    )(x, gain, w_big, w_small)


CUMF_BLK = 128


def _cumf_kernel(sm_ref, bf_ref, cf_ref):
    r = lax.broadcasted_iota(jnp.int32, (CUMF_BLK, CUMF_BLK), 0)
    c = lax.broadcasted_iota(jnp.int32, (CUMF_BLK, CUMF_BLK), 1)
    tri = (r >= c).astype(F32)
    carry = jnp.zeros((1, LANES), F32)
    for blk in range(SEQ // CUMF_BLK):
        rows = slice(blk * CUMF_BLK, (blk + 1) * CUMF_BLK)
        log_f = -_softplus(-(sm_ref[rows, :] + bf_ref[...]))
        cum = _dot(tri, log_f, HI) + carry
        carry = cum[CUMF_BLK - 1:CUMF_BLK, :]
        cf_ref[0, rows, :] = cum


def _cumf(small, bias_row, layer):
    return pl.pallas_call(
        _cumf_kernel,
        grid=(BATCH,),
        in_specs=[
            pl.BlockSpec((SEQ, LANES), lambda b: (b, 0)),
            _of_layer(layer, (1, LANES)),
        ],
        out_specs=pl.BlockSpec((1, SEQ, LANES), lambda b: (b, 0, 0)),
        out_shape=jax.ShapeDtypeStruct((BATCH, SEQ, LANES), F32),
        compiler_params=_cparams(("parallel",)),
        name="cumf",
    )(small, bias_row)


AUG = FOX_HEAD_DIM


def _fox_selectors():
    sel = np.zeros((2, FOX_HEADS // 2, 4 * LANES, 2 * LANES), np.float32)
    ones = np.zeros((2, 1, 2 * LANES), np.float32)
    for hp in range(FOX_HEADS // 2):
        for e in range(2):
            base = e * LANES
            for d in range(FOX_HEAD_DIM):
                sel[:, hp, e * FOX_HEAD_DIM + d, base + d] = 1.0
            for part in range(3):
                src = (1 + part) * LANES + SM_FOXF + 2 * hp + e
                sel[0, hp, src, base + AUG + part] = 1.0
                sel[1, hp, src, base + AUG + 3 + part] = -1.0
    for e in range(2):
        ones[0, 0, e * LANES + AUG + 3:e * LANES + AUG + 6] = 1.0
        ones[1, 0, e * LANES + AUG:e * LANES + AUG + 3] = 1.0
    return jnp.asarray(sel, BF16), jnp.asarray(ones, F32)


def _fox_kernel(q_ref, qnext_ref, k_ref, v_ref, cfq_ref, cfqnext_ref, cfk_ref, qg_ref, kg_ref, sel_ref,
                ones_ref, o_ref, ka_ref, vt_ref, qa_ref, acc_ref, s_ref):
    qi = pl.program_id(1)
    lane = lax.broadcasted_iota(jnp.int32, (1, LANES), 1)
    first_head = lane < FOX_HEAD_DIM

    def headnorm(x, gain):
        x2 = x * x
        s0 = jnp.sum(jnp.where(first_head, x2, 0.0), axis=-1, keepdims=True)
        s1 = jnp.sum(jnp.where(first_head, 0.0, x2), axis=-1, keepdims=True)
        inv = jnp.where(first_head,
                        lax.rsqrt(s0 * (1.0 / FOX_HEAD_DIM) + EPS),
                        lax.rsqrt(s1 * (1.0 / FOX_HEAD_DIM) + EPS))
        return x * inv * gain

    def cum_parts(cf):
        return [part.astype(BF16) for part in _bf16_split3(cf)]

    def augmented(xn, parts, hp, side):
        packed = jnp.concatenate([xn.astype(BF16)] + parts, axis=1)
        return (_dot(packed, sel_ref[side, hp]) + ones_ref[side]).astype(BF16)

    hs = range(FOX_HEADS)

    def hrows(h):
        return slice(h * FOX_HEAD_DIM, (h + 1) * FOX_HEAD_DIM)

    def scores(h, j):
        rows = pl.ds(pl.multiple_of(j * FOX_TK, FOX_TK), FOX_TK)
        return _dot_nt(ka_ref[h, rows, :], qa_ref[h])

    def prepare_queries(src_ref, cf):
        parts = cum_parts(cf)
        for hp in range(FOX_HEADS // 2):
            pair = slice(hp * LANES, (hp + 1) * LANES)
            qn = headnorm(src_ref[:, pair].astype(F32), qg_ref[...]) * (FOX_HEAD_DIM ** -0.5)
            qa = augmented(qn, parts, hp, 0)
            qa_ref[2 * hp] = qa[:, :LANES]
            qa_ref[2 * hp + 1] = qa[:, LANES:]

    @pl.when(qi == 0)
    def _():
        def body(c, carry):
            off = pl.multiple_of(c * FOX_TK, FOX_TK)
            rows = pl.ds(off, FOX_TK)
            parts = cum_parts(cfk_ref[0, rows, :])
            for hp in range(FOX_HEADS // 2):
                pair = slice(hp * LANES, (hp + 1) * LANES)
                kn = headnorm(k_ref[rows, pair].astype(F32), kg_ref[...])
                ka = augmented(kn, parts, hp, 1)
                ka_ref[2 * hp, rows, :] = ka[:, :LANES]
                ka_ref[2 * hp + 1, rows, :] = ka[:, LANES:]
                vt_ref[pair, rows] = v_ref[rows, pair].astype(F32).T.astype(BF16)
            return carry
        lax.fori_loop(0, SEQ // FOX_TK, body, 0)
        prepare_queries(q_ref, cfq_ref[0])
        for h in hs:
            s_ref[h] = scores(h, 0)

    acc_ref[...] = jnp.zeros_like(acc_ref)

    def step(j, nxt, carry, diagonal):
        rows = pl.ds(pl.multiple_of(j * FOX_TK, FOX_TK), FOX_TK)
        out = []
        for h in hs:
            s = s_ref[h]
            if diagonal:
                krow = lax.broadcasted_iota(jnp.int32, (FOX_TK, FOX_TQ), 0)
                qcol = lax.broadcasted_iota(jnp.int32, (FOX_TK, FOX_TQ), 1)
                s = jnp.where(qcol >= krow, s, -jnp.inf)
            m_new = jnp.maximum(carry[h][0], jnp.max(s, axis=0, keepdims=True))
            alpha = jnp.exp(carry[h][0] - m_new)
            p = jnp.exp(s - m_new)
            l_new = alpha * carry[h][1] + jnp.sum(p, axis=0, keepdims=True)
            if nxt is not None:
                s_ref[h] = scores(h, nxt)
            pv = _dot(vt_ref[hrows(h), rows], p.astype(BF16))
            acc_ref[hrows(h), :] = alpha * acc_ref[hrows(h), :] + pv
            out.append((m_new, l_new))
        return tuple(out)

    init = tuple((jnp.full((1, FOX_TQ), -jnp.inf, F32), jnp.zeros((1, FOX_TQ), F32)) for _ in hs)
    carry = lax.fori_loop(0, qi, lambda j, c: step(j, j + 1, c, False), init)

    def finish(more_queries):
        if more_queries:
            prepare_queries(qnext_ref, cfqnext_ref[0])
        final = step(qi, 0 if more_queries else None, carry, True)
        sub = lax.broadcasted_iota(jnp.int32, (LANES, 1), 0)
        for hp in range(FOX_HEADS // 2):
            pair = slice(hp * LANES, (hp + 1) * LANES)
            denom = jnp.where(sub < FOX_HEAD_DIM, final[2 * hp][1], final[2 * hp + 1][1])
            o_ref[:, pair] = (acc_ref[pair, :] / denom).T.astype(BF16)

    last_block = pl.num_programs(1) - 1
    pl.when(qi < last_block)(lambda: finish(True))
    pl.when(qi == last_block)(lambda: finish(False))


def _fox(proj, cf, q_gain2, k_gain2, layer):
    nq = SEQ // FOX_TQ
    fb = COL_FOX // BRANCH_WIDTH
    sel, ones = _fox_selectors()
    return pl.pallas_call(
        _fox_kernel,
        grid=(BATCH, nq),
        in_specs=[
            pl.BlockSpec((FOX_TQ, BRANCH_WIDTH), lambda b, qi: (b * nq + qi, fb)),
            pl.BlockSpec((FOX_TQ, BRANCH_WIDTH), lambda b, qi: (b * nq + jnp.minimum(qi + 1, nq - 1), fb)),
            pl.BlockSpec((SEQ, BRANCH_WIDTH), lambda b, qi: (b, fb + 1)),
            pl.BlockSpec((SEQ, BRANCH_WIDTH), lambda b, qi: (b, fb + 2)),
            pl.BlockSpec((1, FOX_TQ, LANES), lambda b, qi: (b, qi, 0)),
            pl.BlockSpec((1, FOX_TQ, LANES), lambda b, qi: (b, jnp.minimum(qi + 1, nq - 1), 0)),
            pl.BlockSpec((1, SEQ, LANES), lambda b, qi: (b, 0, 0)),
            _of_layer(layer, (1, LANES)),
            _of_layer(layer, (1, LANES)),
            pl.BlockSpec(sel.shape, lambda b, qi: (0, 0, 0, 0)),
            pl.BlockSpec(ones.shape, lambda b, qi: (0, 0, 0)),
        ],
        out_specs=pl.BlockSpec((FOX_TQ, BRANCH_WIDTH), lambda b, qi: (b * nq + qi, 0)),
        out_shape=jax.ShapeDtypeStruct((TOKENS, BRANCH_WIDTH), BF16),
        scratch_shapes=[
            pltpu.VMEM((FOX_HEADS, SEQ, LANES), BF16),
            pltpu.VMEM((BRANCH_WIDTH, SEQ), BF16),
            pltpu.VMEM((FOX_HEADS, FOX_TQ, LANES), BF16),
            pltpu.VMEM((BRANCH_WIDTH, FOX_TQ), F32),
            pltpu.VMEM((FOX_HEADS, FOX_TK, FOX_TQ), F32),
        ],
        compiler_params=_cparams(("parallel", "arbitrary")),
        name="fox",
    )(proj, proj, proj, proj, cf, cf, cf, q_gain2, k_gain2, sel, ones)


DN_STREAMS = 4

def _dn_kernel(q_ref, k_ref, v_ref, z_ref, sm_ref, wq_ref, wk_ref, wv_ref, alog_ref, dtb_ref,
               ng_ref, o_ref, u_ref, kcum_ref, qdec_ref, kdec_ref, qk_ref, gt_ref, st_ref):
    c = DN_CHUNK
    ri = lax.broadcasted_iota(jnp.int32, (c, c), 0)
    ci = lax.broadcasted_iota(jnp.int32, (c, c), 1)
    incl = ri >= ci
    strict = ri > ci
    eye = ri == ci
    eye_f = eye.astype(F32)
    tri3 = jnp.concatenate([jnp.where(incl, 1.0, 0.0)] * 3, axis=1).astype(BF16)
    lane = lax.broadcasted_iota(jnp.int32, (1, LANES), 1)

    def conv_silu(cur, prev, w):
        cat = jnp.concatenate([prev, cur], axis=0)
        y = (w[3:4] * cur + w[2:3] * _shifted(cat, 1) + w[1:2] * _shifted(cat, 2)
             + w[0:1] * _shifted(cat, 3))
        return _silu(y)

    def l2norm(x):
        return x * lax.rsqrt(jnp.sum(x * x, axis=-1, keepdims=True) + EPS)

    def heads():
        return [slice(h * DN_HEAD_DIM, (h + 1) * DN_HEAD_DIM) for h in range(DN_HEADS)]

    def rows_of(chunk):
        if isinstance(chunk, int):
            return chunk * c, max(chunk * c - HALO, 0)
        r0 = pl.multiple_of(chunk * c, c)
        return r0, pl.multiple_of(jnp.maximum(r0 - HALO, 0), HALO)

    def prepare(chunks):
        units = []
        raw, gcum_all, beta_all = [], [], []
        for chunk in chunks:
            r0, p0 = rows_of(chunk)
            rows = pl.ds(r0, c)
            prow = pl.ds(p0, HALO)
            keep_prev = jnp.where(chunk > 0, 1.0, 0.0)
            sm = sm_ref[rows, :]
            bsig = _sigmoid(sm)
            g_parts = _bf16_split3(-jnp.exp(alog_ref[...]) * _softplus(sm + dtb_ref[...]))
            gsum = _dot(tri3, jnp.concatenate(g_parts, axis=0).astype(BF16))
            for h, cols in enumerate(heads()):
                units.append((chunk, rows, h, cols))
                raw.append([(ref[rows, cols].astype(F32), ref[prow, cols].astype(F32) * keep_prev,
                             w_ref[:, cols])
                            for ref, w_ref in ((q_ref, wq_ref), (k_ref, wk_ref), (v_ref, wv_ref))])
                gcum_all.append(gsum)
                beta_all.append(bsig)
        hs = range(len(units))
        gc, beta, q, k, v, decay = [], [], [], [], [], []
        for u in hs:
            head = units[u][2]
            g1 = jnp.sum(jnp.where(lane == SM_A + head, gcum_all[u], 0.0), axis=-1, keepdims=True)
            b1 = jnp.sum(jnp.where(lane == SM_BETA + head, beta_all[u], 0.0), axis=-1, keepdims=True)
            gc.append(jnp.broadcast_to(g1, (c, DN_HEAD_DIM)))
            beta.append(jnp.broadcast_to(b1, (c, DN_HEAD_DIM)))
            q.append(l2norm(conv_silu(*raw[u][0])) * (DN_HEAD_DIM ** -0.5))
            k.append(l2norm(conv_silu(*raw[u][1])))
            v.append(conv_silu(*raw[u][2]))
            gc_sq = gc[u][:, :c]
            g_row = jnp.sum(jnp.where(eye, gc_sq, 0.0), axis=0, keepdims=True)
            decay.append(jnp.exp(jnp.where(incl, gc_sq - g_row, -jnp.inf)))
        kb = [k[h] * beta[h] for h in hs]
        k16 = [k[h].astype(BF16) for h in hs]
        kk = [_dot_nt(kb[h].astype(BF16), k16[h]) for h in hs]
        qk = [_dot_nt(q[h].astype(BF16), k16[h]) for h in hs]
        yield
        neg = [jnp.where(strict, -(kk[h] * decay[h]), 0.0) for h in hs]
        qsum = [eye_f + neg[h] for h in hs]
        neg16 = [neg[h].astype(BF16) for h in hs]
        power = [_dot(neg16[h], neg16[h]) for h in hs]
        yield
        for _ in range(4):
            both = [_dot(power[h].astype(BF16),
                         jnp.concatenate([qsum[h], power[h]], axis=1).astype(BF16)) for h in hs]
            yield
            qsum = [qsum[h] + both[h][:, :c] for h in hs]
            power = [both[h][:, c:] for h in hs]
        last = [_dot(power[h].astype(BF16), qsum[h].astype(BF16)) for h in hs]
        yield
        eg = [jnp.exp(gc[h]) for h in hs]
        rhs = [jnp.concatenate([v[h] * beta[h], kb[h] * eg[h]], axis=1) for h in hs]
        sol = [rhs[h] + _dot((qsum[h] + last[h] - eye_f).astype(BF16), rhs[h].astype(BF16)) for h in hs]
        yield
        for u, (chunk, rows, head, cols) in enumerate(units):
            g_last = gc[u][c - 1:c, :]
            u_ref[rows, cols] = sol[u][:, :DN_HEAD_DIM]
            kcum_ref[rows, cols] = sol[u][:, DN_HEAD_DIM:].astype(BF16)
            qdec_ref[rows, cols] = (q[u] * eg[u]).astype(BF16)
            kdec_ref[rows, cols] = (k[u] * jnp.exp(g_last - gc[u])).astype(BF16)
            qk_ref[rows, head * c:(head + 1) * c] = jnp.where(incl, qk[u] * decay[u], 0.0).astype(BF16)
            gt_ref[chunk, :, cols] = jnp.broadcast_to(jnp.exp(g_last), (SUBLANES, DN_HEAD_DIM))

    def scan(chunks):
        hs = range(DN_HEADS)
        loaded = []
        for chunk in chunks:
            rows = pl.ds(rows_of(chunk)[0], c)
            loaded.append((rows, [(u_ref[rows, cols], kcum_ref[rows, cols], qdec_ref[rows, cols],
                                   qk_ref[rows, h * c:(h + 1) * c], kdec_ref[rows, cols],
                                   gt_ref[chunk, 0:1, cols], z_ref[rows, cols].astype(F32))
                                  for h, cols in enumerate(heads())]))
        state = [st_ref[h] for h in hs]
        outs = []
        for rows, per_head in loaded:
            state16 = [state[h].astype(BF16) for h in hs]
            v16 = [(per_head[h][0] - _dot(per_head[h][1], state16[h])).astype(BF16) for h in hs]
            yield
            upd = [_dot_tn(per_head[h][4], v16[h]) for h in hs]
            out = [_dot(per_head[h][2], state16[h]) + _dot(per_head[h][3], v16[h]) for h in hs]
            yield
            state = [state[h] * per_head[h][5] + upd[h] for h in hs]
            outs.append((rows, [(_rmsnorm(out[h], ng_ref[...])
                                 * _silu(per_head[h][6])).astype(BF16) for h in hs]))
        for h in hs:
            st_ref[h] = state[h]
        for rows, out in outs:
            for h, cols in enumerate(heads()):
                o_ref[rows, cols] = out[h]

    def interleave(*stages):
        active = list(stages)
        while active:
            for gen in list(active):
                if next(gen, StopIteration) is StopIteration:
                    active.remove(gen)

    def pair(n):
        return [DN_STREAMS * n + t for t in range(DN_STREAMS)]

    n_iter = N_CHUNKS // DN_STREAMS
    st_ref[...] = jnp.zeros_like(st_ref)
    interleave(prepare(pair(0)))

    def body(n, carry):
        interleave(scan(pair(n - 1)), prepare(pair(n)))
        return carry

    lax.fori_loop(1, n_iter, body, 0)
    interleave(scan(pair(n_iter - 1)))


def _deltanet(proj, small, conv_w, alog_row, dtb_row, norm_gain, layer):
    qb = COL_DN // BRANCH_WIDTH
    zb = COL_DNZ // BRANCH_WIDTH

    def colspec(blk):
        return pl.BlockSpec((SEQ, BRANCH_WIDTH), lambda b: (b, blk))

    def wspec(blk):
        return _of_layer(layer, (4, BRANCH_WIDTH), lambda b: (0, blk))

    row = _of_layer(layer, (1, LANES))
    return pl.pallas_call(
        _dn_kernel,
        grid=(BATCH,),
        in_specs=[
            colspec(qb), colspec(qb + 1), colspec(qb + 2), colspec(zb),
            pl.BlockSpec((SEQ, LANES), lambda b: (b, 0)),
            wspec(0), wspec(1), wspec(2),
            row, row, row,
        ],
        out_specs=pl.BlockSpec((SEQ, BRANCH_WIDTH), lambda b: (b, 0)),
        out_shape=jax.ShapeDtypeStruct((TOKENS, BRANCH_WIDTH), BF16),
        scratch_shapes=[
            pltpu.VMEM((SEQ, BRANCH_WIDTH), F32),
            pltpu.VMEM((SEQ, BRANCH_WIDTH), BF16),
            pltpu.VMEM((SEQ, BRANCH_WIDTH), BF16),
            pltpu.VMEM((SEQ, BRANCH_WIDTH), BF16),
            pltpu.VMEM((SEQ, DN_HEADS * DN_CHUNK), BF16),
            pltpu.VMEM((N_CHUNKS, SUBLANES, BRANCH_WIDTH), F32),
            pltpu.VMEM((DN_HEADS, DN_HEAD_DIM, DN_HEAD_DIM), F32),
        ],
        compiler_params=_cparams(("parallel",)),
        name="deltanet",
    )(proj, proj, proj, proj, small, conv_w, conv_w, conv_w, alog_row, dtb_row, norm_gain)


def _merge_kernel(yf_ref, sb_ref, sc_ref, sv_ref, scp_ref, svp_ref, yd_ref, g0_ref, g1_ref, g2_ref,
                  x_ref, wb_ref, wo_ref, cw_ref, o_ref):
    i = pl.program_id(0)
    keep_prev = jnp.where(i % (SEQ // MERGE_TM) != 0, 1.0, 0.0)
    cur = sc_ref[...].astype(F32) * sv_ref[...].astype(F32)
    prev = scp_ref[...].astype(F32) * svp_ref[...].astype(F32) * keep_prev
    cat = jnp.concatenate([prev, cur], axis=0)
    cw = cw_ref[...]
    conv = cw[2:3] * cur + cw[1:2] * _shifted(cat, 1) + cw[0:1] * _shifted(cat, 2)
    y_sc = sb_ref[...].astype(F32) * conv
    merged = (_sigmoid(g0_ref[...].astype(F32)) * _dot(yf_ref[...], wb_ref[0])
              + _sigmoid(g1_ref[...].astype(F32)) * _dot(y_sc.astype(BF16), wb_ref[1])
              + _sigmoid(g2_ref[...].astype(F32)) * _dot(yd_ref[...], wb_ref[2]))
    o_ref[...] = x_ref[...] + _dot(merged.astype(BF16), wo_ref[...])


def _merge(x, proj, y_fox, y_dn, w_branch, w_o, sc_conv_w, layer):
    tm = MERGE_TM
    scb = COL_SC // BRANCH_WIDTH
    gb = COL_GATE // D_MODEL
    hpb = tm // HALO

    def prev_rows(i):
        return jnp.maximum(i * hpb - 1, 0)

    return pl.pallas_call(
        _merge_kernel,
        grid=(TOKENS // tm,),
        in_specs=[
            pl.BlockSpec((tm, BRANCH_WIDTH), lambda i: (i, 0)),
            pl.BlockSpec((tm, BRANCH_WIDTH), lambda i: (i, scb)),
            pl.BlockSpec((tm, BRANCH_WIDTH), lambda i: (i, scb + 1)),
            pl.BlockSpec((tm, BRANCH_WIDTH), lambda i: (i, scb + 2)),
            pl.BlockSpec((HALO, BRANCH_WIDTH), lambda i: (prev_rows(i), scb + 1)),
            pl.BlockSpec((HALO, BRANCH_WIDTH), lambda i: (prev_rows(i), scb + 2)),
            pl.BlockSpec((tm, BRANCH_WIDTH), lambda i: (i, 0)),
            pl.BlockSpec((tm, D_MODEL), lambda i: (i, gb)),
            pl.BlockSpec((tm, D_MODEL), lambda i: (i, gb + 1)),
            pl.BlockSpec((tm, D_MODEL), lambda i: (i, gb + 2)),
            pl.BlockSpec((tm, D_MODEL), lambda i: (i, 0)),
            _of_layer(layer, (3, BRANCH_WIDTH, D_MODEL), resident=True),
            _of_layer(layer, (D_MODEL, D_MODEL), resident=True),
            _of_layer(layer, (3, BRANCH_WIDTH)),
        ],
        out_specs=pl.BlockSpec((tm, D_MODEL), lambda i: (i, 0)),
        out_shape=jax.ShapeDtypeStruct((TOKENS, D_MODEL), F32),
        compiler_params=_cparams(("parallel",)),
        name="merge",
    )(y_fox, proj, proj, proj, proj, proj, y_dn, proj, proj, proj, x, w_branch, w_o, sc_conv_w)


def _ffn_kernel(x_ref, xp_ref, gf_ref, wup_ref, cw_ref, wd_ref, gp_ref, wpg_ref, p_ref, wple_ref, o_ref,
                act_ref):
    i = pl.program_id(0)
    keep_prev = jnp.where(i % (SEQ // FFN_TM) != 0, 1.0, 0.0)
    xx = jnp.concatenate([xp_ref[...] * keep_prev, x_ref[...]], axis=0)
    hn = _rmsnorm(xx, gf_ref[...]).astype(BF16)

    def branch(cols):
        u = _dot(hn, wup_ref[:, cols])
        cw = cw_ref[:, cols]
        return cw[2:3] * u[HALO:] + cw[1:2] * _shifted(u, 1) + cw[0:1] * _shifted(u, 2)

    for j in range(FFN_NF):
        gate = branch(slice(j * FFN_TF, (j + 1) * FFN_TF))
        val = branch(slice(D_FF + j * FFN_TF, D_FF + (j + 1) * FFN_TF))
        act_ref[:, j * FFN_TF:(j + 1) * FFN_TF] = (_silu(gate) * val).astype(BF16)

    x1 = x_ref[...] + _dot(act_ref[...], wd_ref[...])
    h2 = _rmsnorm(x1, gp_ref[...]).astype(BF16)
    pgate = _sigmoid(_dot(h2, wpg_ref[...]))
    emb = _dot(p_ref[...].astype(BF16), wple_ref[...])
    o_ref[...] = x1 + pgate * emb


def _ffn(x, p, g_ffn, w_up, conv_w, w_down, g_ple, w_pg, w_ple, layer):
    tm = FFN_TM
    hpb = tm // HALO

    def whole(shape):
        return _of_layer(layer, shape, resident=True)

    return pl.pallas_call(
        _ffn_kernel,
        grid=(TOKENS // tm,),
        in_specs=[
            pl.BlockSpec((tm, D_MODEL), lambda i: (i, 0)),
            pl.BlockSpec((HALO, D_MODEL), lambda i: (jnp.maximum(i * hpb - 1, 0), 0)),
            whole((1, D_MODEL)),
            whole((D_MODEL, 2 * D_FF)),
            whole((3, 2 * D_FF)),
            whole((D_FF, D_MODEL)),
            whole((1, D_MODEL)),
            whole((D_MODEL, D_MODEL)),
            _of_layer(layer, (tm, PLE_DIM), lambda i: (i, 0)),
            whole((PLE_DIM, D_MODEL)),
        ],
        out_specs=pl.BlockSpec((tm, D_MODEL), lambda i: (i, 0)),
        out_shape=jax.ShapeDtypeStruct((TOKENS, D_MODEL), F32),
        scratch_shapes=[pltpu.VMEM((tm, D_FF), BF16)],
        compiler_params=_cparams(("parallel",)),
        name="ffn",
    )(x, x, g_ffn, w_up, conv_w, w_down, g_ple, w_pg, p, w_ple)


def _lane_rows(values, offset):
    depth, n = values.shape
    return jnp.pad(values.astype(F32), ((0, 0), (offset, LANES - offset - n)))[:, None, :]


def kernel(x, p, g_mix, w_in, b_fox_f, fox_q_gain, fox_k_gain, sc_conv_w, dn_conv_w, dn_a_log,
           dn_dt_bias, dn_norm_gain, w_branch, w_o, g_ffn, w_up, ffn_conv_w, w_down, g_ple,
           w_ple_gate, w_ple):
    depth = p.shape[0]
    x = x.reshape(TOKENS, D_MODEL)
    p = p.reshape(depth, TOKENS, PLE_DIM)

    w_big, w_small = _regroup(w_in)
    w_branch16, w_o16, w_up16, w_down16 = (w.astype(BF16) for w in (w_branch, w_o, w_up, w_down))
    w_pg16, w_ple16 = w_ple_gate.astype(BF16), w_ple.astype(BF16)
    g_mix3, g_ffn3, g_ple3, dn_gain3 = (g[:, None, :] for g in (g_mix, g_ffn, g_ple, dn_norm_gain))
    q_gain3 = jnp.tile(fox_q_gain, (1, 2))[:, None, :]
    k_gain3 = jnp.tile(fox_k_gain, (1, 2))[:, None, :]
    fox_bias = _lane_rows(b_fox_f, SM_FOXF)
    a_log = _lane_rows(dn_a_log, SM_A)
    dt_bias = _lane_rows(dn_dt_bias, SM_A)

    for layer in range(depth):
        proj, small = _proj(x, g_mix3, w_big, w_small, layer)
        cf = _cumf(small, fox_bias, layer)
        y_fox = _fox(proj, cf, q_gain3, k_gain3, layer)
        y_dn = _deltanet(proj, small, dn_conv_w, a_log, dt_bias, dn_gain3, layer)
        x = _merge(x, proj, y_fox, y_dn, w_branch16, w_o16, sc_conv_w, layer)
        x = _ffn(x, p, g_ffn3, w_up16, ffn_conv_w, w_down16, g_ple3, w_pg16, w_ple16, layer)
    return x.reshape(BATCH, SEQ, D_MODEL)
```

```python
import jax
import jax.numpy as jnp
import numpy as np
from jax import lax
from jax.experimental import pallas as pl
from jax.experimental.pallas import tpu as pltpu

F32 = jnp.float32
BF16 = jnp.bfloat16
HI = lax.Precision.HIGHEST

D_MODEL = 1024
BATCH = 8
SEQ = 2048
TOKENS = BATCH * SEQ
PLE_DIM = 256
BRANCH_WIDTH = 512
FOX_HEADS = 8
FOX_HEAD_DIM = 64
DN_HEADS = 4
DN_HEAD_DIM = 128
DN_CHUNK = 64
N_CHUNKS = SEQ // DN_CHUNK
D_FF = 2816
EPS = 1e-6

LANES = 128
SUBLANES = 8
HALO = 16

PROJ_WIDTH = 8192
COL_FOX = 0
COL_SC = 1536
COL_DN = 3072
COL_DNZ = 4608
COL_GATE = 5120
SM_FOXF = 0
SM_BETA = 8
SM_A = 12

PROJ_TM = 1024
PROJ_TN = 2048
FOX_TQ = 512
FOX_TK = 512
MERGE_TM = 512
FFN_TM = 512
FFN_TF = 256
FFN_NF = D_FF // FFN_TF

VMEM_LIMIT = 48 * 1024 * 1024


def _cparams(sem):
    return pltpu.CompilerParams(dimension_semantics=sem, vmem_limit_bytes=VMEM_LIMIT)


def _sigmoid(x):
    return 1.0 / (1.0 + jnp.exp(-x))


def _silu(x):
    half = 0.5 * x
    return half + half * jnp.tanh(half)


def _softplus(x):
    return jnp.maximum(x, 0.0) + jnp.log(1.0 + jnp.exp(-jnp.abs(x)))


def _dot(a, b, precision=None):
    return jnp.dot(a, b, preferred_element_type=F32, precision=precision)


def _dot_nt(a, b):
    return lax.dot_general(a, b, (((1,), (1,)), ((), ())), preferred_element_type=F32)


def _dot_tn(a, b):
    return lax.dot_general(a, b, (((0,), (0,)), ((), ())), preferred_element_type=F32)


def _rmsnorm(x, gain):
    ms = jnp.mean(x * x, axis=-1, keepdims=True)
    return x * lax.rsqrt(ms + EPS) * gain


def _shifted(cat, shift):
    if shift == 0:
        return cat[HALO:]
    return pltpu.roll(cat, shift, 0)[HALO:]


def _bf16_split3(x):
    hi = x.astype(BF16).astype(F32)
    r = x - hi
    mid = r.astype(BF16).astype(F32)
    lo = (r - mid).astype(BF16).astype(F32)
    return hi, mid, lo


IN_F = 3 * BRANCH_WIDTH
IN_SC = IN_F + FOX_HEADS
IN_DN = IN_SC + 3 * BRANCH_WIDTH
IN_B = IN_DN + 3 * BRANCH_WIDTH
IN_Z = IN_B + 2 * DN_HEADS
IN_G = IN_Z + BRANCH_WIDTH
IN_WIDTH = IN_G + 3 * D_MODEL
REGROUP_ROWS = 256
assert IN_F % LANES == SM_FOXF and SM_FOXF + FOX_HEADS == SM_BETA
assert IN_B % LANES == SM_BETA and SM_BETA + DN_HEADS == SM_A


def _regroup_kernel(w_ref, big_ref, small_ref):
    col = 0
    for a, b in ((0, IN_F), (IN_SC, IN_DN), (IN_DN, IN_B), (IN_Z, IN_G), (IN_G, IN_WIDTH)):
        big_ref[:, col:col + (b - a)] = w_ref[:, a:b].astype(BF16)
        col += b - a
    lane = lax.broadcasted_iota(jnp.int32, (1, LANES), 1)
    f_tile = w_ref[:, IN_F - SM_FOXF:IN_F - SM_FOXF + LANES]
    b_tile = w_ref[:, IN_B - SM_BETA:IN_B - SM_BETA + LANES]
    small_ref[...] = jnp.where(lane < SM_BETA, f_tile,
                               jnp.where(lane < SM_A + DN_HEADS, b_tile, 0.0)).astype(BF16)


def _regroup(w_in):
    depth = w_in.shape[0]
    return pl.pallas_call(
        _regroup_kernel,
        grid=(depth, D_MODEL // REGROUP_ROWS),
        in_specs=[pl.BlockSpec((None, REGROUP_ROWS, IN_WIDTH), lambda l, i: (l, i, 0))],
        out_specs=[
            pl.BlockSpec((None, REGROUP_ROWS, PROJ_WIDTH), lambda l, i: (l, i, 0)),
            pl.BlockSpec((None, REGROUP_ROWS, LANES), lambda l, i: (l, i, 0)),
        ],
        out_shape=[
            jax.ShapeDtypeStruct((depth, D_MODEL, PROJ_WIDTH), BF16),
            jax.ShapeDtypeStruct((depth, D_MODEL, LANES), BF16),
        ],
        compiler_params=_cparams(("parallel", "parallel")),
        name="regroup",
    )(w_in)


def _proj_kernel(x_ref, g_ref, w_ref, ws_ref, o_ref, os_ref, hn_ref):
    @pl.when(pl.program_id(1) == 0)
    def _():
        hn = _rmsnorm(x_ref[...], g_ref[...]).astype(BF16)
        hn_ref[...] = hn
        os_ref[...] = _dot(hn, ws_ref[...])

    o_ref[...] = _dot(hn_ref[...], w_ref[...]).astype(BF16)


def _of_layer(layer, block, index_map=None, resident=False):
    if index_map is None:
        index_map = lambda *grid: (0,) * len(block)
    mode = dict(pipeline_mode=pl.Buffered(1)) if resident else {}
    return pl.BlockSpec((None,) + tuple(block), lambda *grid: (layer,) + tuple(index_map(*grid)), **mode)


def _proj(x, gain, w_big, w_small, layer):
    grid = (TOKENS // PROJ_TM, PROJ_WIDTH // PROJ_TN)
    return pl.pallas_call(
        _proj_kernel,
        grid=grid,
        in_specs=[
            pl.BlockSpec((PROJ_TM, D_MODEL), lambda i, j: (i, 0)),
            _of_layer(layer, (1, D_MODEL)),
            _of_layer(layer, (D_MODEL, PROJ_TN), lambda i, j: (0, j)),
            _of_layer(layer, (D_MODEL, LANES)),
        ],
        out_specs=[
            pl.BlockSpec((PROJ_TM, PROJ_TN), lambda i, j: (i, j)),
            pl.BlockSpec((PROJ_TM, LANES), lambda i, j: (i, 0)),
        ],
        out_shape=[
            jax.ShapeDtypeStruct((TOKENS, PROJ_WIDTH), BF16),
            jax.ShapeDtypeStruct((TOKENS, LANES), F32),
        ],
        scratch_shapes=[pltpu.VMEM((PROJ_TM, D_MODEL), BF16)],
        compiler_params=_cparams(("parallel", "arbitrary")),
        name="proj",
    )(x, gain, w_big, w_small)


CUMF_BLK = 128


def _cumf_kernel(sm_ref, bf_ref, cf_ref):
    r = lax.broadcasted_iota(jnp.int32, (CUMF_BLK, CUMF_BLK), 0)
    c = lax.broadcasted_iota(jnp.int32, (CUMF_BLK, CUMF_BLK), 1)
    tri = (r >= c).astype(F32)
    carry = jnp.zeros((1, LANES), F32)
    for blk in range(SEQ // CUMF_BLK):
        rows = slice(blk * CUMF_BLK, (blk + 1) * CUMF_BLK)
        log_f = -_softplus(-(sm_ref[rows, :] + bf_ref[...]))
        cum = _dot(tri, log_f, HI) + carry
        carry = cum[CUMF_BLK - 1:CUMF_BLK, :]
        cf_ref[0, rows, :] = cum


def _cumf(small, bias_row, layer):
    return pl.pallas_call(
        _cumf_kernel,
        grid=(BATCH,),
        in_specs=[
            pl.BlockSpec((SEQ, LANES), lambda b: (b, 0)),
            _of_layer(layer, (1, LANES)),
        ],
        out_specs=pl.BlockSpec((1, SEQ, LANES), lambda b: (b, 0, 0)),
        out_shape=jax.ShapeDtypeStruct((BATCH, SEQ, LANES), F32),
        compiler_params=_cparams(("parallel",)),
        name="cumf",
    )(small, bias_row)


AUG = FOX_HEAD_DIM
LOG2E = 1.4426950408889634


def _fox_selectors():
    sel = np.zeros((2, FOX_HEADS // 2, 4 * LANES, 2 * LANES), np.float32)
    ones = np.zeros((2, 1, 2 * LANES), np.float32)
    for hp in range(FOX_HEADS // 2):
        for e in range(2):
            base = e * LANES
            for d in range(FOX_HEAD_DIM):
                sel[:, hp, e * FOX_HEAD_DIM + d, base + d] = 1.0
            for part in range(3):
                src = (1 + part) * LANES + SM_FOXF + 2 * hp + e
                sel[0, hp, src, base + AUG + part] = 1.0
                sel[1, hp, src, base + AUG + 3 + part] = -1.0
    for e in range(2):
        ones[0, 0, e * LANES + AUG + 3:e * LANES + AUG + 6] = 1.0
        ones[1, 0, e * LANES + AUG:e * LANES + AUG + 3] = 1.0
    return jnp.asarray(sel, BF16), jnp.asarray(ones, F32)


def _fox_kernel(q_ref, qnext_ref, k_ref, v_ref, cfq_ref, cfqnext_ref, cfk_ref, qg_ref, kg_ref, sel_ref,
                ones_ref, o_ref, ka_ref, vt_ref, qa_ref, acc_ref, s_ref):
    qi = pl.program_id(1)
    lane = lax.broadcasted_iota(jnp.int32, (1, LANES), 1)
    first_head = lane < FOX_HEAD_DIM

    def headnorm(x, gain):
        x2 = x * x
        s0 = jnp.sum(jnp.where(first_head, x2, 0.0), axis=-1, keepdims=True)
        s1 = jnp.sum(jnp.where(first_head, 0.0, x2), axis=-1, keepdims=True)
        inv = jnp.where(first_head,
                        lax.rsqrt(s0 * (1.0 / FOX_HEAD_DIM) + EPS),
                        lax.rsqrt(s1 * (1.0 / FOX_HEAD_DIM) + EPS))
        return x * inv * gain

    def cum_parts(cf):
        return [part.astype(BF16) for part in _bf16_split3(cf * LOG2E)]

    def augmented(xn, parts, hp, side):
        packed = jnp.concatenate([xn.astype(BF16)] + parts, axis=1)
        return (_dot(packed, sel_ref[side, hp]) + ones_ref[side]).astype(BF16)

    hs = range(FOX_HEADS)

    def hrows(h):
        return slice(h * FOX_HEAD_DIM, (h + 1) * FOX_HEAD_DIM)

    def scores(h, j):
        rows = pl.ds(pl.multiple_of(j * FOX_TK, FOX_TK), FOX_TK)
        return _dot_nt(ka_ref[h, rows, :], qa_ref[h])

    def prepare_queries(src_ref, cf):
        parts = cum_parts(cf)
        for hp in range(FOX_HEADS // 2):
            pair = slice(hp * LANES, (hp + 1) * LANES)
            qn = headnorm(src_ref[:, pair].astype(F32), qg_ref[...]) * (FOX_HEAD_DIM ** -0.5 * LOG2E)
            qa = augmented(qn, parts, hp, 0)
            qa_ref[2 * hp] = qa[:, :LANES]
            qa_ref[2 * hp + 1] = qa[:, LANES:]

    @pl.when(qi == 0)
    def _():
        def body(c, carry):
            off = pl.multiple_of(c * FOX_TK, FOX_TK)
            rows = pl.ds(off, FOX_TK)
            parts = cum_parts(cfk_ref[0, rows, :])
            for hp in range(FOX_HEADS // 2):
                pair = slice(hp * LANES, (hp + 1) * LANES)
                kn = headnorm(k_ref[rows, pair].astype(F32), kg_ref[...])
                ka = augmented(kn, parts, hp, 1)
                ka_ref[2 * hp, rows, :] = ka[:, :LANES]
                ka_ref[2 * hp + 1, rows, :] = ka[:, LANES:]
                vt_ref[pair, rows] = v_ref[rows, pair].astype(F32).T.astype(BF16)
            return carry
        lax.fori_loop(0, SEQ // FOX_TK, body, 0)
        prepare_queries(q_ref, cfq_ref[0])
        for h in hs:
            s_ref[h] = scores(h, 0)

    acc_ref[...] = jnp.zeros_like(acc_ref)

    def step(j, nxt, carry, diagonal):
        rows = pl.ds(pl.multiple_of(j * FOX_TK, FOX_TK), FOX_TK)
        out = []
        for h in hs:
            s = s_ref[h]
            if diagonal:
                krow = lax.broadcasted_iota(jnp.int32, (FOX_TK, FOX_TQ), 0)
                qcol = lax.broadcasted_iota(jnp.int32, (FOX_TK, FOX_TQ), 1)
                s = jnp.where(qcol >= krow, s, -jnp.inf)
            m_new = jnp.maximum(carry[h][0], jnp.max(s, axis=0, keepdims=True))
            alpha = jnp.exp2(carry[h][0] - m_new)
            p = jnp.exp2(s - m_new)
            l_new = alpha * carry[h][1] + jnp.sum(p, axis=0, keepdims=True)
            if nxt is not None:
                s_ref[h] = scores(h, nxt)
            pv = _dot(vt_ref[hrows(h), rows], p.astype(BF16))
            acc_ref[hrows(h), :] = alpha * acc_ref[hrows(h), :] + pv
            out.append((m_new, l_new))
        return tuple(out)

    init = tuple((jnp.full((1, FOX_TQ), -jnp.inf, F32), jnp.zeros((1, FOX_TQ), F32)) for _ in hs)
    carry = lax.fori_loop(0, qi, lambda j, c: step(j, j + 1, c, False), init)

    def finish(more_queries):
        if more_queries:
            prepare_queries(qnext_ref, cfqnext_ref[0])
        final = step(qi, 0 if more_queries else None, carry, True)
        sub = lax.broadcasted_iota(jnp.int32, (LANES, 1), 0)
        for hp in range(FOX_HEADS // 2):
            pair = slice(hp * LANES, (hp + 1) * LANES)
            denom = jnp.where(sub < FOX_HEAD_DIM, final[2 * hp][1], final[2 * hp + 1][1])
            o_ref[:, pair] = (acc_ref[pair, :] / denom).T.astype(BF16)

    last_block = pl.num_programs(1) - 1
    pl.when(qi < last_block)(lambda: finish(True))
    pl.when(qi == last_block)(lambda: finish(False))


def _fox(proj, cf, q_gain2, k_gain2, layer):
    nq = SEQ // FOX_TQ
    fb = COL_FOX // BRANCH_WIDTH
    sel, ones = _fox_selectors()
    return pl.pallas_call(
        _fox_kernel,
        grid=(BATCH, nq),
        in_specs=[
            pl.BlockSpec((FOX_TQ, BRANCH_WIDTH), lambda b, qi: (b * nq + qi, fb)),
            pl.BlockSpec((FOX_TQ, BRANCH_WIDTH), lambda b, qi: (b * nq + jnp.minimum(qi + 1, nq - 1), fb)),
            pl.BlockSpec((SEQ, BRANCH_WIDTH), lambda b, qi: (b, fb + 1)),
            pl.BlockSpec((SEQ, BRANCH_WIDTH), lambda b, qi: (b, fb + 2)),
            pl.BlockSpec((1, FOX_TQ, LANES), lambda b, qi: (b, qi, 0)),
            pl.BlockSpec((1, FOX_TQ, LANES), lambda b, qi: (b, jnp.minimum(qi + 1, nq - 1), 0)),
            pl.BlockSpec((1, SEQ, LANES), lambda b, qi: (b, 0, 0)),
            _of_layer(layer, (1, LANES)),
            _of_layer(layer, (1, LANES)),
            pl.BlockSpec(sel.shape, lambda b, qi: (0, 0, 0, 0)),
            pl.BlockSpec(ones.shape, lambda b, qi: (0, 0, 0)),
        ],
        out_specs=pl.BlockSpec((FOX_TQ, BRANCH_WIDTH), lambda b, qi: (b * nq + qi, 0)),
        out_shape=jax.ShapeDtypeStruct((TOKENS, BRANCH_WIDTH), BF16),
        scratch_shapes=[
            pltpu.VMEM((FOX_HEADS, SEQ, LANES), BF16),
            pltpu.VMEM((BRANCH_WIDTH, SEQ), BF16),
            pltpu.VMEM((FOX_HEADS, FOX_TQ, LANES), BF16),
            pltpu.VMEM((BRANCH_WIDTH, FOX_TQ), F32),
            pltpu.VMEM((FOX_HEADS, FOX_TK, FOX_TQ), F32),
        ],
        compiler_params=_cparams(("parallel", "arbitrary")),
        name="fox",
    )(proj, proj, proj, proj, cf, cf, cf, q_gain2, k_gain2, sel, ones)


DN_STREAMS = 4

def _dn_kernel(q_ref, k_ref, v_ref, z_ref, sm_ref, wq_ref, wk_ref, wv_ref, alog_ref, dtb_ref,
               ng_ref, o_ref, u_ref, kcum_ref, qdec_ref, kdec_ref, qk_ref, gt_ref, st_ref):
    c = DN_CHUNK
    ri = lax.broadcasted_iota(jnp.int32, (c, c), 0)
    ci = lax.broadcasted_iota(jnp.int32, (c, c), 1)
    incl = ri >= ci
    strict = ri > ci
    eye = ri == ci
    eye_f = eye.astype(F32)
    tri3 = jnp.concatenate([jnp.where(incl, 1.0, 0.0)] * 3, axis=1).astype(BF16)
    lane = lax.broadcasted_iota(jnp.int32, (1, LANES), 1)

    n_taps = 4
    dr = lax.broadcasted_iota(jnp.int32, (n_taps * c, 2 * c), 0)
    dc = lax.broadcasted_iota(jnp.int32, (n_taps * c, 2 * c), 1)
    delays = jnp.where(dc == c + dr % c - dr // c, 1.0, 0.0).astype(BF16)

    def conv_silu(taps, w, cols):
        y = w[n_taps - 1:n_taps] * taps[0:c, cols]
        for j in range(1, n_taps):
            y = y + w[n_taps - 1 - j:n_taps - j] * taps[j * c:(j + 1) * c, cols]
        return _silu(y)

    def l2norm(x):
        return x * lax.rsqrt(jnp.sum(x * x, axis=-1, keepdims=True) + EPS)

    def heads():
        return [slice(h * DN_HEAD_DIM, (h + 1) * DN_HEAD_DIM) for h in range(DN_HEADS)]

    def rows_of(chunk):
        if isinstance(chunk, int):
            return chunk * c, max(chunk - 1, 0) * c
        return pl.multiple_of(chunk * c, c), pl.multiple_of(jnp.maximum(chunk - 1, 0) * c, c)

    def prepare(chunks):
        units = []
        raw, gcum_all, beta_all = [], [], []
        for chunk in chunks:
            r0, p0 = rows_of(chunk)
            rows = pl.ds(r0, c)
            prow = pl.ds(p0, c)
            keep_prev = jnp.where(chunk > 0, 1.0, 0.0).astype(BF16)
            taps = [_dot(delays, jnp.concatenate([ref[prow, :] * keep_prev, ref[rows, :]], axis=0))
                    for ref in (q_ref, k_ref, v_ref)]
            sm = sm_ref[rows, :]
            bsig = _sigmoid(sm)
            g_parts = _bf16_split3(-jnp.exp(alog_ref[...]) * _softplus(sm + dtb_ref[...]))
            gsum = _dot(tri3, jnp.concatenate(g_parts, axis=0).astype(BF16))
            for h, cols in enumerate(heads()):
                units.append((chunk, rows, h, cols))
                raw.append([(taps[a], w_ref[:, cols], cols)
                            for a, w_ref in enumerate((wq_ref, wk_ref, wv_ref))])
                gcum_all.append(gsum)
                beta_all.append(bsig)
        hs = range(len(units))
        gc, beta, q, k, v, decay = [], [], [], [], [], []
        for u in hs:
            head = units[u][2]
            g1 = jnp.sum(jnp.where(lane == SM_A + head, gcum_all[u], 0.0), axis=-1, keepdims=True)
            b1 = jnp.sum(jnp.where(lane == SM_BETA + head, beta_all[u], 0.0), axis=-1, keepdims=True)
            gc.append(jnp.broadcast_to(g1, (c, DN_HEAD_DIM)))
            beta.append(jnp.broadcast_to(b1, (c, DN_HEAD_DIM)))
            q.append(l2norm(conv_silu(*raw[u][0])) * (DN_HEAD_DIM ** -0.5))
            k.append(l2norm(conv_silu(*raw[u][1])))
            v.append(conv_silu(*raw[u][2]))
            gc_sq = gc[u][:, :c]
            g_row = jnp.sum(jnp.where(eye, gc_sq, 0.0), axis=0, keepdims=True)
            decay.append(jnp.exp(jnp.where(incl, gc_sq - g_row, -jnp.inf)))
        kb = [k[h] * beta[h] for h in hs]
        k16 = [k[h].astype(BF16) for h in hs]
        kk = [_dot_nt(kb[h].astype(BF16), k16[h]) for h in hs]
        qk = [_dot_nt(q[h].astype(BF16), k16[h]) for h in hs]
        yield
        neg = [jnp.where(strict, -(kk[h] * decay[h]), 0.0) for h in hs]
        qsum = [eye_f + neg[h] for h in hs]
        neg16 = [neg[h].astype(BF16) for h in hs]
        power = [_dot(neg16[h], neg16[h]) for h in hs]
        yield
        for _ in range(4):
            both = [_dot(power[h].astype(BF16),
                         jnp.concatenate([qsum[h], power[h]], axis=1).astype(BF16)) for h in hs]
            yield
            qsum = [qsum[h] + both[h][:, :c] for h in hs]
            power = [both[h][:, c:] for h in hs]
        last = [_dot(power[h].astype(BF16), qsum[h].astype(BF16)) for h in hs]
        yield
        eg = [jnp.exp(gc[h]) for h in hs]
        rhs = [jnp.concatenate([v[h] * beta[h], kb[h] * eg[h]], axis=1) for h in hs]
        sol = [rhs[h] + _dot((qsum[h] + last[h] - eye_f).astype(BF16), rhs[h].astype(BF16)) for h in hs]
        yield
        for u, (chunk, rows, head, cols) in enumerate(units):
            g_last = gc[u][c - 1:c, :]
            u_ref[rows, cols] = sol[u][:, :DN_HEAD_DIM]
            kcum_ref[rows, cols] = sol[u][:, DN_HEAD_DIM:].astype(BF16)
            qdec_ref[rows, cols] = (q[u] * eg[u]).astype(BF16)
            kdec_ref[rows, cols] = (k[u] * jnp.exp(g_last - gc[u])).astype(BF16)
            qk_ref[rows, head * c:(head + 1) * c] = jnp.where(incl, qk[u] * decay[u], 0.0).astype(BF16)
            gt_ref[chunk, :, cols] = jnp.broadcast_to(jnp.exp(g_last), (SUBLANES, DN_HEAD_DIM))

    def scan(chunks):
        hs = range(DN_HEADS)
        loaded = []
        for chunk in chunks:
            rows = pl.ds(rows_of(chunk)[0], c)
            loaded.append((rows, [(u_ref[rows, cols], kcum_ref[rows, cols], qdec_ref[rows, cols],
                                   qk_ref[rows, h * c:(h + 1) * c], kdec_ref[rows, cols],
                                   gt_ref[chunk, 0:1, cols], z_ref[rows, cols].astype(F32))
                                  for h, cols in enumerate(heads())]))
        state = [st_ref[h] for h in hs]
        outs = []
        for rows, per_head in loaded:
            state16 = [state[h].astype(BF16) for h in hs]
            v16 = [(per_head[h][0] - _dot(per_head[h][1], state16[h])).astype(BF16) for h in hs]
            yield
            upd = [_dot_tn(per_head[h][4], v16[h]) for h in hs]
            out = [_dot(per_head[h][2], state16[h]) + _dot(per_head[h][3], v16[h]) for h in hs]
            yield
            state = [state[h] * per_head[h][5] + upd[h] for h in hs]
            outs.append((rows, [(_rmsnorm(out[h], ng_ref[...])
                                 * _silu(per_head[h][6])).astype(BF16) for h in hs]))
        for h in hs:
            st_ref[h] = state[h]
        for rows, out in outs:
            for h, cols in enumerate(heads()):
                o_ref[rows, cols] = out[h]

    def interleave(*stages):
        active = list(stages)
        while active:
            for gen in list(active):
                if next(gen, StopIteration) is StopIteration:
                    active.remove(gen)

    def pair(n):
        return [DN_STREAMS * n + t for t in range(DN_STREAMS)]

    n_iter = N_CHUNKS // DN_STREAMS
    st_ref[...] = jnp.zeros_like(st_ref)
    interleave(prepare(pair(0)))

    def body(n, carry):
        interleave(scan(pair(n - 1)), prepare(pair(n)))
        return carry

    lax.fori_loop(1, n_iter, body, 0)
    interleave(scan(pair(n_iter - 1)))


def _deltanet(proj, small, conv_w, alog_row, dtb_row, norm_gain, layer):
    qb = COL_DN // BRANCH_WIDTH
    zb = COL_DNZ // BRANCH_WIDTH

    def colspec(blk):
        return pl.BlockSpec((SEQ, BRANCH_WIDTH), lambda b: (b, blk))

    def wspec(blk):
        return _of_layer(layer, (4, BRANCH_WIDTH), lambda b: (0, blk))

    row = _of_layer(layer, (1, LANES))
    return pl.pallas_call(
        _dn_kernel,
        grid=(BATCH,),
        in_specs=[
            colspec(qb), colspec(qb + 1), colspec(qb + 2), colspec(zb),
            pl.BlockSpec((SEQ, LANES), lambda b: (b, 0)),
            wspec(0), wspec(1), wspec(2),
            row, row, row,
        ],
        out_specs=pl.BlockSpec((SEQ, BRANCH_WIDTH), lambda b: (b, 0)),
        out_shape=jax.ShapeDtypeStruct((TOKENS, BRANCH_WIDTH), BF16),
        scratch_shapes=[
            pltpu.VMEM((SEQ, BRANCH_WIDTH), F32),
            pltpu.VMEM((SEQ, BRANCH_WIDTH), BF16),
            pltpu.VMEM((SEQ, BRANCH_WIDTH), BF16),
            pltpu.VMEM((SEQ, BRANCH_WIDTH), BF16),
            pltpu.VMEM((SEQ, DN_HEADS * DN_CHUNK), BF16),
            pltpu.VMEM((N_CHUNKS, SUBLANES, BRANCH_WIDTH), F32),
            pltpu.VMEM((DN_HEADS, DN_HEAD_DIM, DN_HEAD_DIM), F32),
        ],
        compiler_params=_cparams(("parallel",)),
        name="deltanet",
    )(proj, proj, proj, proj, small, conv_w, conv_w, conv_w, alog_row, dtb_row, norm_gain)


def _merge_kernel(yf_ref, sb_ref, sc_ref, sv_ref, scp_ref, svp_ref, yd_ref, g0_ref, g1_ref, g2_ref,
                  x_ref, wb_ref, wo_ref, cw_ref, o_ref):
    i = pl.program_id(0)
    keep_prev = jnp.where(i % (SEQ // MERGE_TM) != 0, 1.0, 0.0)
    cur = sc_ref[...].astype(F32) * sv_ref[...].astype(F32)
    prev = scp_ref[...].astype(F32) * svp_ref[...].astype(F32) * keep_prev
    cat = jnp.concatenate([prev, cur], axis=0)
    cw = cw_ref[...]
    conv = cw[2:3] * cur + cw[1:2] * _shifted(cat, 1) + cw[0:1] * _shifted(cat, 2)
    y_sc = sb_ref[...].astype(F32) * conv
    merged = (_sigmoid(g0_ref[...].astype(F32)) * _dot(yf_ref[...], wb_ref[0])
              + _sigmoid(g1_ref[...].astype(F32)) * _dot(y_sc.astype(BF16), wb_ref[1])
              + _sigmoid(g2_ref[...].astype(F32)) * _dot(yd_ref[...], wb_ref[2]))
    o_ref[...] = x_ref[...] + _dot(merged.astype(BF16), wo_ref[...])


def _merge(x, proj, y_fox, y_dn, w_branch, w_o, sc_conv_w, layer):
    tm = MERGE_TM
    scb = COL_SC // BRANCH_WIDTH
    gb = COL_GATE // D_MODEL
    hpb = tm // HALO

    def prev_rows(i):
        return jnp.maximum(i * hpb - 1, 0)

    return pl.pallas_call(
        _merge_kernel,
        grid=(TOKENS // tm,),
        in_specs=[
            pl.BlockSpec((tm, BRANCH_WIDTH), lambda i: (i, 0)),
            pl.BlockSpec((tm, BRANCH_WIDTH), lambda i: (i, scb)),
            pl.BlockSpec((tm, BRANCH_WIDTH), lambda i: (i, scb + 1)),
            pl.BlockSpec((tm, BRANCH_WIDTH), lambda i: (i, scb + 2)),
            pl.BlockSpec((HALO, BRANCH_WIDTH), lambda i: (prev_rows(i), scb + 1)),
            pl.BlockSpec((HALO, BRANCH_WIDTH), lambda i: (prev_rows(i), scb + 2)),
            pl.BlockSpec((tm, BRANCH_WIDTH), lambda i: (i, 0)),
            pl.BlockSpec((tm, D_MODEL), lambda i: (i, gb)),
            pl.BlockSpec((tm, D_MODEL), lambda i: (i, gb + 1)),
            pl.BlockSpec((tm, D_MODEL), lambda i: (i, gb + 2)),
            pl.BlockSpec((tm, D_MODEL), lambda i: (i, 0)),
            _of_layer(layer, (3, BRANCH_WIDTH, D_MODEL), resident=True),
            _of_layer(layer, (D_MODEL, D_MODEL), resident=True),
            _of_layer(layer, (3, BRANCH_WIDTH)),
        ],
        out_specs=pl.BlockSpec((tm, D_MODEL), lambda i: (i, 0)),
        out_shape=jax.ShapeDtypeStruct((TOKENS, D_MODEL), F32),
        compiler_params=_cparams(("parallel",)),
        name="merge",
    )(y_fox, proj, proj, proj, proj, proj, y_dn, proj, proj, proj, x, w_branch, w_o, sc_conv_w)


def _ffn_kernel(x_ref, xp_ref, gf_ref, wup_ref, cw_ref, wd_ref, gp_ref, wpg_ref, p_ref, wple_ref, o_ref,
                act_ref):
    i = pl.program_id(0)
    keep_prev = jnp.where(i % (SEQ // FFN_TM) != 0, 1.0, 0.0)
    xx = jnp.concatenate([xp_ref[...] * keep_prev, x_ref[...]], axis=0)
    hn = _rmsnorm(xx, gf_ref[...]).astype(BF16)

    def branch(cols):
        u = _dot(hn, wup_ref[:, cols])
        cw = cw_ref[:, cols]
        return cw[2:3] * u[HALO:] + cw[1:2] * _shifted(u, 1) + cw[0:1] * _shifted(u, 2)

    for j in range(FFN_NF):
        gate = branch(slice(j * FFN_TF, (j + 1) * FFN_TF))
        val = branch(slice(D_FF + j * FFN_TF, D_FF + (j + 1) * FFN_TF))
        act_ref[:, j * FFN_TF:(j + 1) * FFN_TF] = (_silu(gate) * val).astype(BF16)

    x1 = x_ref[...] + _dot(act_ref[...], wd_ref[...])
    h2 = _rmsnorm(x1, gp_ref[...]).astype(BF16)
    pgate = _sigmoid(_dot(h2, wpg_ref[...]))
    emb = _dot(p_ref[...].astype(BF16), wple_ref[...])
    o_ref[...] = x1 + pgate * emb


def _ffn(x, p, g_ffn, w_up, conv_w, w_down, g_ple, w_pg, w_ple, layer):
    tm = FFN_TM
    hpb = tm // HALO

    def whole(shape):
        return _of_layer(layer, shape, resident=True)

    return pl.pallas_call(
        _ffn_kernel,
        grid=(TOKENS // tm,),
        in_specs=[
            pl.BlockSpec((tm, D_MODEL), lambda i: (i, 0)),
            pl.BlockSpec((HALO, D_MODEL), lambda i: (jnp.maximum(i * hpb - 1, 0), 0)),
            whole((1, D_MODEL)),
            whole((D_MODEL, 2 * D_FF)),
            whole((3, 2 * D_FF)),
            whole((D_FF, D_MODEL)),
            whole((1, D_MODEL)),
            whole((D_MODEL, D_MODEL)),
            _of_layer(layer, (tm, PLE_DIM), lambda i: (i, 0)),
            whole((PLE_DIM, D_MODEL)),
        ],
        out_specs=pl.BlockSpec((tm, D_MODEL), lambda i: (i, 0)),
        out_shape=jax.ShapeDtypeStruct((TOKENS, D_MODEL), F32),
        scratch_shapes=[pltpu.VMEM((tm, D_FF), BF16)],
        compiler_params=_cparams(("parallel",)),
        name="ffn",
    )(x, x, g_ffn, w_up, conv_w, w_down, g_ple, w_pg, p, w_ple)


def _lane_rows(values, offset):
    depth, n = values.shape
    return jnp.pad(values.astype(F32), ((0, 0), (offset, LANES - offset - n)))[:, None, :]


def kernel(x, p, g_mix, w_in, b_fox_f, fox_q_gain, fox_k_gain, sc_conv_w, dn_conv_w, dn_a_log,
           dn_dt_bias, dn_norm_gain, w_branch, w_o, g_ffn, w_up, ffn_conv_w, w_down, g_ple,
           w_ple_gate, w_ple):
    depth = p.shape[0]
    x = x.reshape(TOKENS, D_MODEL)
    p = p.reshape(depth, TOKENS, PLE_DIM)

    w_big, w_small = _regroup(w_in)
    w_branch16, w_o16, w_up16, w_down16 = (w.astype(BF16) for w in (w_branch, w_o, w_up, w_down))
    w_pg16, w_ple16 = w_ple_gate.astype(BF16), w_ple.astype(BF16)
    g_mix3, g_ffn3, g_ple3, dn_gain3 = (g[:, None, :] for g in (g_mix, g_ffn, g_ple, dn_norm_gain))
    q_gain3 = jnp.tile(fox_q_gain, (1, 2))[:, None, :]
    k_gain3 = jnp.tile(fox_k_gain, (1, 2))[:, None, :]
    fox_bias = _lane_rows(b_fox_f, SM_FOXF)
    a_log = _lane_rows(dn_a_log, SM_A)
    dt_bias = _lane_rows(dn_dt_bias, SM_A)

    for layer in range(depth):
        proj, small = _proj(x, g_mix3, w_big, w_small, layer)
        cf = _cumf(small, fox_bias, layer)
        y_fox = _fox(proj, cf, q_gain3, k_gain3, layer)
        y_dn = _deltanet(proj, small, dn_conv_w, a_log, dt_bias, dn_gain3, layer)
        x = _merge(x, proj, y_fox, y_dn, w_branch16, w_o16, sc_conv_w, layer)
        x = _ffn(x, p, g_ffn3, w_up16, ffn_conv_w, w_down16, g_ple3, w_pg16, w_ple16, layer)
    return x.reshape(BATCH, SEQ, D_MODEL)
```

```python
import jax
import jax.numpy as jnp
import numpy as np
from jax import lax
from jax.experimental import pallas as pl
from jax.experimental.pallas import tpu as pltpu

F32 = jnp.float32
BF16 = jnp.bfloat16
HI = lax.Precision.HIGHEST

D_MODEL = 1024
BATCH = 8
SEQ = 2048
TOKENS = BATCH * SEQ
PLE_DIM = 256
BRANCH_WIDTH = 512
FOX_HEADS = 8
FOX_HEAD_DIM = 64
DN_HEADS = 4
DN_HEAD_DIM = 128
DN_CHUNK = 64
N_CHUNKS = SEQ // DN_CHUNK
D_FF = 2816
EPS = 1e-6

LANES = 128
SUBLANES = 8
HALO = 16

PROJ_WIDTH = 8192
COL_FOX = 0
COL_SC = 1536
COL_DN = 3072
COL_DNZ = 4608
COL_GATE = 5120
SM_FOXF = 0
SM_BETA = 8
SM_A = 12

PROJ_TM = 1024
PROJ_TN = 2048
FOX_TQ = 512
FOX_TK = 512
MERGE_TM = 512
MERGE_SPLIT = 2
FFN_TM = 512
FFN_TF = 256
FFN_NF = D_FF // FFN_TF

VMEM_LIMIT = 48 * 1024 * 1024


def _cparams(sem):
    return pltpu.CompilerParams(dimension_semantics=sem, vmem_limit_bytes=VMEM_LIMIT)


def _sigmoid(x):
    return 1.0 / (1.0 + jnp.exp(-x))


def _silu(x):
    half = 0.5 * x
    return half + half * jnp.tanh(half)


def _softplus(x):
    return jnp.maximum(x, 0.0) + jnp.log(1.0 + jnp.exp(-jnp.abs(x)))


def _dot(a, b, precision=None):
    return jnp.dot(a, b, preferred_element_type=F32, precision=precision)


def _dot_nt(a, b):
    return lax.dot_general(a, b, (((1,), (1,)), ((), ())), preferred_element_type=F32)


def _dot_tn(a, b):
    return lax.dot_general(a, b, (((0,), (0,)), ((), ())), preferred_element_type=F32)


def _rmsnorm(x, gain):
    ms = jnp.mean(x * x, axis=-1, keepdims=True)
    return x * lax.rsqrt(ms + EPS) * gain


def _shifted(cat, shift):
    if shift == 0:
        return cat[HALO:]
    return pltpu.roll(cat, shift, 0)[HALO:]


def _bf16_split3(x):
    hi = x.astype(BF16).astype(F32)
    r = x - hi
    mid = r.astype(BF16).astype(F32)
    lo = (r - mid).astype(BF16).astype(F32)
    return hi, mid, lo


IN_F = 3 * BRANCH_WIDTH
IN_SC = IN_F + FOX_HEADS
IN_DN = IN_SC + 3 * BRANCH_WIDTH
IN_B = IN_DN + 3 * BRANCH_WIDTH
IN_Z = IN_B + 2 * DN_HEADS
IN_G = IN_Z + BRANCH_WIDTH
IN_WIDTH = IN_G + 3 * D_MODEL
REGROUP_ROWS = 256
assert IN_F % LANES == SM_FOXF and SM_FOXF + FOX_HEADS == SM_BETA
assert IN_B % LANES == SM_BETA and SM_BETA + DN_HEADS == SM_A


def _regroup_kernel(w_ref, big_ref, small_ref):
    col = 0
    for a, b in ((0, IN_F), (IN_SC, IN_DN), (IN_DN, IN_B), (IN_Z, IN_G), (IN_G, IN_WIDTH)):
        big_ref[:, col:col + (b - a)] = w_ref[:, a:b].astype(BF16)
        col += b - a
    lane = lax.broadcasted_iota(jnp.int32, (1, LANES), 1)
    f_tile = w_ref[:, IN_F - SM_FOXF:IN_F - SM_FOXF + LANES]
    b_tile = w_ref[:, IN_B - SM_BETA:IN_B - SM_BETA + LANES]
    small_ref[...] = jnp.where(lane < SM_BETA, f_tile,
                               jnp.where(lane < SM_A + DN_HEADS, b_tile, 0.0)).astype(BF16)


def _regroup(w_in):
    depth = w_in.shape[0]
    return pl.pallas_call(
        _regroup_kernel,
        grid=(depth, D_MODEL // REGROUP_ROWS),
        in_specs=[pl.BlockSpec((None, REGROUP_ROWS, IN_WIDTH), lambda l, i: (l, i, 0))],
        out_specs=[
            pl.BlockSpec((None, REGROUP_ROWS, PROJ_WIDTH), lambda l, i: (l, i, 0)),
            pl.BlockSpec((None, REGROUP_ROWS, LANES), lambda l, i: (l, i, 0)),
        ],
        out_shape=[
            jax.ShapeDtypeStruct((depth, D_MODEL, PROJ_WIDTH), BF16),
            jax.ShapeDtypeStruct((depth, D_MODEL, LANES), BF16),
        ],
        compiler_params=_cparams(("parallel", "parallel")),
        name="regroup",
    )(w_in)


def _proj_kernel(x_ref, g_ref, w_ref, ws_ref, o_ref, os_ref, hn_ref):
    @pl.when(pl.program_id(1) == 0)
    def _():
        hn = _rmsnorm(x_ref[...], g_ref[...]).astype(BF16)
        hn_ref[...] = hn
        os_ref[...] = _dot(hn, ws_ref[...])

    o_ref[...] = _dot(hn_ref[...], w_ref[...]).astype(BF16)


def _of_layer(layer, block, index_map=None, resident=False):
    if index_map is None:
        index_map = lambda *grid: (0,) * len(block)
    mode = dict(pipeline_mode=pl.Buffered(1)) if resident else {}
    return pl.BlockSpec((None,) + tuple(block), lambda *grid: (layer,) + tuple(index_map(*grid)), **mode)


def _proj(x, gain, w_big, w_small, layer):
    grid = (TOKENS // PROJ_TM, PROJ_WIDTH // PROJ_TN)
    return pl.pallas_call(
        _proj_kernel,
        grid=grid,
        in_specs=[
            pl.BlockSpec((PROJ_TM, D_MODEL), lambda i, j: (i, 0)),
            _of_layer(layer, (1, D_MODEL)),
            _of_layer(layer, (D_MODEL, PROJ_TN), lambda i, j: (0, j)),
            _of_layer(layer, (D_MODEL, LANES)),
        ],
        out_specs=[
            pl.BlockSpec((PROJ_TM, PROJ_TN), lambda i, j: (i, j)),
            pl.BlockSpec((PROJ_TM, LANES), lambda i, j: (i, 0)),
        ],
        out_shape=[
            jax.ShapeDtypeStruct((TOKENS, PROJ_WIDTH), BF16),
            jax.ShapeDtypeStruct((TOKENS, LANES), F32),
        ],
        scratch_shapes=[pltpu.VMEM((PROJ_TM, D_MODEL), BF16)],
        compiler_params=_cparams(("parallel", "arbitrary")),
        name="proj",
    )(x, gain, w_big, w_small)


CUMF_BLK = 128


def _cumf_kernel(sm_ref, bf_ref, cf_ref):
    r = lax.broadcasted_iota(jnp.int32, (CUMF_BLK, CUMF_BLK), 0)
    c = lax.broadcasted_iota(jnp.int32, (CUMF_BLK, CUMF_BLK), 1)
    tri = (r >= c).astype(F32)
    carry = jnp.zeros((1, LANES), F32)
    for blk in range(SEQ // CUMF_BLK):
        rows = slice(blk * CUMF_BLK, (blk + 1) * CUMF_BLK)
        log_f = -_softplus(-(sm_ref[rows, :] + bf_ref[...]))
        cum = _dot(tri, log_f, HI) + carry
        carry = cum[CUMF_BLK - 1:CUMF_BLK, :]
        cf_ref[0, rows, :] = cum


def _cumf(small, bias_row, layer):
    return pl.pallas_call(
        _cumf_kernel,
        grid=(BATCH,),
        in_specs=[
            pl.BlockSpec((SEQ, LANES), lambda b: (b, 0)),
            _of_layer(layer, (1, LANES)),
        ],
        out_specs=pl.BlockSpec((1, SEQ, LANES), lambda b: (b, 0, 0)),
        out_shape=jax.ShapeDtypeStruct((BATCH, SEQ, LANES), F32),
        compiler_params=_cparams(("parallel",)),
        name="cumf",
    )(small, bias_row)


AUG = FOX_HEAD_DIM
LOG2E = 1.4426950408889634


def _fox_selectors():
    sel = np.zeros((2, FOX_HEADS // 2, 4 * LANES, 2 * LANES), np.float32)
    ones = np.zeros((2, 1, 2 * LANES), np.float32)
    for hp in range(FOX_HEADS // 2):
        for e in range(2):
            base = e * LANES
            for d in range(FOX_HEAD_DIM):
                sel[:, hp, e * FOX_HEAD_DIM + d, base + d] = 1.0
            for part in range(3):
                src = (1 + part) * LANES + SM_FOXF + 2 * hp + e
                sel[0, hp, src, base + AUG + part] = 1.0
                sel[1, hp, src, base + AUG + 3 + part] = -1.0
    for e in range(2):
        ones[0, 0, e * LANES + AUG + 3:e * LANES + AUG + 6] = 1.0
        ones[1, 0, e * LANES + AUG:e * LANES + AUG + 3] = 1.0
    return jnp.asarray(sel, BF16), jnp.asarray(ones, F32)


def _fox_kernel(q_ref, qnext_ref, k_ref, v_ref, cfq_ref, cfqnext_ref, cfk_ref, qg_ref, kg_ref, sel_ref,
                ones_ref, o_ref, ka_ref, vt_ref, qa_ref, acc_ref, s_ref):
    qi = pl.program_id(1)
    lane = lax.broadcasted_iota(jnp.int32, (1, LANES), 1)
    first_head = lane < FOX_HEAD_DIM

    def headnorm(x, gain):
        x2 = x * x
        s0 = jnp.sum(jnp.where(first_head, x2, 0.0), axis=-1, keepdims=True)
        s1 = jnp.sum(jnp.where(first_head, 0.0, x2), axis=-1, keepdims=True)
        inv = jnp.where(first_head,
                        lax.rsqrt(s0 * (1.0 / FOX_HEAD_DIM) + EPS),
                        lax.rsqrt(s1 * (1.0 / FOX_HEAD_DIM) + EPS))
        return x * inv * gain

    def cum_parts(cf):
        return [part.astype(BF16) for part in _bf16_split3(cf * LOG2E)]

    def augmented(xn, parts, hp, side):
        packed = jnp.concatenate([xn.astype(BF16)] + parts, axis=1)
        return (_dot(packed, sel_ref[side, hp]) + ones_ref[side]).astype(BF16)

    hs = range(FOX_HEADS)

    def hrows(h):
        return slice(h * FOX_HEAD_DIM, (h + 1) * FOX_HEAD_DIM)

    def scores(h, j):
        rows = pl.ds(pl.multiple_of(j * FOX_TK, FOX_TK), FOX_TK)
        return _dot_nt(ka_ref[h, rows, :], qa_ref[h])

    def prepare_queries(src_ref, cf):
        parts = cum_parts(cf)
        for hp in range(FOX_HEADS // 2):
            pair = slice(hp * LANES, (hp + 1) * LANES)
            qn = headnorm(src_ref[:, pair].astype(F32), qg_ref[...]) * (FOX_HEAD_DIM ** -0.5 * LOG2E)
            qa = augmented(qn, parts, hp, 0)
            qa_ref[2 * hp] = qa[:, :LANES]
            qa_ref[2 * hp + 1] = qa[:, LANES:]

    @pl.when(qi == 0)
    def _():
        def body(c, carry):
            off = pl.multiple_of(c * FOX_TK, FOX_TK)
            rows = pl.ds(off, FOX_TK)
            parts = cum_parts(cfk_ref[0, rows, :])
            for hp in range(FOX_HEADS // 2):
                pair = slice(hp * LANES, (hp + 1) * LANES)
                kn = headnorm(k_ref[rows, pair].astype(F32), kg_ref[...])
                ka = augmented(kn, parts, hp, 1)
                ka_ref[2 * hp, rows, :] = ka[:, :LANES]
                ka_ref[2 * hp + 1, rows, :] = ka[:, LANES:]
                vt_ref[pair, rows] = v_ref[rows, pair].astype(F32).T.astype(BF16)
            return carry
        lax.fori_loop(0, SEQ // FOX_TK, body, 0)
        prepare_queries(q_ref, cfq_ref[0])
        for h in hs:
            s_ref[h] = scores(h, 0)

    acc_ref[...] = jnp.zeros_like(acc_ref)

    def absorb(s, m, l, h, krows, qcols):
        m_new = jnp.maximum(m, jnp.max(s, axis=0, keepdims=True))
        alpha = jnp.exp2(m - m_new)
        p = jnp.exp2(s - m_new)
        l_new = alpha * l + jnp.sum(p, axis=0, keepdims=True)
        pv = _dot(vt_ref[hrows(h), krows], p.astype(BF16))
        acc_ref[hrows(h), qcols] = alpha * acc_ref[hrows(h), qcols] + pv
        return m_new, l_new

    half = FOX_TQ // 2
    left, right = slice(0, half), slice(half, FOX_TQ)

    def step(j, nxt, carry, diagonal):
        k0 = pl.multiple_of(j * FOX_TK, FOX_TK)
        out = []
        for h in hs:
            (m_l, l_l), (m_r, l_r) = carry[h]
            if not diagonal:
                s_l, s_r = s_ref[h, :, left], s_ref[h, :, right]
                if nxt is not None:
                    s_ref[h] = scores(h, nxt)
                keys = pl.ds(k0, FOX_TK)
                out.append((absorb(s_l, m_l, l_l, h, keys, left), absorb(s_r, m_r, l_r, h, keys, right)))
                continue
            krow = lax.broadcasted_iota(jnp.int32, (half, half), 0)
            qcol = lax.broadcasted_iota(jnp.int32, (half, half), 1)
            causal = qcol >= krow
            early, late = pl.ds(k0, half), pl.ds(k0 + half, half)
            s_ll = jnp.where(causal, s_ref[h, left, left], -jnp.inf)
            s_lr = s_ref[h, left, right]
            s_rr = jnp.where(causal, s_ref[h, right, right], -jnp.inf)
            if nxt is not None:
                s_ref[h] = scores(h, nxt)
            out.append((absorb(s_ll, m_l, l_l, h, early, left),
                        absorb(s_rr, *absorb(s_lr, m_r, l_r, h, early, right), h, late, right)))
        return tuple(out)

    def fresh():
        return jnp.full((1, half), -jnp.inf, F32), jnp.zeros((1, half), F32)

    init = tuple((fresh(), fresh()) for _ in hs)
    carry = lax.fori_loop(0, qi, lambda j, c: step(j, j + 1, c, False), init)

    def finish(more_queries):
        if more_queries:
            prepare_queries(qnext_ref, cfqnext_ref[0])
        final = step(qi, 0 if more_queries else None, carry, True)
        sub = lax.broadcasted_iota(jnp.int32, (LANES, 1), 0)
        for hp in range(FOX_HEADS // 2):
            pair = slice(hp * LANES, (hp + 1) * LANES)
            for side, qcols in enumerate((left, right)):
                denom = jnp.where(sub < FOX_HEAD_DIM, final[2 * hp][side][1], final[2 * hp + 1][side][1])
                o_ref[qcols, pair] = (acc_ref[pair, qcols] / denom).T.astype(BF16)

    last_block = pl.num_programs(1) - 1
    pl.when(qi < last_block)(lambda: finish(True))
    pl.when(qi == last_block)(lambda: finish(False))


def _fox(proj, cf, q_gain2, k_gain2, layer):
    nq = SEQ // FOX_TQ
    fb = COL_FOX // BRANCH_WIDTH
    sel, ones = _fox_selectors()
    return pl.pallas_call(
        _fox_kernel,
        grid=(BATCH, nq),
        in_specs=[
            pl.BlockSpec((FOX_TQ, BRANCH_WIDTH), lambda b, qi: (b * nq + qi, fb)),
            pl.BlockSpec((FOX_TQ, BRANCH_WIDTH), lambda b, qi: (b * nq + jnp.minimum(qi + 1, nq - 1), fb)),
            pl.BlockSpec((SEQ, BRANCH_WIDTH), lambda b, qi: (b, fb + 1)),
            pl.BlockSpec((SEQ, BRANCH_WIDTH), lambda b, qi: (b, fb + 2)),
            pl.BlockSpec((1, FOX_TQ, LANES), lambda b, qi: (b, qi, 0)),
            pl.BlockSpec((1, FOX_TQ, LANES), lambda b, qi: (b, jnp.minimum(qi + 1, nq - 1), 0)),
            pl.BlockSpec((1, SEQ, LANES), lambda b, qi: (b, 0, 0)),
            _of_layer(layer, (1, LANES)),
            _of_layer(layer, (1, LANES)),
            pl.BlockSpec(sel.shape, lambda b, qi: (0, 0, 0, 0)),
            pl.BlockSpec(ones.shape, lambda b, qi: (0, 0, 0)),
        ],
        out_specs=pl.BlockSpec((FOX_TQ, BRANCH_WIDTH), lambda b, qi: (b * nq + qi, 0)),
        out_shape=jax.ShapeDtypeStruct((TOKENS, BRANCH_WIDTH), BF16),
        scratch_shapes=[
            pltpu.VMEM((FOX_HEADS, SEQ, LANES), BF16),
            pltpu.VMEM((BRANCH_WIDTH, SEQ), BF16),
            pltpu.VMEM((FOX_HEADS, FOX_TQ, LANES), BF16),
            pltpu.VMEM((BRANCH_WIDTH, FOX_TQ), F32),
            pltpu.VMEM((FOX_HEADS, FOX_TK, FOX_TQ), F32),
        ],
        compiler_params=_cparams(("parallel", "arbitrary")),
        name="fox",
    )(proj, proj, proj, proj, cf, cf, cf, q_gain2, k_gain2, sel, ones)


DN_STREAMS = 4

def _dn_kernel(q_ref, k_ref, v_ref, z_ref, sm_ref, wq_ref, wk_ref, wv_ref, alog_ref, dtb_ref,
               ng_ref, o_ref, u_ref, kcum_ref, qdec_ref, kdec_ref, qk_ref, gt_ref, st_ref):
    c = DN_CHUNK
    ri = lax.broadcasted_iota(jnp.int32, (c, c), 0)
    ci = lax.broadcasted_iota(jnp.int32, (c, c), 1)
    incl = ri >= ci
    strict = ri > ci
    eye = ri == ci
    eye_f = eye.astype(F32)
    tri3 = jnp.concatenate([jnp.where(incl, 1.0, 0.0)] * 3, axis=1).astype(BF16)
    lane = lax.broadcasted_iota(jnp.int32, (1, LANES), 1)

    n_taps = 4
    dr = lax.broadcasted_iota(jnp.int32, (n_taps * c, 2 * c), 0)
    dc = lax.broadcasted_iota(jnp.int32, (n_taps * c, 2 * c), 1)
    delays = jnp.where(dc == c + dr % c - dr // c, 1.0, 0.0).astype(BF16)

    def conv_silu(taps, w, cols):
        y = w[n_taps - 1:n_taps] * taps[0:c, cols]
        for j in range(1, n_taps):
            y = y + w[n_taps - 1 - j:n_taps - j] * taps[j * c:(j + 1) * c, cols]
        return _silu(y)

    def l2norm(x):
        return x * lax.rsqrt(jnp.sum(x * x, axis=-1, keepdims=True) + EPS)

    def heads():
        return [slice(h * DN_HEAD_DIM, (h + 1) * DN_HEAD_DIM) for h in range(DN_HEADS)]

    def rows_of(chunk):
        if isinstance(chunk, int):
            return chunk * c, max(chunk - 1, 0) * c
        return pl.multiple_of(chunk * c, c), pl.multiple_of(jnp.maximum(chunk - 1, 0) * c, c)

    def prepare(chunks):
        units = []
        raw, gcum_all, beta_all = [], [], []
        for chunk in chunks:
            r0, p0 = rows_of(chunk)
            rows = pl.ds(r0, c)
            prow = pl.ds(p0, c)
            keep_prev = jnp.where(chunk > 0, 1.0, 0.0).astype(BF16)
            taps = [_dot(delays, jnp.concatenate([ref[prow, :] * keep_prev, ref[rows, :]], axis=0))
                    for ref in (q_ref, k_ref, v_ref)]
            sm = sm_ref[rows, :]
            bsig = _sigmoid(sm)
            g_parts = _bf16_split3(-jnp.exp(alog_ref[...]) * _softplus(sm + dtb_ref[...]))
            gsum = _dot(tri3, jnp.concatenate(g_parts, axis=0).astype(BF16))
            for h, cols in enumerate(heads()):
                units.append((chunk, rows, h, cols))
                raw.append([(taps[a], w_ref[:, cols], cols)
                            for a, w_ref in enumerate((wq_ref, wk_ref, wv_ref))])
                gcum_all.append(gsum)
                beta_all.append(bsig)
        hs = range(len(units))
        gc, beta, q, k, v, decay = [], [], [], [], [], []
        for u in hs:
            head = units[u][2]
            g1 = jnp.sum(jnp.where(lane == SM_A + head, gcum_all[u], 0.0), axis=-1, keepdims=True)
            b1 = jnp.sum(jnp.where(lane == SM_BETA + head, beta_all[u], 0.0), axis=-1, keepdims=True)
            gc.append(jnp.broadcast_to(g1, (c, DN_HEAD_DIM)))
            beta.append(jnp.broadcast_to(b1, (c, DN_HEAD_DIM)))
            q.append(l2norm(conv_silu(*raw[u][0])) * (DN_HEAD_DIM ** -0.5))
            k.append(l2norm(conv_silu(*raw[u][1])))
            v.append(conv_silu(*raw[u][2]))
            gc_sq = gc[u][:, :c]
            g_row = jnp.sum(jnp.where(eye, gc_sq, 0.0), axis=0, keepdims=True)
            decay.append(jnp.exp(jnp.where(incl, gc_sq - g_row, -jnp.inf)))
        kb = [k[h] * beta[h] for h in hs]
        k16 = [k[h].astype(BF16) for h in hs]
        kk = [_dot_nt(kb[h].astype(BF16), k16[h]) for h in hs]
        qk = [_dot_nt(q[h].astype(BF16), k16[h]) for h in hs]
        yield
        neg = [jnp.where(strict, -(kk[h] * decay[h]), 0.0) for h in hs]
        qsum = [eye_f + neg[h] for h in hs]
        neg16 = [neg[h].astype(BF16) for h in hs]
        power = [_dot(neg16[h], neg16[h]) for h in hs]
        yield
        for _ in range(4):
            both = [_dot(power[h].astype(BF16),
                         jnp.concatenate([qsum[h], power[h]], axis=1).astype(BF16)) for h in hs]
            yield
            qsum = [qsum[h] + both[h][:, :c] for h in hs]
            power = [both[h][:, c:] for h in hs]
        last = [_dot(power[h].astype(BF16), qsum[h].astype(BF16)) for h in hs]
        yield
        eg = [jnp.exp(gc[h]) for h in hs]
        rhs = [jnp.concatenate([v[h] * beta[h], kb[h] * eg[h]], axis=1) for h in hs]
        sol = [rhs[h] + _dot((qsum[h] + last[h] - eye_f).astype(BF16), rhs[h].astype(BF16)) for h in hs]
        yield
        for u, (chunk, rows, head, cols) in enumerate(units):
            g_last = gc[u][c - 1:c, :]
            u_ref[rows, cols] = sol[u][:, :DN_HEAD_DIM]
            kcum_ref[rows, cols] = sol[u][:, DN_HEAD_DIM:].astype(BF16)
            qdec_ref[rows, cols] = (q[u] * eg[u]).astype(BF16)
            kdec_ref[rows, cols] = (k[u] * jnp.exp(g_last - gc[u])).astype(BF16)
            qk_ref[rows, head * c:(head + 1) * c] = jnp.where(incl, qk[u] * decay[u], 0.0).astype(BF16)
            gt_ref[chunk, :, cols] = jnp.broadcast_to(jnp.exp(g_last), (SUBLANES, DN_HEAD_DIM))

    def scan(chunks):
        hs = range(DN_HEADS)
        loaded = []
        for chunk in chunks:
            rows = pl.ds(rows_of(chunk)[0], c)
            loaded.append((rows, [(u_ref[rows, cols], kcum_ref[rows, cols], qdec_ref[rows, cols],
                                   qk_ref[rows, h * c:(h + 1) * c], kdec_ref[rows, cols],
                                   gt_ref[chunk, 0:1, cols], z_ref[rows, cols].astype(F32))
                                  for h, cols in enumerate(heads())]))
        state = [st_ref[h] for h in hs]
        outs = []
        for rows, per_head in loaded:
            state16 = [state[h].astype(BF16) for h in hs]
            v16 = [(per_head[h][0] - _dot(per_head[h][1], state16[h])).astype(BF16) for h in hs]
            yield
            upd = [_dot_tn(per_head[h][4], v16[h]) for h in hs]
            out = [_dot(per_head[h][2], state16[h]) + _dot(per_head[h][3], v16[h]) for h in hs]
            yield
            state = [state[h] * per_head[h][5] + upd[h] for h in hs]
            outs.append((rows, [(_rmsnorm(out[h], ng_ref[...])
                                 * _silu(per_head[h][6])).astype(BF16) for h in hs]))
        for h in hs:
            st_ref[h] = state[h]
        for rows, out in outs:
            for h, cols in enumerate(heads()):
                o_ref[rows, cols] = out[h]

    def interleave(*stages):
        active = list(stages)
        while active:
            for gen in list(active):
                if next(gen, StopIteration) is StopIteration:
                    active.remove(gen)

    def pair(n):
        return [DN_STREAMS * n + t for t in range(DN_STREAMS)]

    n_iter = N_CHUNKS // DN_STREAMS
    st_ref[...] = jnp.zeros_like(st_ref)
    interleave(prepare(pair(0)))

    def body(n, carry):
        interleave(scan(pair(n - 1)), prepare(pair(n)))
        return carry

    lax.fori_loop(1, n_iter, body, 0)
    interleave(scan(pair(n_iter - 1)))


def _deltanet(proj, small, conv_w, alog_row, dtb_row, norm_gain, layer):
    qb = COL_DN // BRANCH_WIDTH
    zb = COL_DNZ // BRANCH_WIDTH

    def colspec(blk):
        return pl.BlockSpec((SEQ, BRANCH_WIDTH), lambda b: (b, blk))

    def wspec(blk):
        return _of_layer(layer, (4, BRANCH_WIDTH), lambda b: (0, blk))

    row = _of_layer(layer, (1, LANES))
    return pl.pallas_call(
        _dn_kernel,
        grid=(BATCH,),
        in_specs=[
            colspec(qb), colspec(qb + 1), colspec(qb + 2), colspec(zb),
            pl.BlockSpec((SEQ, LANES), lambda b: (b, 0)),
            wspec(0), wspec(1), wspec(2),
            row, row, row,
        ],
        out_specs=pl.BlockSpec((SEQ, BRANCH_WIDTH), lambda b: (b, 0)),
        out_shape=jax.ShapeDtypeStruct((TOKENS, BRANCH_WIDTH), BF16),
        scratch_shapes=[
            pltpu.VMEM((SEQ, BRANCH_WIDTH), F32),
            pltpu.VMEM((SEQ, BRANCH_WIDTH), BF16),
            pltpu.VMEM((SEQ, BRANCH_WIDTH), BF16),
            pltpu.VMEM((SEQ, BRANCH_WIDTH), BF16),
            pltpu.VMEM((SEQ, DN_HEADS * DN_CHUNK), BF16),
            pltpu.VMEM((N_CHUNKS, SUBLANES, BRANCH_WIDTH), F32),
            pltpu.VMEM((DN_HEADS, DN_HEAD_DIM, DN_HEAD_DIM), F32),
        ],
        compiler_params=_cparams(("parallel",)),
        name="deltanet",
    )(proj, proj, proj, proj, small, conv_w, conv_w, conv_w, alog_row, dtb_row, norm_gain)


def _merge_kernel(yf_ref, sb_ref, sc_ref, sv_ref, scp_ref, svp_ref, yd_ref, g0_ref, g1_ref, g2_ref,
                  x_ref, wb_ref, wo_ref, cw_ref, o_ref):
    i = pl.program_id(0)
    keep_prev = jnp.where(i % (SEQ // MERGE_TM) != 0, 1.0, 0.0)
    cur = sc_ref[...].astype(F32) * sv_ref[...].astype(F32)
    prev = scp_ref[...].astype(F32) * svp_ref[...].astype(F32) * keep_prev
    cat = jnp.concatenate([prev, cur], axis=0)
    cw = cw_ref[...]
    conv = cw[2:3] * cur + cw[1:2] * _shifted(cat, 1) + cw[0:1] * _shifted(cat, 2)
    y_sc = (sb_ref[...].astype(F32) * conv).astype(BF16)
    sub = MERGE_TM // MERGE_SPLIT
    blocks = [slice(r * sub, (r + 1) * sub) for r in range(MERGE_SPLIT)]
    branch = [(_dot(yf_ref[rows, :], wb_ref[0]), _dot(y_sc[rows], wb_ref[1]), _dot(yd_ref[rows, :], wb_ref[2]))
              for rows in blocks]
    for rows, (d0, d1, d2) in zip(blocks, branch):
        merged = (_sigmoid(g0_ref[rows, :].astype(F32)) * d0 + _sigmoid(g1_ref[rows, :].astype(F32)) * d1
                  + _sigmoid(g2_ref[rows, :].astype(F32)) * d2)
        o_ref[rows, :] = x_ref[rows, :] + _dot(merged.astype(BF16), wo_ref[...])


def _merge(x, proj, y_fox, y_dn, w_branch, w_o, sc_conv_w, layer):
    tm = MERGE_TM
    scb = COL_SC // BRANCH_WIDTH
    gb = COL_GATE // D_MODEL
    hpb = tm // HALO

    def prev_rows(i):
        return jnp.maximum(i * hpb - 1, 0)

    return pl.pallas_call(
        _merge_kernel,
        grid=(TOKENS // tm,),
        in_specs=[
            pl.BlockSpec((tm, BRANCH_WIDTH), lambda i: (i, 0)),
            pl.BlockSpec((tm, BRANCH_WIDTH), lambda i: (i, scb)),
            pl.BlockSpec((tm, BRANCH_WIDTH), lambda i: (i, scb + 1)),
            pl.BlockSpec((tm, BRANCH_WIDTH), lambda i: (i, scb + 2)),
            pl.BlockSpec((HALO, BRANCH_WIDTH), lambda i: (prev_rows(i), scb + 1)),
            pl.BlockSpec((HALO, BRANCH_WIDTH), lambda i: (prev_rows(i), scb + 2)),
            pl.BlockSpec((tm, BRANCH_WIDTH), lambda i: (i, 0)),
            pl.BlockSpec((tm, D_MODEL), lambda i: (i, gb)),
            pl.BlockSpec((tm, D_MODEL), lambda i: (i, gb + 1)),
            pl.BlockSpec((tm, D_MODEL), lambda i: (i, gb + 2)),
            pl.BlockSpec((tm, D_MODEL), lambda i: (i, 0)),
            _of_layer(layer, (3, BRANCH_WIDTH, D_MODEL), resident=True),
            _of_layer(layer, (D_MODEL, D_MODEL), resident=True),
            _of_layer(layer, (3, BRANCH_WIDTH)),
        ],
        out_specs=pl.BlockSpec((tm, D_MODEL), lambda i: (i, 0)),
        out_shape=jax.ShapeDtypeStruct((TOKENS, D_MODEL), F32),
        compiler_params=_cparams(("parallel",)),
        name="merge",
    )(y_fox, proj, proj, proj, proj, proj, y_dn, proj, proj, proj, x, w_branch, w_o, sc_conv_w)


def _ffn_kernel(x_ref, xp_ref, gf_ref, wup_ref, cw_ref, wd_ref, gp_ref, wpg_ref, p_ref, wple_ref, o_ref,
                act_ref):
    i = pl.program_id(0)
    keep_prev = jnp.where(i % (SEQ // FFN_TM) != 0, 1.0, 0.0)
    xx = jnp.concatenate([xp_ref[...] * keep_prev, x_ref[...]], axis=0)
    hn = _rmsnorm(xx, gf_ref[...]).astype(BF16)

    def branch(cols):
        u = _dot(hn, wup_ref[:, cols])
        cw = cw_ref[:, cols]
        return cw[2:3] * u[HALO:] + cw[1:2] * _shifted(u, 1) + cw[0:1] * _shifted(u, 2)

    for j in range(FFN_NF):
        gate = branch(slice(j * FFN_TF, (j + 1) * FFN_TF))
        val = branch(slice(D_FF + j * FFN_TF, D_FF + (j + 1) * FFN_TF))
        act_ref[:, j * FFN_TF:(j + 1) * FFN_TF] = (_silu(gate) * val).astype(BF16)

    x1 = x_ref[...] + _dot(act_ref[...], wd_ref[...])
    h2 = _rmsnorm(x1, gp_ref[...]).astype(BF16)
    pgate = _sigmoid(_dot(h2, wpg_ref[...]))
    emb = _dot(p_ref[...].astype(BF16), wple_ref[...])
    o_ref[...] = x1 + pgate * emb


def _ffn(x, p, g_ffn, w_up, conv_w, w_down, g_ple, w_pg, w_ple, layer):
    tm = FFN_TM
    hpb = tm // HALO

    def whole(shape):
        return _of_layer(layer, shape, resident=True)

    return pl.pallas_call(
        _ffn_kernel,
        grid=(TOKENS // tm,),
        in_specs=[
            pl.BlockSpec((tm, D_MODEL), lambda i: (i, 0)),
            pl.BlockSpec((HALO, D_MODEL), lambda i: (jnp.maximum(i * hpb - 1, 0), 0)),
            whole((1, D_MODEL)),
            whole((D_MODEL, 2 * D_FF)),
            whole((3, 2 * D_FF)),
            whole((D_FF, D_MODEL)),
            whole((1, D_MODEL)),
            whole((D_MODEL, D_MODEL)),
            _of_layer(layer, (tm, PLE_DIM), lambda i: (i, 0)),
            whole((PLE_DIM, D_MODEL)),
        ],
        out_specs=pl.BlockSpec((tm, D_MODEL), lambda i: (i, 0)),
        out_shape=jax.ShapeDtypeStruct((TOKENS, D_MODEL), F32),
        scratch_shapes=[pltpu.VMEM((tm, D_FF), BF16)],
        compiler_params=_cparams(("parallel",)),
        name="ffn",
    )(x, x, g_ffn, w_up, conv_w, w_down, g_ple, w_pg, p, w_ple)


def _lane_rows(values, offset):
    depth, n = values.shape
    return jnp.pad(values.astype(F32), ((0, 0), (offset, LANES - offset - n)))[:, None, :]


def kernel(x, p, g_mix, w_in, b_fox_f, fox_q_gain, fox_k_gain, sc_conv_w, dn_conv_w, dn_a_log,
           dn_dt_bias, dn_norm_gain, w_branch, w_o, g_ffn, w_up, ffn_conv_w, w_down, g_ple,
           w_ple_gate, w_ple):
    depth = p.shape[0]
    x = x.reshape(TOKENS, D_MODEL)
    p = p.reshape(depth, TOKENS, PLE_DIM)

    w_big, w_small = _regroup(w_in)
    w_branch16, w_o16, w_up16, w_down16 = (w.astype(BF16) for w in (w_branch, w_o, w_up, w_down))
    w_pg16, w_ple16 = w_ple_gate.astype(BF16), w_ple.astype(BF16)
    g_mix3, g_ffn3, g_ple3, dn_gain3 = (g[:, None, :] for g in (g_mix, g_ffn, g_ple, dn_norm_gain))
    q_gain3 = jnp.tile(fox_q_gain, (1, 2))[:, None, :]
    k_gain3 = jnp.tile(fox_k_gain, (1, 2))[:, None, :]
    fox_bias = _lane_rows(b_fox_f, SM_FOXF)
    a_log = _lane_rows(dn_a_log, SM_A)
    dt_bias = _lane_rows(dn_dt_bias, SM_A)

    for layer in range(depth):
        proj, small = _proj(x, g_mix3, w_big, w_small, layer)
        cf = _cumf(small, fox_bias, layer)
        y_fox = _fox(proj, cf, q_gain3, k_gain3, layer)
        y_dn = _deltanet(proj, small, dn_conv_w, a_log, dt_bias, dn_gain3, layer)
        x = _merge(x, proj, y_fox, y_dn, w_branch16, w_o16, sc_conv_w, layer)
        x = _ffn(x, p, g_ffn3, w_up16, ffn_conv_w, w_down16, g_ple3, w_pg16, w_ple16, layer)
    return x.reshape(BATCH, SEQ, D_MODEL)
```

```python
import jax
import jax.numpy as jnp
import numpy as np
from jax import lax
from jax.experimental import pallas as pl
from jax.experimental.pallas import tpu as pltpu

F32 = jnp.float32
BF16 = jnp.bfloat16
HI = lax.Precision.HIGHEST

D_MODEL = 1024
BATCH = 8
SEQ = 2048
TOKENS = BATCH * SEQ
PLE_DIM = 256
BRANCH_WIDTH = 512
FOX_HEADS = 8
FOX_HEAD_DIM = 64
DN_HEADS = 4
DN_HEAD_DIM = 128
DN_CHUNK = 64
N_CHUNKS = SEQ // DN_CHUNK
D_FF = 2816
EPS = 1e-6

LANES = 128
SUBLANES = 8
HALO = 16

PROJ_WIDTH = 8192
COL_FOX = 0
COL_SC = 1536
COL_DN = 3072
COL_DNZ = 4608
COL_GATE = 5120
SM_FOXF = 0
SM_BETA = 8
SM_A = 12

PROJ_TM = 1024
PROJ_TN = 2048
FOX_TQ = 512
FOX_TK = 512
MERGE_TM = 512
MERGE_SPLIT = 2
FFN_TM = 512
FFN_TF = 256
FFN_NF = D_FF // FFN_TF

VMEM_LIMIT = 48 * 1024 * 1024


def _cparams(sem):
    return pltpu.CompilerParams(dimension_semantics=sem, vmem_limit_bytes=VMEM_LIMIT)


def _sigmoid(x):
    return 1.0 / (1.0 + jnp.exp(-x))


def _silu(x):
    half = 0.5 * x
    return half + half * jnp.tanh(half)


def _softplus(x):
    return jnp.maximum(x, 0.0) + jnp.log(1.0 + jnp.exp(-jnp.abs(x)))


def _dot(a, b, precision=None):
    return jnp.dot(a, b, preferred_element_type=F32, precision=precision)


def _dot_nt(a, b):
    return lax.dot_general(a, b, (((1,), (1,)), ((), ())), preferred_element_type=F32)


def _dot_tn(a, b):
    return lax.dot_general(a, b, (((0,), (0,)), ((), ())), preferred_element_type=F32)


def _rmsnorm(x, gain):
    ms = jnp.mean(x * x, axis=-1, keepdims=True)
    return x * lax.rsqrt(ms + EPS) * gain


def _shifted(cat, shift):
    if shift == 0:
        return cat[HALO:]
    return pltpu.roll(cat, shift, 0)[HALO:]


def _bf16_split3(x):
    hi = x.astype(BF16).astype(F32)
    r = x - hi
    mid = r.astype(BF16).astype(F32)
    lo = (r - mid).astype(BF16).astype(F32)
    return hi, mid, lo


IN_F = 3 * BRANCH_WIDTH
IN_SC = IN_F + FOX_HEADS
IN_DN = IN_SC + 3 * BRANCH_WIDTH
IN_B = IN_DN + 3 * BRANCH_WIDTH
IN_Z = IN_B + 2 * DN_HEADS
IN_G = IN_Z + BRANCH_WIDTH
IN_WIDTH = IN_G + 3 * D_MODEL
REGROUP_ROWS = 256
assert IN_F % LANES == SM_FOXF and SM_FOXF + FOX_HEADS == SM_BETA
assert IN_B % LANES == SM_BETA and SM_BETA + DN_HEADS == SM_A


def _regroup_kernel(w_ref, big_ref, small_ref):
    col = 0
    for a, b in ((0, IN_F), (IN_SC, IN_DN), (IN_DN, IN_B), (IN_Z, IN_G), (IN_G, IN_WIDTH)):
        big_ref[:, col:col + (b - a)] = w_ref[:, a:b].astype(BF16)
        col += b - a
    lane = lax.broadcasted_iota(jnp.int32, (1, LANES), 1)
    f_tile = w_ref[:, IN_F - SM_FOXF:IN_F - SM_FOXF + LANES]
    b_tile = w_ref[:, IN_B - SM_BETA:IN_B - SM_BETA + LANES]
    small_ref[...] = jnp.where(lane < SM_BETA, f_tile,
                               jnp.where(lane < SM_A + DN_HEADS, b_tile, 0.0)).astype(BF16)


def _regroup(w_in):
    depth = w_in.shape[0]
    steps = D_MODEL // REGROUP_ROWS
    return pl.pallas_call(
        _regroup_kernel,
        grid=(depth, steps),
        in_specs=[pl.BlockSpec((REGROUP_ROWS, IN_WIDTH), lambda l, i: (l * steps + i, 0))],
        out_specs=[
            pl.BlockSpec((None, REGROUP_ROWS, PROJ_WIDTH), lambda l, i: (l, i, 0)),
            pl.BlockSpec((None, REGROUP_ROWS, LANES), lambda l, i: (l, i, 0)),
        ],
        out_shape=[
            jax.ShapeDtypeStruct((depth, D_MODEL, PROJ_WIDTH), BF16),
            jax.ShapeDtypeStruct((depth, D_MODEL, LANES), BF16),
        ],
        compiler_params=_cparams(("parallel", "parallel")),
        name="regroup",
    )(w_in.reshape(depth * D_MODEL, IN_WIDTH))


def _proj_kernel(x_ref, g_ref, w_ref, ws_ref, o_ref, os_ref, hn_ref):
    @pl.when(pl.program_id(1) == 0)
    def _():
        hn = _rmsnorm(x_ref[...], g_ref[...]).astype(BF16)
        hn_ref[...] = hn
        os_ref[...] = _dot(hn, ws_ref[...])

    o_ref[...] = _dot(hn_ref[...], w_ref[...]).astype(BF16)


def _of_layer(layer, block, index_map=None, resident=False):
    if index_map is None:
        index_map = lambda *grid: (0,) * len(block)
    mode = dict(pipeline_mode=pl.Buffered(1)) if resident else {}
    return pl.BlockSpec((None,) + tuple(block), lambda *grid: (layer,) + tuple(index_map(*grid)), **mode)


def _proj(x, gain, w_big, w_small, layer):
    grid = (TOKENS // PROJ_TM, PROJ_WIDTH // PROJ_TN)
    return pl.pallas_call(
        _proj_kernel,
        grid=grid,
        in_specs=[
            pl.BlockSpec((PROJ_TM, D_MODEL), lambda i, j: (i, 0)),
            _of_layer(layer, (1, D_MODEL)),
            _of_layer(layer, (D_MODEL, PROJ_TN), lambda i, j: (0, j)),
            _of_layer(layer, (D_MODEL, LANES)),
        ],
        out_specs=[
            pl.BlockSpec((PROJ_TM, PROJ_TN), lambda i, j: (i, j)),
            pl.BlockSpec((PROJ_TM, LANES), lambda i, j: (i, 0)),
        ],
        out_shape=[
            jax.ShapeDtypeStruct((TOKENS, PROJ_WIDTH), BF16),
            jax.ShapeDtypeStruct((TOKENS, LANES), F32),
        ],
        scratch_shapes=[pltpu.VMEM((PROJ_TM, D_MODEL), BF16)],
        compiler_params=_cparams(("parallel", "arbitrary")),
        name="proj",
    )(x, gain, w_big, w_small)


CUMF_BLK = 128


def _cumf_kernel(sm_ref, bf_ref, cf_ref):
    r = lax.broadcasted_iota(jnp.int32, (CUMF_BLK, CUMF_BLK), 0)
    c = lax.broadcasted_iota(jnp.int32, (CUMF_BLK, CUMF_BLK), 1)
    tri = (r >= c).astype(F32)
    carry = jnp.zeros((1, LANES), F32)
    for blk in range(SEQ // CUMF_BLK):
        rows = slice(blk * CUMF_BLK, (blk + 1) * CUMF_BLK)
        log_f = -_softplus(-(sm_ref[rows, :] + bf_ref[...]))
        cum = _dot(tri, log_f, HI) + carry
        carry = cum[CUMF_BLK - 1:CUMF_BLK, :]
        cf_ref[0, rows, :] = cum


def _cumf(small, bias_row, layer):
    return pl.pallas_call(
        _cumf_kernel,
        grid=(BATCH,),
        in_specs=[
            pl.BlockSpec((SEQ, LANES), lambda b: (b, 0)),
            _of_layer(layer, (1, LANES)),
        ],
        out_specs=pl.BlockSpec((1, SEQ, LANES), lambda b: (b, 0, 0)),
        out_shape=jax.ShapeDtypeStruct((BATCH, SEQ, LANES), F32),
        compiler_params=_cparams(("parallel",)),
        name="cumf",
    )(small, bias_row)


AUG = FOX_HEAD_DIM
LOG2E = 1.4426950408889634


def _fox_selectors():
    sel = np.zeros((2, FOX_HEADS // 2, 4 * LANES, 2 * LANES), np.float32)
    ones = np.zeros((2, 1, 2 * LANES), np.float32)
    for hp in range(FOX_HEADS // 2):
        for e in range(2):
            base = e * LANES
            for d in range(FOX_HEAD_DIM):
                sel[:, hp, e * FOX_HEAD_DIM + d, base + d] = 1.0
            for part in range(3):
                src = (1 + part) * LANES + SM_FOXF + 2 * hp + e
                sel[0, hp, src, base + AUG + part] = 1.0
                sel[1, hp, src, base + AUG + 3 + part] = -1.0
    for e in range(2):
        ones[0, 0, e * LANES + AUG + 3:e * LANES + AUG + 6] = 1.0
        ones[1, 0, e * LANES + AUG:e * LANES + AUG + 3] = 1.0
    return jnp.asarray(sel, BF16), jnp.asarray(ones, F32)


def _fox_kernel(q_ref, qnext_ref, k_ref, v_ref, cfq_ref, cfqnext_ref, cfk_ref, qg_ref, kg_ref, sel_ref,
                ones_ref, o_ref, ka_ref, vt_ref, qa_ref, acc_ref, s_ref):
    qi = pl.program_id(1)
    lane = lax.broadcasted_iota(jnp.int32, (1, LANES), 1)
    first_head = lane < FOX_HEAD_DIM

    def headnorm(x, gain):
        x2 = x * x
        s0 = jnp.sum(jnp.where(first_head, x2, 0.0), axis=-1, keepdims=True)
        s1 = jnp.sum(jnp.where(first_head, 0.0, x2), axis=-1, keepdims=True)
        inv = jnp.where(first_head,
                        lax.rsqrt(s0 * (1.0 / FOX_HEAD_DIM) + EPS),
                        lax.rsqrt(s1 * (1.0 / FOX_HEAD_DIM) + EPS))
        return x * inv * gain

    def cum_parts(cf):
        return [part.astype(BF16) for part in _bf16_split3(cf * LOG2E)]

    def augmented(xn, parts, hp, side):
        packed = jnp.concatenate([xn.astype(BF16)] + parts, axis=1)
        return (_dot(packed, sel_ref[side, hp]) + ones_ref[side]).astype(BF16)

    hs = range(FOX_HEADS)

    def hrows(h):
        return slice(h * FOX_HEAD_DIM, (h + 1) * FOX_HEAD_DIM)

    def scores(h, j):
        rows = pl.ds(pl.multiple_of(j * FOX_TK, FOX_TK), FOX_TK)
        return _dot_nt(ka_ref[h, rows, :], qa_ref[h])

    def prepare_queries(src_ref, cf):
        parts = cum_parts(cf)
        for hp in range(FOX_HEADS // 2):
            pair = slice(hp * LANES, (hp + 1) * LANES)
            qn = headnorm(src_ref[:, pair].astype(F32), qg_ref[...]) * (FOX_HEAD_DIM ** -0.5 * LOG2E)
            qa = augmented(qn, parts, hp, 0)
            qa_ref[2 * hp] = qa[:, :LANES]
            qa_ref[2 * hp + 1] = qa[:, LANES:]

    @pl.when(qi == 0)
    def _():
        def body(c, carry):
            off = pl.multiple_of(c * FOX_TK, FOX_TK)
            rows = pl.ds(off, FOX_TK)
            parts = cum_parts(cfk_ref[0, rows, :])
            for hp in range(FOX_HEADS // 2):
                pair = slice(hp * LANES, (hp + 1) * LANES)
                kn = headnorm(k_ref[rows, pair].astype(F32), kg_ref[...])
                ka = augmented(kn, parts, hp, 1)
                ka_ref[2 * hp, rows, :] = ka[:, :LANES]
                ka_ref[2 * hp + 1, rows, :] = ka[:, LANES:]
                vt_ref[pair, rows] = v_ref[rows, pair].astype(F32).T.astype(BF16)
            return carry
        lax.fori_loop(0, SEQ // FOX_TK, body, 0)
        prepare_queries(q_ref, cfq_ref[0])
        for h in hs:
            s_ref[h] = scores(h, 0)

    acc_ref[...] = jnp.zeros_like(acc_ref)

    def absorb(s, m, l, h, krows, qcols):
        m_new = jnp.maximum(m, jnp.max(s, axis=0, keepdims=True))
        alpha = jnp.exp2(m - m_new)
        p = jnp.exp2(s - m_new)
        l_new = alpha * l + jnp.sum(p, axis=0, keepdims=True)
        pv = _dot(vt_ref[hrows(h), krows], p.astype(BF16))
        acc_ref[hrows(h), qcols] = alpha * acc_ref[hrows(h), qcols] + pv
        return m_new, l_new

    half = FOX_TQ // 2
    left, right = slice(0, half), slice(half, FOX_TQ)

    def step(j, nxt, carry, diagonal):
        k0 = pl.multiple_of(j * FOX_TK, FOX_TK)
        out = []
        for h in hs:
            (m_l, l_l), (m_r, l_r) = carry[h]
            if not diagonal:
                s_l, s_r = s_ref[h, :, left], s_ref[h, :, right]
                if nxt is not None:
                    s_ref[h] = scores(h, nxt)
                keys = pl.ds(k0, FOX_TK)
                out.append((absorb(s_l, m_l, l_l, h, keys, left), absorb(s_r, m_r, l_r, h, keys, right)))
                continue
            krow = lax.broadcasted_iota(jnp.int32, (half, half), 0)
            qcol = lax.broadcasted_iota(jnp.int32, (half, half), 1)
            causal = qcol >= krow
            early, late = pl.ds(k0, half), pl.ds(k0 + half, half)
            s_ll = jnp.where(causal, s_ref[h, left, left], -jnp.inf)
            s_lr = s_ref[h, left, right]
            s_rr = jnp.where(causal, s_ref[h, right, right], -jnp.inf)
            if nxt is not None:
                s_ref[h] = scores(h, nxt)
            out.append((absorb(s_ll, m_l, l_l, h, early, left),
                        absorb(s_rr, *absorb(s_lr, m_r, l_r, h, early, right), h, late, right)))
        return tuple(out)

    def fresh():
        return jnp.full((1, half), -jnp.inf, F32), jnp.zeros((1, half), F32)

    init = tuple((fresh(), fresh()) for _ in hs)
    carry = lax.fori_loop(0, qi, lambda j, c: step(j, j + 1, c, False), init)

    def finish(more_queries):
        if more_queries:
            prepare_queries(qnext_ref, cfqnext_ref[0])
        final = step(qi, 0 if more_queries else None, carry, True)
        sub = lax.broadcasted_iota(jnp.int32, (LANES, 1), 0)
        for hp in range(FOX_HEADS // 2):
            pair = slice(hp * LANES, (hp + 1) * LANES)
            for side, qcols in enumerate((left, right)):
                denom = jnp.where(sub < FOX_HEAD_DIM, final[2 * hp][side][1], final[2 * hp + 1][side][1])
                o_ref[qcols, pair] = (acc_ref[pair, qcols] / denom).T.astype(BF16)

    last_block = pl.num_programs(1) - 1
    pl.when(qi < last_block)(lambda: finish(True))
    pl.when(qi == last_block)(lambda: finish(False))


def _fox(proj, cf, q_gain2, k_gain2, layer):
    nq = SEQ // FOX_TQ
    fb = COL_FOX // BRANCH_WIDTH
    sel, ones = _fox_selectors()
    return pl.pallas_call(
        _fox_kernel,
        grid=(BATCH, nq),
        in_specs=[
            pl.BlockSpec((FOX_TQ, BRANCH_WIDTH), lambda b, qi: (b * nq + qi, fb)),
            pl.BlockSpec((FOX_TQ, BRANCH_WIDTH), lambda b, qi: (b * nq + jnp.minimum(qi + 1, nq - 1), fb)),
            pl.BlockSpec((SEQ, BRANCH_WIDTH), lambda b, qi: (b, fb + 1)),
            pl.BlockSpec((SEQ, BRANCH_WIDTH), lambda b, qi: (b, fb + 2)),
            pl.BlockSpec((1, FOX_TQ, LANES), lambda b, qi: (b, qi, 0)),
            pl.BlockSpec((1, FOX_TQ, LANES), lambda b, qi: (b, jnp.minimum(qi + 1, nq - 1), 0)),
            pl.BlockSpec((1, SEQ, LANES), lambda b, qi: (b, 0, 0)),
            _of_layer(layer, (1, LANES)),
            _of_layer(layer, (1, LANES)),
            pl.BlockSpec(sel.shape, lambda b, qi: (0, 0, 0, 0)),
            pl.BlockSpec(ones.shape, lambda b, qi: (0, 0, 0)),
        ],
        out_specs=pl.BlockSpec((FOX_TQ, BRANCH_WIDTH), lambda b, qi: (b * nq + qi, 0)),
        out_shape=jax.ShapeDtypeStruct((TOKENS, BRANCH_WIDTH), BF16),
        scratch_shapes=[
            pltpu.VMEM((FOX_HEADS, SEQ, LANES), BF16),
            pltpu.VMEM((BRANCH_WIDTH, SEQ), BF16),
            pltpu.VMEM((FOX_HEADS, FOX_TQ, LANES), BF16),
            pltpu.VMEM((BRANCH_WIDTH, FOX_TQ), F32),
            pltpu.VMEM((FOX_HEADS, FOX_TK, FOX_TQ), F32),
        ],
        compiler_params=_cparams(("parallel", "arbitrary")),
        name="fox",
    )(proj, proj, proj, proj, cf, cf, cf, q_gain2, k_gain2, sel, ones)


DN_STREAMS = 4

def _dn_kernel(q_ref, k_ref, v_ref, z_ref, sm_ref, wq_ref, wk_ref, wv_ref, alog_ref, dtb_ref,
               ng_ref, o_ref, u_ref, kcum_ref, qdec_ref, kdec_ref, qk_ref, gt_ref, st_ref):
    c = DN_CHUNK
    ri = lax.broadcasted_iota(jnp.int32, (c, c), 0)
    ci = lax.broadcasted_iota(jnp.int32, (c, c), 1)
    incl = ri >= ci
    strict = ri > ci
    eye = ri == ci
    eye_f = eye.astype(F32)
    tri3 = jnp.concatenate([jnp.where(incl, 1.0, 0.0)] * 3, axis=1).astype(BF16)
    lane = lax.broadcasted_iota(jnp.int32, (1, LANES), 1)

    n_taps = 4
    dr = lax.broadcasted_iota(jnp.int32, (n_taps * c, 2 * c), 0)
    dc = lax.broadcasted_iota(jnp.int32, (n_taps * c, 2 * c), 1)
    delays = jnp.where(dc == c + dr % c - dr // c, 1.0, 0.0).astype(BF16)

    def conv_silu(taps, w, cols):
        y = w[n_taps - 1:n_taps] * taps[0:c, cols]
        for j in range(1, n_taps):
            y = y + w[n_taps - 1 - j:n_taps - j] * taps[j * c:(j + 1) * c, cols]
        return _silu(y)

    def l2norm(x):
        return x * lax.rsqrt(jnp.sum(x * x, axis=-1, keepdims=True) + EPS)

    def heads():
        return [slice(h * DN_HEAD_DIM, (h + 1) * DN_HEAD_DIM) for h in range(DN_HEADS)]

    def rows_of(chunk):
        if isinstance(chunk, int):
            return chunk * c, max(chunk - 1, 0) * c
        return pl.multiple_of(chunk * c, c), pl.multiple_of(jnp.maximum(chunk - 1, 0) * c, c)

    def prepare(chunks):
        units = []
        raw, gcum_all, beta_all = [], [], []
        for chunk in chunks:
            r0, p0 = rows_of(chunk)
            rows = pl.ds(r0, c)
            prow = pl.ds(p0, c)
            keep_prev = jnp.where(chunk > 0, 1.0, 0.0).astype(BF16)
            taps = [_dot(delays, jnp.concatenate([ref[prow, :] * keep_prev, ref[rows, :]], axis=0))
                    for ref in (q_ref, k_ref, v_ref)]
            sm = sm_ref[rows, :]
            bsig = _sigmoid(sm)
            g_parts = _bf16_split3(-jnp.exp(alog_ref[...]) * _softplus(sm + dtb_ref[...]))
            gsum = _dot(tri3, jnp.concatenate(g_parts, axis=0).astype(BF16))
            for h, cols in enumerate(heads()):
                units.append((chunk, rows, h, cols))
                raw.append([(taps[a], w_ref[:, cols], cols)
                            for a, w_ref in enumerate((wq_ref, wk_ref, wv_ref))])
                gcum_all.append(gsum)
                beta_all.append(bsig)
        hs = range(len(units))
        gc, beta, q, k, v, decay = [], [], [], [], [], []
        for u in hs:
            head = units[u][2]
            g1 = jnp.sum(jnp.where(lane == SM_A + head, gcum_all[u], 0.0), axis=-1, keepdims=True)
            b1 = jnp.sum(jnp.where(lane == SM_BETA + head, beta_all[u], 0.0), axis=-1, keepdims=True)
            gc.append(jnp.broadcast_to(g1, (c, DN_HEAD_DIM)))
            beta.append(jnp.broadcast_to(b1, (c, DN_HEAD_DIM)))
            q.append(l2norm(conv_silu(*raw[u][0])) * (DN_HEAD_DIM ** -0.5))
            k.append(l2norm(conv_silu(*raw[u][1])))
            v.append(conv_silu(*raw[u][2]))
            gc_sq = gc[u][:, :c]
            g_row = jnp.sum(jnp.where(eye, gc_sq, 0.0), axis=0, keepdims=True)
            decay.append(jnp.exp(jnp.where(incl, gc_sq - g_row, -jnp.inf)))
        kb = [k[h] * beta[h] for h in hs]
        k16 = [k[h].astype(BF16) for h in hs]
        kk = [_dot_nt(kb[h].astype(BF16), k16[h]) for h in hs]
        qk = [_dot_nt(q[h].astype(BF16), k16[h]) for h in hs]
        yield
        neg = [jnp.where(strict, -(kk[h] * decay[h]), 0.0) for h in hs]
        qsum = [eye_f + neg[h] for h in hs]
        neg16 = [neg[h].astype(BF16) for h in hs]
        power = [_dot(neg16[h], neg16[h]) for h in hs]
        yield
        for _ in range(4):
            both = [_dot(power[h].astype(BF16),
                         jnp.concatenate([qsum[h], power[h]], axis=1).astype(BF16)) for h in hs]
            yield
            qsum = [qsum[h] + both[h][:, :c] for h in hs]
            power = [both[h][:, c:] for h in hs]
        last = [_dot(power[h].astype(BF16), qsum[h].astype(BF16)) for h in hs]
        yield
        eg = [jnp.exp(gc[h]) for h in hs]
        rhs = [jnp.concatenate([v[h] * beta[h], kb[h] * eg[h]], axis=1) for h in hs]
        sol = [rhs[h] + _dot((qsum[h] + last[h] - eye_f).astype(BF16), rhs[h].astype(BF16)) for h in hs]
        yield
        for u, (chunk, rows, head, cols) in enumerate(units):
            g_last = gc[u][c - 1:c, :]
            u_ref[rows, cols] = sol[u][:, :DN_HEAD_DIM]
            kcum_ref[rows, cols] = sol[u][:, DN_HEAD_DIM:].astype(BF16)
            qdec_ref[rows, cols] = (q[u] * eg[u]).astype(BF16)
            kdec_ref[rows, cols] = (k[u] * jnp.exp(g_last - gc[u])).astype(BF16)
            qk_ref[rows, head * c:(head + 1) * c] = jnp.where(incl, qk[u] * decay[u], 0.0).astype(BF16)
            gt_ref[chunk, :, cols] = jnp.broadcast_to(jnp.exp(g_last), (SUBLANES, DN_HEAD_DIM))

    def scan(chunks):
        hs = range(DN_HEADS)
        cols = heads()
        state = [st_ref[h] for h in hs]
        for chunk in chunks:
            rows = pl.ds(rows_of(chunk)[0], c)
            state16 = [state[h].astype(BF16) for h in hs]
            v16 = [(u_ref[rows, cols[h]] - _dot(kcum_ref[rows, cols[h]], state16[h])).astype(BF16)
                   for h in hs]
            yield
            upd = [_dot_tn(kdec_ref[rows, cols[h]], v16[h]) for h in hs]
            out = [_dot(qdec_ref[rows, cols[h]], state16[h])
                   + _dot(qk_ref[rows, h * c:(h + 1) * c], v16[h]) for h in hs]
            yield
            state = [state[h] * gt_ref[chunk, 0:1, cols[h]] + upd[h] for h in hs]
            for h in hs:
                o_ref[rows, cols[h]] = (_rmsnorm(out[h], ng_ref[...])
                                        * _silu(z_ref[rows, cols[h]].astype(F32))).astype(BF16)
        for h in hs:
            st_ref[h] = state[h]

    def interleave(*stages):
        active = list(stages)
        while active:
            for gen in list(active):
                if next(gen, StopIteration) is StopIteration:
                    active.remove(gen)

    def pair(n):
        return [DN_STREAMS * n + t for t in range(DN_STREAMS)]

    n_iter = N_CHUNKS // DN_STREAMS
    st_ref[...] = jnp.zeros_like(st_ref)
    interleave(prepare(pair(0)))

    def body(n, carry):
        interleave(scan(pair(n - 1)), prepare(pair(n)))
        return carry

    lax.fori_loop(1, n_iter, body, 0)
    interleave(scan(pair(n_iter - 1)))


def _deltanet(proj, small, conv_w, alog_row, dtb_row, norm_gain, layer):
    qb = COL_DN // BRANCH_WIDTH
    zb = COL_DNZ // BRANCH_WIDTH

    def colspec(blk):
        return pl.BlockSpec((SEQ, BRANCH_WIDTH), lambda b: (b, blk))

    def wspec(blk):
        return _of_layer(layer, (4, BRANCH_WIDTH), lambda b: (0, blk))

    row = _of_layer(layer, (1, LANES))
    return pl.pallas_call(
        _dn_kernel,
        grid=(BATCH,),
        in_specs=[
            colspec(qb), colspec(qb + 1), colspec(qb + 2), colspec(zb),
            pl.BlockSpec((SEQ, LANES), lambda b: (b, 0)),
            wspec(0), wspec(1), wspec(2),
            row, row, row,
        ],
        out_specs=pl.BlockSpec((SEQ, BRANCH_WIDTH), lambda b: (b, 0)),
        out_shape=jax.ShapeDtypeStruct((TOKENS, BRANCH_WIDTH), BF16),
        scratch_shapes=[
            pltpu.VMEM((SEQ, BRANCH_WIDTH), F32),
            pltpu.VMEM((SEQ, BRANCH_WIDTH), BF16),
            pltpu.VMEM((SEQ, BRANCH_WIDTH), BF16),
            pltpu.VMEM((SEQ, BRANCH_WIDTH), BF16),
            pltpu.VMEM((SEQ, DN_HEADS * DN_CHUNK), BF16),
            pltpu.VMEM((N_CHUNKS, SUBLANES, BRANCH_WIDTH), F32),
            pltpu.VMEM((DN_HEADS, DN_HEAD_DIM, DN_HEAD_DIM), F32),
        ],
        compiler_params=_cparams(("parallel",)),
        name="deltanet",
    )(proj, proj, proj, proj, small, conv_w, conv_w, conv_w, alog_row, dtb_row, norm_gain)


def _merge_kernel(yf_ref, sb_ref, sc_ref, sv_ref, scp_ref, svp_ref, yd_ref, g0_ref, g1_ref, g2_ref,
                  x_ref, wb_ref, wo_ref, cw_ref, o_ref):
    i = pl.program_id(0)
    keep_prev = jnp.where(i % (SEQ // MERGE_TM) != 0, 1.0, 0.0)
    cur = sc_ref[...].astype(F32) * sv_ref[...].astype(F32)
    prev = scp_ref[...].astype(F32) * svp_ref[...].astype(F32) * keep_prev
    cat = jnp.concatenate([prev, cur], axis=0)
    cw = cw_ref[...]
    conv = cw[2:3] * cur + cw[1:2] * _shifted(cat, 1) + cw[0:1] * _shifted(cat, 2)
    y_sc = (sb_ref[...].astype(F32) * conv).astype(BF16)
    sub = MERGE_TM // MERGE_SPLIT
    blocks = [slice(r * sub, (r + 1) * sub) for r in range(MERGE_SPLIT)]
    branch = [(_dot(yf_ref[rows, :], wb_ref[0]), _dot(y_sc[rows], wb_ref[1]), _dot(yd_ref[rows, :], wb_ref[2]))
              for rows in blocks]
    for rows, (d0, d1, d2) in zip(blocks, branch):
        merged = (_sigmoid(g0_ref[rows, :].astype(F32)) * d0 + _sigmoid(g1_ref[rows, :].astype(F32)) * d1
                  + _sigmoid(g2_ref[rows, :].astype(F32)) * d2)
        o_ref[rows, :] = x_ref[rows, :] + _dot(merged.astype(BF16), wo_ref[...])


def _merge(x, proj, y_fox, y_dn, w_branch, w_o, sc_conv_w, layer):
    tm = MERGE_TM
    scb = COL_SC // BRANCH_WIDTH
    gb = COL_GATE // D_MODEL
    hpb = tm // HALO

    def prev_rows(i):
        return jnp.maximum(i * hpb - 1, 0)

    return pl.pallas_call(
        _merge_kernel,
        grid=(TOKENS // tm,),
        in_specs=[
            pl.BlockSpec((tm, BRANCH_WIDTH), lambda i: (i, 0)),
            pl.BlockSpec((tm, BRANCH_WIDTH), lambda i: (i, scb)),
            pl.BlockSpec((tm, BRANCH_WIDTH), lambda i: (i, scb + 1)),
            pl.BlockSpec((tm, BRANCH_WIDTH), lambda i: (i, scb + 2)),
            pl.BlockSpec((HALO, BRANCH_WIDTH), lambda i: (prev_rows(i), scb + 1)),
            pl.BlockSpec((HALO, BRANCH_WIDTH), lambda i: (prev_rows(i), scb + 2)),
            pl.BlockSpec((tm, BRANCH_WIDTH), lambda i: (i, 0)),
            pl.BlockSpec((tm, D_MODEL), lambda i: (i, gb)),
            pl.BlockSpec((tm, D_MODEL), lambda i: (i, gb + 1)),
            pl.BlockSpec((tm, D_MODEL), lambda i: (i, gb + 2)),
            pl.BlockSpec((tm, D_MODEL), lambda i: (i, 0)),
            _of_layer(layer, (3, BRANCH_WIDTH, D_MODEL), resident=True),
            _of_layer(layer, (D_MODEL, D_MODEL), resident=True),
            _of_layer(layer, (3, BRANCH_WIDTH)),
        ],
        out_specs=pl.BlockSpec((tm, D_MODEL), lambda i: (i, 0)),
        out_shape=jax.ShapeDtypeStruct((TOKENS, D_MODEL), F32),
        compiler_params=_cparams(("parallel",)),
        name="merge",
    )(y_fox, proj, proj, proj, proj, proj, y_dn, proj, proj, proj, x, w_branch, w_o, sc_conv_w)


def _ffn_kernel(x_ref, xp_ref, gf_ref, wup_ref, cw_ref, wd_ref, gp_ref, wpg_ref, p_ref, wple_ref, o_ref,
                act_ref):
    i = pl.program_id(0)
    keep_prev = jnp.where(i % (SEQ // FFN_TM) != 0, 1.0, 0.0)
    xx = jnp.concatenate([xp_ref[...] * keep_prev, x_ref[...]], axis=0)
    hn = _rmsnorm(xx, gf_ref[...]).astype(BF16)

    def branch(cols):
        u = _dot(hn, wup_ref[:, cols])
        cw = cw_ref[:, cols]
        return cw[2:3] * u[HALO:] + cw[1:2] * _shifted(u, 1) + cw[0:1] * _shifted(u, 2)

    for j in range(FFN_NF):
        gate = branch(slice(j * FFN_TF, (j + 1) * FFN_TF))
        val = branch(slice(D_FF + j * FFN_TF, D_FF + (j + 1) * FFN_TF))
        act_ref[:, j * FFN_TF:(j + 1) * FFN_TF] = (_silu(gate) * val).astype(BF16)

    x1 = x_ref[...] + _dot(act_ref[...], wd_ref[...])
    h2 = _rmsnorm(x1, gp_ref[...]).astype(BF16)
    pgate = _sigmoid(_dot(h2, wpg_ref[...]))
    emb = _dot(p_ref[...].astype(BF16), wple_ref[...])
    o_ref[...] = x1 + pgate * emb


def _ffn(x, p, g_ffn, w_up, conv_w, w_down, g_ple, w_pg, w_ple, layer):
    tm = FFN_TM
    hpb = tm // HALO

    def whole(shape):
        return _of_layer(layer, shape, resident=True)

    return pl.pallas_call(
        _ffn_kernel,
        grid=(TOKENS // tm,),
        in_specs=[
            pl.BlockSpec((tm, D_MODEL), lambda i: (i, 0)),
            pl.BlockSpec((HALO, D_MODEL), lambda i: (jnp.maximum(i * hpb - 1, 0), 0)),
            whole((1, D_MODEL)),
            whole((D_MODEL, 2 * D_FF)),
            whole((3, 2 * D_FF)),
            whole((D_FF, D_MODEL)),
            whole((1, D_MODEL)),
            whole((D_MODEL, D_MODEL)),
            _of_layer(layer, (tm, PLE_DIM), lambda i: (i, 0)),
            whole((PLE_DIM, D_MODEL)),
        ],
        out_specs=pl.BlockSpec((tm, D_MODEL), lambda i: (i, 0)),
        out_shape=jax.ShapeDtypeStruct((TOKENS, D_MODEL), F32),
        scratch_shapes=[pltpu.VMEM((tm, D_FF), BF16)],
        compiler_params=_cparams(("parallel",)),
        name="ffn",
    )(x, x, g_ffn, w_up, conv_w, w_down, g_ple, w_pg, p, w_ple)


def _lane_rows(values, offset):
    depth, n = values.shape
    return jnp.pad(values.astype(F32), ((0, 0), (offset, LANES - offset - n)))[:, None, :]


def kernel(x, p, g_mix, w_in, b_fox_f, fox_q_gain, fox_k_gain, sc_conv_w, dn_conv_w, dn_a_log,
           dn_dt_bias, dn_norm_gain, w_branch, w_o, g_ffn, w_up, ffn_conv_w, w_down, g_ple,
           w_ple_gate, w_ple):
    depth = p.shape[0]
    x = x.reshape(TOKENS, D_MODEL)
    p = p.reshape(depth, TOKENS, PLE_DIM)

    w_big, w_small = _regroup(w_in)
    w_branch16, w_o16, w_up16, w_down16 = (w.astype(BF16) for w in (w_branch, w_o, w_up, w_down))
    w_pg16, w_ple16 = w_ple_gate.astype(BF16), w_ple.astype(BF16)
    g_mix3, g_ffn3, g_ple3, dn_gain3 = (g[:, None, :] for g in (g_mix, g_ffn, g_ple, dn_norm_gain))
    q_gain3 = jnp.tile(fox_q_gain, (1, 2))[:, None, :]
    k_gain3 = jnp.tile(fox_k_gain, (1, 2))[:, None, :]
    fox_bias = _lane_rows(b_fox_f, SM_FOXF)
    a_log = _lane_rows(dn_a_log, SM_A)
    dt_bias = _lane_rows(dn_dt_bias, SM_A)

    for layer in range(depth):
        proj, small = _proj(x, g_mix3, w_big, w_small, layer)
        cf = _cumf(small, fox_bias, layer)
        y_fox = _fox(proj, cf, q_gain3, k_gain3, layer)
        y_dn = _deltanet(proj, small, dn_conv_w, a_log, dt_bias, dn_gain3, layer)
        x = _merge(x, proj, y_fox, y_dn, w_branch16, w_o16, sc_conv_w, layer)
        x = _ffn(x, p, g_ffn3, w_up16, ffn_conv_w, w_down16, g_ple3, w_pg16, w_ple16, layer)
    return x.reshape(BATCH, SEQ, D_MODEL)
```

```python
import jax
import jax.numpy as jnp
import numpy as np
from jax import lax
from jax.experimental import pallas as pl
from jax.experimental.pallas import tpu as pltpu

F32 = jnp.float32
BF16 = jnp.bfloat16
HI = lax.Precision.HIGHEST

D_MODEL = 1024
BATCH = 8
SEQ = 2048
TOKENS = BATCH * SEQ
PLE_DIM = 256
BRANCH_WIDTH = 512
FOX_HEADS = 8
FOX_HEAD_DIM = 64
DN_HEADS = 4
DN_HEAD_DIM = 128
DN_CHUNK = 64
N_CHUNKS = SEQ // DN_CHUNK
D_FF = 2816
EPS = 1e-6

LANES = 128
SUBLANES = 8
HALO = 16

PROJ_WIDTH = 8192
COL_FOX = 0
COL_SC = 1536
COL_DN = 3072
COL_DNZ = 4608
COL_GATE = 5120
SM_FOXF = 0
SM_BETA = 8
SM_A = 12

PROJ_TM = 1024
PROJ_TN = 2048
FOX_TQ = 512
FOX_TK = 512
MERGE_TM = 512
MERGE_SPLIT = 2
FFN_TM = 1024
FFN_TF = 256
FFN_NF = D_FF // FFN_TF

VMEM_LIMIT = 48 * 1024 * 1024


def _cparams(sem):
    return pltpu.CompilerParams(dimension_semantics=sem, vmem_limit_bytes=VMEM_LIMIT)


def _sigmoid(x):
    return 1.0 / (1.0 + jnp.exp(-x))


def _silu(x):
    half = 0.5 * x
    return half + half * jnp.tanh(half)


def _softplus(x):
    return jnp.maximum(x, 0.0) + jnp.log(1.0 + jnp.exp(-jnp.abs(x)))


def _dot(a, b, precision=None):
    return jnp.dot(a, b, preferred_element_type=F32, precision=precision)


def _dot_nt(a, b):
    return lax.dot_general(a, b, (((1,), (1,)), ((), ())), preferred_element_type=F32)


def _dot_tn(a, b):
    return lax.dot_general(a, b, (((0,), (0,)), ((), ())), preferred_element_type=F32)


def _rmsnorm(x, gain):
    ms = jnp.mean(x * x, axis=-1, keepdims=True)
    return x * lax.rsqrt(ms + EPS) * gain


def _shifted(cat, shift):
    if shift == 0:
        return cat[HALO:]
    return pltpu.roll(cat, shift, 0)[HALO:]


def _bf16_split3(x):
    hi = x.astype(BF16).astype(F32)
    r = x - hi
    mid = r.astype(BF16).astype(F32)
    lo = (r - mid).astype(BF16).astype(F32)
    return hi, mid, lo


IN_F = 3 * BRANCH_WIDTH
IN_SC = IN_F + FOX_HEADS
IN_DN = IN_SC + 3 * BRANCH_WIDTH
IN_B = IN_DN + 3 * BRANCH_WIDTH
IN_Z = IN_B + 2 * DN_HEADS
IN_G = IN_Z + BRANCH_WIDTH
IN_WIDTH = IN_G + 3 * D_MODEL
REGROUP_ROWS = 256
assert IN_F % LANES == SM_FOXF and SM_FOXF + FOX_HEADS == SM_BETA
assert IN_B % LANES == SM_BETA and SM_BETA + DN_HEADS == SM_A


def _regroup_kernel(w_ref, big_ref, small_ref):
    col = 0
    for a, b in ((0, IN_F), (IN_SC, IN_DN), (IN_DN, IN_B), (IN_Z, IN_G), (IN_G, IN_WIDTH)):
        big_ref[:, col:col + (b - a)] = w_ref[:, a:b].astype(BF16)
        col += b - a
    lane = lax.broadcasted_iota(jnp.int32, (1, LANES), 1)
    f_tile = w_ref[:, IN_F - SM_FOXF:IN_F - SM_FOXF + LANES]
    b_tile = w_ref[:, IN_B - SM_BETA:IN_B - SM_BETA + LANES]
    small_ref[...] = jnp.where(lane < SM_BETA, f_tile,
                               jnp.where(lane < SM_A + DN_HEADS, b_tile, 0.0)).astype(BF16)


def _regroup(w_in):
    depth = w_in.shape[0]
    steps = D_MODEL // REGROUP_ROWS
    return pl.pallas_call(
        _regroup_kernel,
        grid=(depth, steps),
        in_specs=[pl.BlockSpec((REGROUP_ROWS, IN_WIDTH), lambda l, i: (l * steps + i, 0))],
        out_specs=[
            pl.BlockSpec((None, REGROUP_ROWS, PROJ_WIDTH), lambda l, i: (l, i, 0)),
            pl.BlockSpec((None, REGROUP_ROWS, LANES), lambda l, i: (l, i, 0)),
        ],
        out_shape=[
            jax.ShapeDtypeStruct((depth, D_MODEL, PROJ_WIDTH), BF16),
            jax.ShapeDtypeStruct((depth, D_MODEL, LANES), BF16),
        ],
        compiler_params=_cparams(("parallel", "parallel")),
        name="regroup",
    )(w_in.reshape(depth * D_MODEL, IN_WIDTH))


def _proj_kernel(x_ref, g_ref, w_ref, ws_ref, o_ref, os_ref, hn_ref):
    @pl.when(pl.program_id(1) == 0)
    def _():
        hn = _rmsnorm(x_ref[...], g_ref[...]).astype(BF16)
        hn_ref[...] = hn
        os_ref[...] = _dot(hn, ws_ref[...])

    o_ref[...] = _dot(hn_ref[...], w_ref[...]).astype(BF16)


def _of_layer(layer, block, index_map=None, resident=False):
    if index_map is None:
        index_map = lambda *grid: (0,) * len(block)
    mode = dict(pipeline_mode=pl.Buffered(1)) if resident else {}
    return pl.BlockSpec((None,) + tuple(block), lambda *grid: (layer,) + tuple(index_map(*grid)), **mode)


def _proj(x, gain, w_big, w_small, layer):
    grid = (TOKENS // PROJ_TM, PROJ_WIDTH // PROJ_TN)
    return pl.pallas_call(
        _proj_kernel,
        grid=grid,
        in_specs=[
            pl.BlockSpec((PROJ_TM, D_MODEL), lambda i, j: (i, 0)),
            _of_layer(layer, (1, D_MODEL)),
            _of_layer(layer, (D_MODEL, PROJ_TN), lambda i, j: (0, j)),
            _of_layer(layer, (D_MODEL, LANES)),
        ],
        out_specs=[
            pl.BlockSpec((PROJ_TM, PROJ_TN), lambda i, j: (i, j)),
            pl.BlockSpec((PROJ_TM, LANES), lambda i, j: (i, 0)),
        ],
        out_shape=[
            jax.ShapeDtypeStruct((TOKENS, PROJ_WIDTH), BF16),
            jax.ShapeDtypeStruct((TOKENS, LANES), F32),
        ],
        scratch_shapes=[pltpu.VMEM((PROJ_TM, D_MODEL), BF16)],
        compiler_params=_cparams(("parallel", "arbitrary")),
        name="proj",
    )(x, gain, w_big, w_small)


CUMF_BLK = 128


def _cumf_kernel(sm_ref, bf_ref, cf_ref):
    r = lax.broadcasted_iota(jnp.int32, (CUMF_BLK, CUMF_BLK), 0)
    c = lax.broadcasted_iota(jnp.int32, (CUMF_BLK, CUMF_BLK), 1)
    tri = (r >= c).astype(F32)
    carry = jnp.zeros((1, LANES), F32)
    for blk in range(SEQ // CUMF_BLK):
        rows = slice(blk * CUMF_BLK, (blk + 1) * CUMF_BLK)
        log_f = -_softplus(-(sm_ref[rows, :] + bf_ref[...]))
        cum = _dot(tri, log_f, HI) + carry
        carry = cum[CUMF_BLK - 1:CUMF_BLK, :]
        cf_ref[0, rows, :] = cum


def _cumf(small, bias_row, layer):
    return pl.pallas_call(
        _cumf_kernel,
        grid=(BATCH,),
        in_specs=[
            pl.BlockSpec((SEQ, LANES), lambda b: (b, 0)),
            _of_layer(layer, (1, LANES)),
        ],
        out_specs=pl.BlockSpec((1, SEQ, LANES), lambda b: (b, 0, 0)),
        out_shape=jax.ShapeDtypeStruct((BATCH, SEQ, LANES), F32),
        compiler_params=_cparams(("parallel",)),
        name="cumf",
    )(small, bias_row)


AUG = FOX_HEAD_DIM
LOG2E = 1.4426950408889634


def _fox_selectors():
    sel = np.zeros((2, FOX_HEADS // 2, 4 * LANES, 2 * LANES), np.float32)
    ones = np.zeros((2, 1, 2 * LANES), np.float32)
    for hp in range(FOX_HEADS // 2):
        for e in range(2):
            base = e * LANES
            for d in range(FOX_HEAD_DIM):
                sel[:, hp, e * FOX_HEAD_DIM + d, base + d] = 1.0
            for part in range(3):
                src = (1 + part) * LANES + SM_FOXF + 2 * hp + e
                sel[0, hp, src, base + AUG + part] = 1.0
                sel[1, hp, src, base + AUG + 3 + part] = -1.0
    for e in range(2):
        ones[0, 0, e * LANES + AUG + 3:e * LANES + AUG + 6] = 1.0
        ones[1, 0, e * LANES + AUG:e * LANES + AUG + 3] = 1.0
    return jnp.asarray(sel, BF16), jnp.asarray(ones, F32)


def _fox_kernel(q_ref, qnext_ref, k_ref, v_ref, cfq_ref, cfqnext_ref, cfk_ref, qg_ref, kg_ref, sel_ref,
                ones_ref, o_ref, ka_ref, vt_ref, qa_ref, acc_ref, s_ref):
    qi = pl.program_id(1)
    lane = lax.broadcasted_iota(jnp.int32, (1, LANES), 1)
    first_head = lane < FOX_HEAD_DIM

    def headnorm(x, gain):
        x2 = x * x
        s0 = jnp.sum(jnp.where(first_head, x2, 0.0), axis=-1, keepdims=True)
        s1 = jnp.sum(jnp.where(first_head, 0.0, x2), axis=-1, keepdims=True)
        inv = jnp.where(first_head,
                        lax.rsqrt(s0 * (1.0 / FOX_HEAD_DIM) + EPS),
                        lax.rsqrt(s1 * (1.0 / FOX_HEAD_DIM) + EPS))
        return x * inv * gain

    def cum_parts(cf):
        return [part.astype(BF16) for part in _bf16_split3(cf * LOG2E)]

    def augmented(xn, parts, hp, side):
        packed = jnp.concatenate([xn.astype(BF16)] + parts, axis=1)
        return (_dot(packed, sel_ref[side, hp]) + ones_ref[side]).astype(BF16)

    hs = range(FOX_HEADS)

    def hrows(h):
        return slice(h * FOX_HEAD_DIM, (h + 1) * FOX_HEAD_DIM)

    def scores(h, j):
        rows = pl.ds(pl.multiple_of(j * FOX_TK, FOX_TK), FOX_TK)
        return _dot_nt(ka_ref[h, rows, :], qa_ref[h])

    def prepare_queries(src_ref, cf):
        parts = cum_parts(cf)
        for hp in range(FOX_HEADS // 2):
            pair = slice(hp * LANES, (hp + 1) * LANES)
            qn = headnorm(src_ref[:, pair].astype(F32), qg_ref[...]) * (FOX_HEAD_DIM ** -0.5 * LOG2E)
            qa = augmented(qn, parts, hp, 0)
            qa_ref[2 * hp] = qa[:, :LANES]
            qa_ref[2 * hp + 1] = qa[:, LANES:]

    @pl.when(qi == 0)
    def _():
        def body(c, carry):
            off = pl.multiple_of(c * FOX_TK, FOX_TK)
            rows = pl.ds(off, FOX_TK)
            parts = cum_parts(cfk_ref[0, rows, :])
            for hp in range(FOX_HEADS // 2):
                pair = slice(hp * LANES, (hp + 1) * LANES)
                kn = headnorm(k_ref[rows, pair].astype(F32), kg_ref[...])
                ka = augmented(kn, parts, hp, 1)
                ka_ref[2 * hp, rows, :] = ka[:, :LANES]
                ka_ref[2 * hp + 1, rows, :] = ka[:, LANES:]
                vt_ref[pair, rows] = v_ref[rows, pair].astype(F32).T.astype(BF16)
            return carry
        lax.fori_loop(0, SEQ // FOX_TK, body, 0)
        prepare_queries(q_ref, cfq_ref[0])
        for h in hs:
            s_ref[h] = scores(h, 0)

    acc_ref[...] = jnp.zeros_like(acc_ref)

    def absorb(s, m, l, h, krows, qcols):
        m_new = jnp.maximum(m, jnp.max(s, axis=0, keepdims=True))
        alpha = jnp.exp2(m - m_new)
        p = jnp.exp2(s - m_new)
        l_new = alpha * l + jnp.sum(p, axis=0, keepdims=True)
        pv = _dot(vt_ref[hrows(h), krows], p.astype(BF16))
        acc_ref[hrows(h), qcols] = alpha * acc_ref[hrows(h), qcols] + pv
        return m_new, l_new

    half = FOX_TQ // 2
    left, right = slice(0, half), slice(half, FOX_TQ)

    def step(j, nxt, carry, diagonal):
        k0 = pl.multiple_of(j * FOX_TK, FOX_TK)
        out = []
        for h in hs:
            (m_l, l_l), (m_r, l_r) = carry[h]
            if not diagonal:
                s_l, s_r = s_ref[h, :, left], s_ref[h, :, right]
                if nxt is not None:
                    s_ref[h] = scores(h, nxt)
                keys = pl.ds(k0, FOX_TK)
                out.append((absorb(s_l, m_l, l_l, h, keys, left), absorb(s_r, m_r, l_r, h, keys, right)))
                continue
            krow = lax.broadcasted_iota(jnp.int32, (half, half), 0)
            qcol = lax.broadcasted_iota(jnp.int32, (half, half), 1)
            causal = qcol >= krow
            early, late = pl.ds(k0, half), pl.ds(k0 + half, half)
            s_ll = jnp.where(causal, s_ref[h, left, left], -jnp.inf)
            s_lr = s_ref[h, left, right]
            s_rr = jnp.where(causal, s_ref[h, right, right], -jnp.inf)
            if nxt is not None:
                s_ref[h] = scores(h, nxt)
            out.append((absorb(s_ll, m_l, l_l, h, early, left),
                        absorb(s_rr, *absorb(s_lr, m_r, l_r, h, early, right), h, late, right)))
        return tuple(out)

    def fresh():
        return jnp.full((1, half), -jnp.inf, F32), jnp.zeros((1, half), F32)

    init = tuple((fresh(), fresh()) for _ in hs)
    carry = lax.fori_loop(0, qi, lambda j, c: step(j, j + 1, c, False), init)

    def finish(more_queries):
        if more_queries:
            prepare_queries(qnext_ref, cfqnext_ref[0])
        final = step(qi, 0 if more_queries else None, carry, True)
        sub = lax.broadcasted_iota(jnp.int32, (LANES, 1), 0)
        for hp in range(FOX_HEADS // 2):
            pair = slice(hp * LANES, (hp + 1) * LANES)
            for side, qcols in enumerate((left, right)):
                denom = jnp.where(sub < FOX_HEAD_DIM, final[2 * hp][side][1], final[2 * hp + 1][side][1])
                o_ref[qcols, pair] = (acc_ref[pair, qcols] / denom).T.astype(BF16)

    last_block = pl.num_programs(1) - 1
    pl.when(qi < last_block)(lambda: finish(True))
    pl.when(qi == last_block)(lambda: finish(False))


def _fox(proj, cf, q_gain2, k_gain2, layer):
    nq = SEQ // FOX_TQ
    fb = COL_FOX // BRANCH_WIDTH
    sel, ones = _fox_selectors()
    return pl.pallas_call(
        _fox_kernel,
        grid=(BATCH, nq),
        in_specs=[
            pl.BlockSpec((FOX_TQ, BRANCH_WIDTH), lambda b, qi: (b * nq + qi, fb)),
            pl.BlockSpec((FOX_TQ, BRANCH_WIDTH), lambda b, qi: (b * nq + jnp.minimum(qi + 1, nq - 1), fb)),
            pl.BlockSpec((SEQ, BRANCH_WIDTH), lambda b, qi: (b, fb + 1)),
            pl.BlockSpec((SEQ, BRANCH_WIDTH), lambda b, qi: (b, fb + 2)),
            pl.BlockSpec((1, FOX_TQ, LANES), lambda b, qi: (b, qi, 0)),
            pl.BlockSpec((1, FOX_TQ, LANES), lambda b, qi: (b, jnp.minimum(qi + 1, nq - 1), 0)),
            pl.BlockSpec((1, SEQ, LANES), lambda b, qi: (b, 0, 0)),
            _of_layer(layer, (1, LANES)),
            _of_layer(layer, (1, LANES)),
            pl.BlockSpec(sel.shape, lambda b, qi: (0, 0, 0, 0)),
            pl.BlockSpec(ones.shape, lambda b, qi: (0, 0, 0)),
        ],
        out_specs=pl.BlockSpec((FOX_TQ, BRANCH_WIDTH), lambda b, qi: (b * nq + qi, 0)),
        out_shape=jax.ShapeDtypeStruct((TOKENS, BRANCH_WIDTH), BF16),
        scratch_shapes=[
            pltpu.VMEM((FOX_HEADS, SEQ, LANES), BF16),
            pltpu.VMEM((BRANCH_WIDTH, SEQ), BF16),
            pltpu.VMEM((FOX_HEADS, FOX_TQ, LANES), BF16),
            pltpu.VMEM((BRANCH_WIDTH, FOX_TQ), F32),
            pltpu.VMEM((FOX_HEADS, FOX_TK, FOX_TQ), F32),
        ],
        compiler_params=_cparams(("parallel", "arbitrary")),
        name="fox",
    )(proj, proj, proj, proj, cf, cf, cf, q_gain2, k_gain2, sel, ones)


DN_STREAMS = 4

def _dn_kernel(q_ref, k_ref, v_ref, z_ref, sm_ref, wq_ref, wk_ref, wv_ref, alog_ref, dtb_ref,
               ng_ref, o_ref, u_ref, kcum_ref, qdec_ref, kdec_ref, qk_ref, gt_ref, st_ref):
    c = DN_CHUNK
    ri = lax.broadcasted_iota(jnp.int32, (c, c), 0)
    ci = lax.broadcasted_iota(jnp.int32, (c, c), 1)
    incl = ri >= ci
    strict = ri > ci
    eye = ri == ci
    eye_f = eye.astype(F32)
    tri3 = jnp.concatenate([jnp.where(incl, 1.0, 0.0)] * 3, axis=1).astype(BF16)
    lane = lax.broadcasted_iota(jnp.int32, (1, LANES), 1)

    n_taps = 4
    dr = lax.broadcasted_iota(jnp.int32, (n_taps * c, 2 * c), 0)
    dc = lax.broadcasted_iota(jnp.int32, (n_taps * c, 2 * c), 1)
    delays = jnp.where(dc == c + dr % c - dr // c, 1.0, 0.0).astype(BF16)

    def conv_silu(taps, w, cols):
        y = w[n_taps - 1:n_taps] * taps[0:c, cols]
        for j in range(1, n_taps):
            y = y + w[n_taps - 1 - j:n_taps - j] * taps[j * c:(j + 1) * c, cols]
        return _silu(y)

    def l2norm(x):
        return x * lax.rsqrt(jnp.sum(x * x, axis=-1, keepdims=True) + EPS)

    def heads():
        return [slice(h * DN_HEAD_DIM, (h + 1) * DN_HEAD_DIM) for h in range(DN_HEADS)]

    def rows_of(chunk):
        if isinstance(chunk, int):
            return chunk * c, max(chunk - 1, 0) * c
        return pl.multiple_of(chunk * c, c), pl.multiple_of(jnp.maximum(chunk - 1, 0) * c, c)

    def prepare(chunks):
        units = []
        raw, gcum_all, beta_all = [], [], []
        for chunk in chunks:
            r0, p0 = rows_of(chunk)
            rows = pl.ds(r0, c)
            prow = pl.ds(p0, c)
            keep_prev = jnp.where(chunk > 0, 1.0, 0.0).astype(BF16)
            taps = [_dot(delays, jnp.concatenate([ref[prow, :] * keep_prev, ref[rows, :]], axis=0))
                    for ref in (q_ref, k_ref, v_ref)]
            sm = sm_ref[rows, :]
            bsig = _sigmoid(sm)
            g_parts = _bf16_split3(-jnp.exp(alog_ref[...]) * _softplus(sm + dtb_ref[...]))
            gsum = _dot(tri3, jnp.concatenate(g_parts, axis=0).astype(BF16))
            for h, cols in enumerate(heads()):
                units.append((chunk, rows, h, cols))
                raw.append([(taps[a], w_ref[:, cols], cols)
                            for a, w_ref in enumerate((wq_ref, wk_ref, wv_ref))])
                gcum_all.append(gsum)
                beta_all.append(bsig)
        hs = range(len(units))
        gc, beta, q, k, v, decay = [], [], [], [], [], []
        for u in hs:
            head = units[u][2]
            g1 = jnp.sum(jnp.where(lane == SM_A + head, gcum_all[u], 0.0), axis=-1, keepdims=True)
            b1 = jnp.sum(jnp.where(lane == SM_BETA + head, beta_all[u], 0.0), axis=-1, keepdims=True)
            gc.append(jnp.broadcast_to(g1, (c, DN_HEAD_DIM)))
            beta.append(jnp.broadcast_to(b1, (c, DN_HEAD_DIM)))
            q.append(l2norm(conv_silu(*raw[u][0])) * (DN_HEAD_DIM ** -0.5))
            k.append(l2norm(conv_silu(*raw[u][1])))
            v.append(conv_silu(*raw[u][2]))
            gc_sq = gc[u][:, :c]
            g_row = jnp.sum(jnp.where(eye, gc_sq, 0.0), axis=0, keepdims=True)
            decay.append(jnp.exp(jnp.where(incl, gc_sq - g_row, -jnp.inf)))
        kb = [k[h] * beta[h] for h in hs]
        k16 = [k[h].astype(BF16) for h in hs]
        kk = [_dot_nt(kb[h].astype(BF16), k16[h]) for h in hs]
        qk = [_dot_nt(q[h].astype(BF16), k16[h]) for h in hs]
        yield
        neg = [jnp.where(strict, -(kk[h] * decay[h]), 0.0) for h in hs]
        qsum = [eye_f + neg[h] for h in hs]
        neg16 = [neg[h].astype(BF16) for h in hs]
        power = [_dot(neg16[h], neg16[h]) for h in hs]
        yield
        for _ in range(4):
            both = [_dot(power[h].astype(BF16),
                         jnp.concatenate([qsum[h], power[h]], axis=1).astype(BF16)) for h in hs]
            yield
            qsum = [qsum[h] + both[h][:, :c] for h in hs]
            power = [both[h][:, c:] for h in hs]
        last = [_dot(power[h].astype(BF16), qsum[h].astype(BF16)) for h in hs]
        yield
        eg = [jnp.exp(gc[h]) for h in hs]
        rhs = [jnp.concatenate([v[h] * beta[h], kb[h] * eg[h]], axis=1) for h in hs]
        sol = [rhs[h] + _dot((qsum[h] + last[h] - eye_f).astype(BF16), rhs[h].astype(BF16)) for h in hs]
        yield
        for u, (chunk, rows, head, cols) in enumerate(units):
            g_last = gc[u][c - 1:c, :]
            u_ref[rows, cols] = sol[u][:, :DN_HEAD_DIM]
            kcum_ref[rows, cols] = sol[u][:, DN_HEAD_DIM:].astype(BF16)
            qdec_ref[rows, cols] = (q[u] * eg[u]).astype(BF16)
            kdec_ref[rows, cols] = (k[u] * jnp.exp(g_last - gc[u])).astype(BF16)
            qk_ref[rows, head * c:(head + 1) * c] = jnp.where(incl, qk[u] * decay[u], 0.0).astype(BF16)
            gt_ref[chunk, :, cols] = jnp.broadcast_to(jnp.exp(g_last), (SUBLANES, DN_HEAD_DIM))

    def scan(chunks):
        hs = range(DN_HEADS)
        cols = heads()
        state = [st_ref[h] for h in hs]
        for chunk in chunks:
            rows = pl.ds(rows_of(chunk)[0], c)
            state16 = [state[h].astype(BF16) for h in hs]
            v16 = [(u_ref[rows, cols[h]] - _dot(kcum_ref[rows, cols[h]], state16[h])).astype(BF16)
                   for h in hs]
            yield
            upd = [_dot_tn(kdec_ref[rows, cols[h]], v16[h]) for h in hs]
            out = [_dot(qdec_ref[rows, cols[h]], state16[h])
                   + _dot(qk_ref[rows, h * c:(h + 1) * c], v16[h]) for h in hs]
            yield
            state = [state[h] * gt_ref[chunk, 0:1, cols[h]] + upd[h] for h in hs]
            for h in hs:
                o_ref[rows, cols[h]] = (_rmsnorm(out[h], ng_ref[...])
                                        * _silu(z_ref[rows, cols[h]].astype(F32))).astype(BF16)
        for h in hs:
            st_ref[h] = state[h]

    def interleave(*stages):
        active = list(stages)
        while active:
            for gen in list(active):
                if next(gen, StopIteration) is StopIteration:
                    active.remove(gen)

    def pair(n):
        return [DN_STREAMS * n + t for t in range(DN_STREAMS)]

    n_iter = N_CHUNKS // DN_STREAMS
    st_ref[...] = jnp.zeros_like(st_ref)
    interleave(prepare(pair(0)))

    def body(n, carry):
        interleave(scan(pair(n - 1)), prepare(pair(n)))
        return carry

    lax.fori_loop(1, n_iter, body, 0)
    interleave(scan(pair(n_iter - 1)))


def _deltanet(proj, small, conv_w, alog_row, dtb_row, norm_gain, layer):
    qb = COL_DN // BRANCH_WIDTH
    zb = COL_DNZ // BRANCH_WIDTH

    def colspec(blk):
        return pl.BlockSpec((SEQ, BRANCH_WIDTH), lambda b: (b, blk))

    def wspec(blk):
        return _of_layer(layer, (4, BRANCH_WIDTH), lambda b: (0, blk))

    row = _of_layer(layer, (1, LANES))
    return pl.pallas_call(
        _dn_kernel,
        grid=(BATCH,),
        in_specs=[
            colspec(qb), colspec(qb + 1), colspec(qb + 2), colspec(zb),
            pl.BlockSpec((SEQ, LANES), lambda b: (b, 0)),
            wspec(0), wspec(1), wspec(2),
            row, row, row,
        ],
        out_specs=pl.BlockSpec((SEQ, BRANCH_WIDTH), lambda b: (b, 0)),
        out_shape=jax.ShapeDtypeStruct((TOKENS, BRANCH_WIDTH), BF16),
        scratch_shapes=[
            pltpu.VMEM((SEQ, BRANCH_WIDTH), F32),
            pltpu.VMEM((SEQ, BRANCH_WIDTH), BF16),
            pltpu.VMEM((SEQ, BRANCH_WIDTH), BF16),
            pltpu.VMEM((SEQ, BRANCH_WIDTH), BF16),
            pltpu.VMEM((SEQ, DN_HEADS * DN_CHUNK), BF16),
            pltpu.VMEM((N_CHUNKS, SUBLANES, BRANCH_WIDTH), F32),
            pltpu.VMEM((DN_HEADS, DN_HEAD_DIM, DN_HEAD_DIM), F32),
        ],
        compiler_params=_cparams(("parallel",)),
        name="deltanet",
    )(proj, proj, proj, proj, small, conv_w, conv_w, conv_w, alog_row, dtb_row, norm_gain)


def _merge_kernel(yf_ref, sb_ref, sc_ref, sv_ref, scp_ref, svp_ref, yd_ref, g0_ref, g1_ref, g2_ref,
                  x_ref, wb_ref, wo_ref, cw_ref, o_ref):
    i = pl.program_id(0)
    keep_prev = jnp.where(i % (SEQ // MERGE_TM) != 0, 1.0, 0.0)
    cur = sc_ref[...].astype(F32) * sv_ref[...].astype(F32)
    prev = scp_ref[...].astype(F32) * svp_ref[...].astype(F32) * keep_prev
    cat = jnp.concatenate([prev, cur], axis=0)
    cw = cw_ref[...]
    conv = cw[2:3] * cur + cw[1:2] * _shifted(cat, 1) + cw[0:1] * _shifted(cat, 2)
    y_sc = (sb_ref[...].astype(F32) * conv).astype(BF16)
    sub = MERGE_TM // MERGE_SPLIT
    blocks = [slice(r * sub, (r + 1) * sub) for r in range(MERGE_SPLIT)]
    branch = [(_dot(yf_ref[rows, :], wb_ref[0]), _dot(y_sc[rows], wb_ref[1]), _dot(yd_ref[rows, :], wb_ref[2]))
              for rows in blocks]
    for rows, (d0, d1, d2) in zip(blocks, branch):
        merged = (_sigmoid(g0_ref[rows, :].astype(F32)) * d0 + _sigmoid(g1_ref[rows, :].astype(F32)) * d1
                  + _sigmoid(g2_ref[rows, :].astype(F32)) * d2)
        o_ref[rows, :] = x_ref[rows, :] + _dot(merged.astype(BF16), wo_ref[...])


def _merge(x, proj, y_fox, y_dn, w_branch, w_o, sc_conv_w, layer):
    tm = MERGE_TM
    scb = COL_SC // BRANCH_WIDTH
    gb = COL_GATE // D_MODEL
    hpb = tm // HALO

    def prev_rows(i):
        return jnp.maximum(i * hpb - 1, 0)

    return pl.pallas_call(
        _merge_kernel,
        grid=(TOKENS // tm,),
        in_specs=[
            pl.BlockSpec((tm, BRANCH_WIDTH), lambda i: (i, 0)),
            pl.BlockSpec((tm, BRANCH_WIDTH), lambda i: (i, scb)),
            pl.BlockSpec((tm, BRANCH_WIDTH), lambda i: (i, scb + 1)),
            pl.BlockSpec((tm, BRANCH_WIDTH), lambda i: (i, scb + 2)),
            pl.BlockSpec((HALO, BRANCH_WIDTH), lambda i: (prev_rows(i), scb + 1)),
            pl.BlockSpec((HALO, BRANCH_WIDTH), lambda i: (prev_rows(i), scb + 2)),
            pl.BlockSpec((tm, BRANCH_WIDTH), lambda i: (i, 0)),
            pl.BlockSpec((tm, D_MODEL), lambda i: (i, gb)),
            pl.BlockSpec((tm, D_MODEL), lambda i: (i, gb + 1)),
            pl.BlockSpec((tm, D_MODEL), lambda i: (i, gb + 2)),
            pl.BlockSpec((tm, D_MODEL), lambda i: (i, 0)),
            _of_layer(layer, (3, BRANCH_WIDTH, D_MODEL), resident=True),
            _of_layer(layer, (D_MODEL, D_MODEL), resident=True),
            _of_layer(layer, (3, BRANCH_WIDTH)),
        ],
        out_specs=pl.BlockSpec((tm, D_MODEL), lambda i: (i, 0)),
        out_shape=jax.ShapeDtypeStruct((TOKENS, D_MODEL), F32),
        compiler_params=_cparams(("parallel",)),
        name="merge",
    )(y_fox, proj, proj, proj, proj, proj, y_dn, proj, proj, proj, x, w_branch, w_o, sc_conv_w)


def _ffn_kernel(x_ref, xp_ref, gf_ref, wup_ref, cw_ref, wd_ref, gp_ref, wpg_ref, p_ref, wple_ref, o_ref,
                act_ref):
    i = pl.program_id(0)
    keep_prev = jnp.where(i % (SEQ // FFN_TM) != 0, 1.0, 0.0)
    xx = jnp.concatenate([xp_ref[...] * keep_prev, x_ref[...]], axis=0)
    hn = _rmsnorm(xx, gf_ref[...]).astype(BF16)

    def branch(cols):
        u = _dot(hn, wup_ref[:, cols])
        cw = cw_ref[:, cols]
        return cw[2:3] * u[HALO:] + cw[1:2] * _shifted(u, 1) + cw[0:1] * _shifted(u, 2)

    for j in range(FFN_NF):
        gate = branch(slice(j * FFN_TF, (j + 1) * FFN_TF))
        val = branch(slice(D_FF + j * FFN_TF, D_FF + (j + 1) * FFN_TF))
        act_ref[:, j * FFN_TF:(j + 1) * FFN_TF] = (_silu(gate) * val).astype(BF16)

    x1 = x_ref[...] + _dot(act_ref[...], wd_ref[...])
    h2 = _rmsnorm(x1, gp_ref[...]).astype(BF16)
    pgate = _sigmoid(_dot(h2, wpg_ref[...]))
    emb = _dot(p_ref[...].astype(BF16), wple_ref[...])
    o_ref[...] = x1 + pgate * emb


def _ffn(x, p, g_ffn, w_up, conv_w, w_down, g_ple, w_pg, w_ple, layer):
    tm = FFN_TM
    hpb = tm // HALO

    def whole(shape):
        return _of_layer(layer, shape, resident=True)

    return pl.pallas_call(
        _ffn_kernel,
        grid=(TOKENS // tm,),
        in_specs=[
            pl.BlockSpec((tm, D_MODEL), lambda i: (i, 0)),
            pl.BlockSpec((HALO, D_MODEL), lambda i: (jnp.maximum(i * hpb - 1, 0), 0)),
            whole((1, D_MODEL)),
            whole((D_MODEL, 2 * D_FF)),
            whole((3, 2 * D_FF)),
            whole((D_FF, D_MODEL)),
            whole((1, D_MODEL)),
            whole((D_MODEL, D_MODEL)),
            _of_layer(layer, (tm, PLE_DIM), lambda i: (i, 0)),
            whole((PLE_DIM, D_MODEL)),
        ],
        out_specs=pl.BlockSpec((tm, D_MODEL), lambda i: (i, 0)),
        out_shape=jax.ShapeDtypeStruct((TOKENS, D_MODEL), F32),
        scratch_shapes=[pltpu.VMEM((tm, D_FF), BF16)],
        compiler_params=_cparams(("parallel",)),
        name="ffn",
    )(x, x, g_ffn, w_up, conv_w, w_down, g_ple, w_pg, p, w_ple)


def _lane_rows(values, offset):
    depth, n = values.shape
    return jnp.pad(values.astype(F32), ((0, 0), (offset, LANES - offset - n)))[:, None, :]


def kernel(x, p, g_mix, w_in, b_fox_f, fox_q_gain, fox_k_gain, sc_conv_w, dn_conv_w, dn_a_log,
           dn_dt_bias, dn_norm_gain, w_branch, w_o, g_ffn, w_up, ffn_conv_w, w_down, g_ple,
           w_ple_gate, w_ple):
    depth = p.shape[0]
    x = x.reshape(TOKENS, D_MODEL)
    p = p.reshape(depth, TOKENS, PLE_DIM)

    w_big, w_small = _regroup(w_in)
    w_branch16, w_o16, w_up16, w_down16 = (w.astype(BF16) for w in (w_branch, w_o, w_up, w_down))
    w_pg16, w_ple16 = w_ple_gate.astype(BF16), w_ple.astype(BF16)
    g_mix3, g_ffn3, g_ple3, dn_gain3 = (g[:, None, :] for g in (g_mix, g_ffn, g_ple, dn_norm_gain))
    q_gain3 = jnp.tile(fox_q_gain, (1, 2))[:, None, :]
    k_gain3 = jnp.tile(fox_k_gain, (1, 2))[:, None, :]
    fox_bias = _lane_rows(b_fox_f, SM_FOXF)
    a_log = _lane_rows(dn_a_log, SM_A)
    dt_bias = _lane_rows(dn_dt_bias, SM_A)

    for layer in range(depth):
        proj, small = _proj(x, g_mix3, w_big, w_small, layer)
        cf = _cumf(small, fox_bias, layer)
        y_fox = _fox(proj, cf, q_gain3, k_gain3, layer)
        y_dn = _deltanet(proj, small, dn_conv_w, a_log, dt_bias, dn_gain3, layer)
        x = _merge(x, proj, y_fox, y_dn, w_branch16, w_o16, sc_conv_w, layer)
        x = _ffn(x, p, g_ffn3, w_up16, ffn_conv_w, w_down16, g_ple3, w_pg16, w_ple16, layer)
    return x.reshape(BATCH, SEQ, D_MODEL)
```

```python
import jax
import jax.numpy as jnp
import numpy as np
from jax import lax
from jax.experimental import pallas as pl
from jax.experimental.pallas import tpu as pltpu

F32 = jnp.float32
BF16 = jnp.bfloat16

D_MODEL = 1024
BATCH = 8
SEQ = 2048
TOKENS = BATCH * SEQ
PLE_DIM = 256
BRANCH_WIDTH = 512
FOX_HEADS = 8
FOX_HEAD_DIM = 64
DN_HEADS = 4
DN_HEAD_DIM = 128
DN_CHUNK = 64
N_CHUNKS = SEQ // DN_CHUNK
D_FF = 2816
EPS = 1e-6

LANES = 128
SUBLANES = 8
HALO = 16

PROJ_WIDTH = 8192
COL_FOX = 0
COL_SC = 1536
COL_DN = 3072
COL_DNZ = 4608
COL_GATE = 5120
SM_FOXF = 0
SM_BETA = 8
SM_A = 12

PROJ_TM = 1024
PROJ_TN = 2048
FOX_TQ = 512
FOX_TK = 512
MERGE_TM = 512
MERGE_SPLIT = 2
FFN_TM = 1024
FFN_TF = 256
FFN_NF = D_FF // FFN_TF

VMEM_LIMIT = 48 * 1024 * 1024


def _cparams(sem):
    return pltpu.CompilerParams(dimension_semantics=sem, vmem_limit_bytes=VMEM_LIMIT)


def _sigmoid(x):
    return 1.0 / (1.0 + jnp.exp(-x))


def _silu(x):
    half = 0.5 * x
    return half + half * jnp.tanh(half)


def _softplus(x):
    return jnp.maximum(x, 0.0) + jnp.log(1.0 + jnp.exp(-jnp.abs(x)))


def _dot(a, b, precision=None):
    return jnp.dot(a, b, preferred_element_type=F32, precision=precision)


def _dot_nt(a, b):
    return lax.dot_general(a, b, (((1,), (1,)), ((), ())), preferred_element_type=F32)


def _dot_tn(a, b):
    return lax.dot_general(a, b, (((0,), (0,)), ((), ())), preferred_element_type=F32)


def _rmsnorm(x, gain):
    ms = jnp.mean(x * x, axis=-1, keepdims=True)
    return x * lax.rsqrt(ms + EPS) * gain


def _shifted(cat, shift):
    if shift == 0:
        return cat[HALO:]
    return pltpu.roll(cat, shift, 0)[HALO:]


def _bf16_split3(x):
    hi = x.astype(BF16).astype(F32)
    r = x - hi
    mid = r.astype(BF16).astype(F32)
    lo = (r - mid).astype(BF16).astype(F32)
    return hi, mid, lo


IN_F = 3 * BRANCH_WIDTH
IN_SC = IN_F + FOX_HEADS
IN_DN = IN_SC + 3 * BRANCH_WIDTH
IN_B = IN_DN + 3 * BRANCH_WIDTH
IN_Z = IN_B + 2 * DN_HEADS
IN_G = IN_Z + BRANCH_WIDTH
IN_WIDTH = IN_G + 3 * D_MODEL
REGROUP_ROWS = 256
assert IN_F % LANES == SM_FOXF and SM_FOXF + FOX_HEADS == SM_BETA
assert IN_B % LANES == SM_BETA and SM_BETA + DN_HEADS == SM_A


def _regroup_kernel(w_ref, big_ref, small_ref):
    col = 0
    for a, b in ((0, IN_F), (IN_SC, IN_DN), (IN_DN, IN_B), (IN_Z, IN_G), (IN_G, IN_WIDTH)):
        big_ref[:, col:col + (b - a)] = w_ref[:, a:b].astype(BF16)
        col += b - a
    lane = lax.broadcasted_iota(jnp.int32, (1, LANES), 1)
    f_tile = w_ref[:, IN_F - SM_FOXF:IN_F - SM_FOXF + LANES]
    b_tile = w_ref[:, IN_B - SM_BETA:IN_B - SM_BETA + LANES]
    small_ref[...] = jnp.where(lane < SM_BETA, f_tile,
                               jnp.where(lane < SM_A + DN_HEADS, b_tile, 0.0)).astype(BF16)


def _regroup(w_in):
    depth = w_in.shape[0]
    steps = D_MODEL // REGROUP_ROWS
    return pl.pallas_call(
        _regroup_kernel,
        grid=(depth, steps),
        in_specs=[pl.BlockSpec((REGROUP_ROWS, IN_WIDTH), lambda l, i: (l * steps + i, 0))],
        out_specs=[
            pl.BlockSpec((None, REGROUP_ROWS, PROJ_WIDTH), lambda l, i: (l, i, 0)),
            pl.BlockSpec((None, REGROUP_ROWS, LANES), lambda l, i: (l, i, 0)),
        ],
        out_shape=[
            jax.ShapeDtypeStruct((depth, D_MODEL, PROJ_WIDTH), BF16),
            jax.ShapeDtypeStruct((depth, D_MODEL, LANES), BF16),
        ],
        compiler_params=_cparams(("parallel", "parallel")),
        name="regroup",
    )(w_in.reshape(depth * D_MODEL, IN_WIDTH))


def _proj_kernel(x_ref, g_ref, w_ref, ws_ref, o_ref, os_ref, hn_ref):
    @pl.when(pl.program_id(1) == 0)
    def _():
        hn = _rmsnorm(x_ref[...], g_ref[...]).astype(BF16)
        hn_ref[...] = hn
        os_ref[...] = _dot(hn, ws_ref[...])

    o_ref[...] = _dot(hn_ref[...], w_ref[...]).astype(BF16)


def _of_layer(layer, block, index_map=None, resident=False):
    if index_map is None:
        index_map = lambda *grid: (0,) * len(block)
    mode = dict(pipeline_mode=pl.Buffered(1)) if resident else {}
    return pl.BlockSpec((None,) + tuple(block), lambda *grid: (layer,) + tuple(index_map(*grid)), **mode)


def _proj(x, gain, w_big, w_small, layer):
    grid = (TOKENS // PROJ_TM, PROJ_WIDTH // PROJ_TN)
    return pl.pallas_call(
        _proj_kernel,
        grid=grid,
        in_specs=[
            pl.BlockSpec((PROJ_TM, D_MODEL), lambda i, j: (i, 0)),
            _of_layer(layer, (1, D_MODEL)),
            _of_layer(layer, (D_MODEL, PROJ_TN), lambda i, j: (0, j)),
            _of_layer(layer, (D_MODEL, LANES)),
        ],
        out_specs=[
            pl.BlockSpec((PROJ_TM, PROJ_TN), lambda i, j: (i, j)),
            pl.BlockSpec((PROJ_TM, LANES), lambda i, j: (i, 0)),
        ],
        out_shape=[
            jax.ShapeDtypeStruct((TOKENS, PROJ_WIDTH), BF16),
            jax.ShapeDtypeStruct((TOKENS, LANES), F32),
        ],
        scratch_shapes=[pltpu.VMEM((PROJ_TM, D_MODEL), BF16)],
        compiler_params=_cparams(("parallel", "arbitrary")),
        name="proj",
    )(x, gain, w_big, w_small)


CUMF_BLK = 128


def _cumf_kernel(sm_ref, bf_ref, cf_ref):
    r = lax.broadcasted_iota(jnp.int32, (CUMF_BLK, CUMF_BLK), 0)
    c = lax.broadcasted_iota(jnp.int32, (CUMF_BLK, CUMF_BLK), 1)
    tri3 = jnp.concatenate([jnp.where(r >= c, 1.0, 0.0)] * 3, axis=1).astype(BF16)
    carry = jnp.zeros((1, LANES), F32)
    for blk in range(SEQ // CUMF_BLK):
        rows = slice(blk * CUMF_BLK, (blk + 1) * CUMF_BLK)
        log_f = -_softplus(-(sm_ref[rows, :] + bf_ref[...]))
        parts = jnp.concatenate(_bf16_split3(log_f), axis=0).astype(BF16)
        cum = _dot(tri3, parts) + carry
        carry = cum[CUMF_BLK - 1:CUMF_BLK, :]
        cf_ref[0, rows, :] = cum


def _cumf(small, bias_row, layer):
    return pl.pallas_call(
        _cumf_kernel,
        grid=(BATCH,),
        in_specs=[
            pl.BlockSpec((SEQ, LANES), lambda b: (b, 0)),
            _of_layer(layer, (1, LANES)),
        ],
        out_specs=pl.BlockSpec((1, SEQ, LANES), lambda b: (b, 0, 0)),
        out_shape=jax.ShapeDtypeStruct((BATCH, SEQ, LANES), F32),
        compiler_params=_cparams(("parallel",)),
        name="cumf",
    )(small, bias_row)


AUG = FOX_HEAD_DIM
LOG2E = 1.4426950408889634


def _fox_selectors():
    sel = np.zeros((2, FOX_HEADS // 2, 4 * LANES, 2 * LANES), np.float32)
    ones = np.zeros((2, 1, 2 * LANES), np.float32)
    for hp in range(FOX_HEADS // 2):
        for e in range(2):
            base = e * LANES
            for d in range(FOX_HEAD_DIM):
                sel[:, hp, e * FOX_HEAD_DIM + d, base + d] = 1.0
            for part in range(3):
                src = (1 + part) * LANES + SM_FOXF + 2 * hp + e
                sel[0, hp, src, base + AUG + part] = 1.0
                sel[1, hp, src, base + AUG + 3 + part] = -1.0
    for e in range(2):
        ones[0, 0, e * LANES + AUG + 3:e * LANES + AUG + 6] = 1.0
        ones[1, 0, e * LANES + AUG:e * LANES + AUG + 3] = 1.0
    return jnp.asarray(sel, BF16), jnp.asarray(ones, F32)


def _fox_kernel(q_ref, qnext_ref, k_ref, v_ref, cfq_ref, cfqnext_ref, cfk_ref, qg_ref, kg_ref, sel_ref,
                ones_ref, o_ref, ka_ref, vt_ref, qa_ref, acc_ref, s_ref):
    qi = pl.program_id(1)
    lane = lax.broadcasted_iota(jnp.int32, (1, LANES), 1)
    first_head = lane < FOX_HEAD_DIM

    def headnorm(x, gain):
        x2 = x * x
        s0 = jnp.sum(jnp.where(first_head, x2, 0.0), axis=-1, keepdims=True)
        s1 = jnp.sum(jnp.where(first_head, 0.0, x2), axis=-1, keepdims=True)
        inv = jnp.where(first_head,
                        lax.rsqrt(s0 * (1.0 / FOX_HEAD_DIM) + EPS),
                        lax.rsqrt(s1 * (1.0 / FOX_HEAD_DIM) + EPS))
        return x * inv * gain

    def cum_parts(cf):
        return [part.astype(BF16) for part in _bf16_split3(cf * LOG2E)]

    def augmented(xn, parts, hp, side):
        packed = jnp.concatenate([xn.astype(BF16)] + parts, axis=1)
        return (_dot(packed, sel_ref[side, hp]) + ones_ref[side]).astype(BF16)

    hs = range(FOX_HEADS)

    def hrows(h):
        return slice(h * FOX_HEAD_DIM, (h + 1) * FOX_HEAD_DIM)

    def scores(h, j):
        rows = pl.ds(pl.multiple_of(j * FOX_TK, FOX_TK), FOX_TK)
        return _dot_nt(ka_ref[h, rows, :], qa_ref[h])

    def prepare_queries(src_ref, cf):
        parts = cum_parts(cf)
        for hp in range(FOX_HEADS // 2):
            pair = slice(hp * LANES, (hp + 1) * LANES)
            qn = headnorm(src_ref[:, pair].astype(F32), qg_ref[...]) * (FOX_HEAD_DIM ** -0.5 * LOG2E)
            qa = augmented(qn, parts, hp, 0)
            qa_ref[2 * hp] = qa[:, :LANES]
            qa_ref[2 * hp + 1] = qa[:, LANES:]

    @pl.when(qi == 0)
    def _():
        def body(c, carry):
            off = pl.multiple_of(c * FOX_TK, FOX_TK)
            rows = pl.ds(off, FOX_TK)
            parts = cum_parts(cfk_ref[0, rows, :])
            for hp in range(FOX_HEADS // 2):
                pair = slice(hp * LANES, (hp + 1) * LANES)
                kn = headnorm(k_ref[rows, pair].astype(F32), kg_ref[...])
                ka = augmented(kn, parts, hp, 1)
                ka_ref[2 * hp, rows, :] = ka[:, :LANES]
                ka_ref[2 * hp + 1, rows, :] = ka[:, LANES:]
                vt_ref[pair, rows] = v_ref[rows, pair].astype(F32).T.astype(BF16)
            return carry
        lax.fori_loop(0, SEQ // FOX_TK, body, 0)
        prepare_queries(q_ref, cfq_ref[0])
        for h in hs:
            s_ref[h] = scores(h, 0)

    acc_ref[...] = jnp.zeros_like(acc_ref)

    def absorb(s, m, l, h, krows, qcols):
        m_new = jnp.maximum(m, jnp.max(s, axis=0, keepdims=True))
        alpha = jnp.exp2(m - m_new)
        p = jnp.exp2(s - m_new)
        l_new = alpha * l + jnp.sum(p, axis=0, keepdims=True)
        pv = _dot(vt_ref[hrows(h), krows], p.astype(BF16))
        acc_ref[hrows(h), qcols] = alpha * acc_ref[hrows(h), qcols] + pv
        return m_new, l_new

    half = FOX_TQ // 2
    left, right = slice(0, half), slice(half, FOX_TQ)

    def step(j, nxt, carry, diagonal):
        k0 = pl.multiple_of(j * FOX_TK, FOX_TK)
        out = []
        for h in hs:
            (m_l, l_l), (m_r, l_r) = carry[h]
            if not diagonal:
                s_l, s_r = s_ref[h, :, left], s_ref[h, :, right]
                if nxt is not None:
                    s_ref[h] = scores(h, nxt)
                keys = pl.ds(k0, FOX_TK)
                out.append((absorb(s_l, m_l, l_l, h, keys, left), absorb(s_r, m_r, l_r, h, keys, right)))
                continue
            krow = lax.broadcasted_iota(jnp.int32, (half, half), 0)
            qcol = lax.broadcasted_iota(jnp.int32, (half, half), 1)
            causal = qcol >= krow
            early, late = pl.ds(k0, half), pl.ds(k0 + half, half)
            s_ll = jnp.where(causal, s_ref[h, left, left], -jnp.inf)
            s_lr = s_ref[h, left, right]
            s_rr = jnp.where(causal, s_ref[h, right, right], -jnp.inf)
            if nxt is not None:
                s_ref[h] = scores(h, nxt)
            out.append((absorb(s_ll, m_l, l_l, h, early, left),
                        absorb(s_rr, *absorb(s_lr, m_r, l_r, h, early, right), h, late, right)))
        return tuple(out)

    def fresh():
        return jnp.full((1, half), -jnp.inf, F32), jnp.zeros((1, half), F32)

    init = tuple((fresh(), fresh()) for _ in hs)
    carry = lax.fori_loop(0, qi, lambda j, c: step(j, j + 1, c, False), init)

    def finish(more_queries):
        if more_queries:
            prepare_queries(qnext_ref, cfqnext_ref[0])
        final = step(qi, 0 if more_queries else None, carry, True)
        sub = lax.broadcasted_iota(jnp.int32, (LANES, 1), 0)
        for hp in range(FOX_HEADS // 2):
            pair = slice(hp * LANES, (hp + 1) * LANES)
            for side, qcols in enumerate((left, right)):
                denom = jnp.where(sub < FOX_HEAD_DIM, final[2 * hp][side][1], final[2 * hp + 1][side][1])
                o_ref[qcols, pair] = (acc_ref[pair, qcols] / denom).T.astype(BF16)

    last_block = pl.num_programs(1) - 1
    pl.when(qi < last_block)(lambda: finish(True))
    pl.when(qi == last_block)(lambda: finish(False))


def _fox(proj, cf, q_gain2, k_gain2, layer):
    nq = SEQ // FOX_TQ
    fb = COL_FOX // BRANCH_WIDTH
    sel, ones = _fox_selectors()
    return pl.pallas_call(
        _fox_kernel,
        grid=(BATCH, nq),
        in_specs=[
            pl.BlockSpec((FOX_TQ, BRANCH_WIDTH), lambda b, qi: (b * nq + qi, fb)),
            pl.BlockSpec((FOX_TQ, BRANCH_WIDTH), lambda b, qi: (b * nq + jnp.minimum(qi + 1, nq - 1), fb)),
            pl.BlockSpec((SEQ, BRANCH_WIDTH), lambda b, qi: (b, fb + 1)),
            pl.BlockSpec((SEQ, BRANCH_WIDTH), lambda b, qi: (b, fb + 2)),
            pl.BlockSpec((1, FOX_TQ, LANES), lambda b, qi: (b, qi, 0)),
            pl.BlockSpec((1, FOX_TQ, LANES), lambda b, qi: (b, jnp.minimum(qi + 1, nq - 1), 0)),
            pl.BlockSpec((1, SEQ, LANES), lambda b, qi: (b, 0, 0)),
            _of_layer(layer, (1, LANES)),
            _of_layer(layer, (1, LANES)),
            pl.BlockSpec(sel.shape, lambda b, qi: (0, 0, 0, 0)),
            pl.BlockSpec(ones.shape, lambda b, qi: (0, 0, 0)),
        ],
        out_specs=pl.BlockSpec((FOX_TQ, BRANCH_WIDTH), lambda b, qi: (b * nq + qi, 0)),
        out_shape=jax.ShapeDtypeStruct((TOKENS, BRANCH_WIDTH), BF16),
        scratch_shapes=[
            pltpu.VMEM((FOX_HEADS, SEQ, LANES), BF16),
            pltpu.VMEM((BRANCH_WIDTH, SEQ), BF16),
            pltpu.VMEM((FOX_HEADS, FOX_TQ, LANES), BF16),
            pltpu.VMEM((BRANCH_WIDTH, FOX_TQ), F32),
            pltpu.VMEM((FOX_HEADS, FOX_TK, FOX_TQ), F32),
        ],
        compiler_params=_cparams(("parallel", "arbitrary")),
        name="fox",
    )(proj, proj, proj, proj, cf, cf, cf, q_gain2, k_gain2, sel, ones)


DN_STREAMS = 4

def _dn_kernel(q_ref, k_ref, v_ref, z_ref, sm_ref, wq_ref, wk_ref, wv_ref, alog_ref, dtb_ref,
               ng_ref, o_ref, u_ref, kcum_ref, qdec_ref, kdec_ref, qk_ref, gt_ref, st_ref):
    c = DN_CHUNK
    ri = lax.broadcasted_iota(jnp.int32, (c, c), 0)
    ci = lax.broadcasted_iota(jnp.int32, (c, c), 1)
    incl = ri >= ci
    strict = ri > ci
    eye = ri == ci
    eye_f = eye.astype(F32)
    tri3 = jnp.concatenate([jnp.where(incl, 1.0, 0.0)] * 3, axis=1).astype(BF16)
    lane = lax.broadcasted_iota(jnp.int32, (1, LANES), 1)

    n_taps = 4
    dr = lax.broadcasted_iota(jnp.int32, (n_taps * c, 2 * c), 0)
    dc = lax.broadcasted_iota(jnp.int32, (n_taps * c, 2 * c), 1)
    delays = jnp.where(dc == c + dr % c - dr // c, 1.0, 0.0).astype(BF16)

    def conv_silu(taps, w, cols):
        y = w[n_taps - 1:n_taps] * taps[0:c, cols]
        for j in range(1, n_taps):
            y = y + w[n_taps - 1 - j:n_taps - j] * taps[j * c:(j + 1) * c, cols]
        return _silu(y)

    def l2norm(x):
        return x * lax.rsqrt(jnp.sum(x * x, axis=-1, keepdims=True) + EPS)

    def heads():
        return [slice(h * DN_HEAD_DIM, (h + 1) * DN_HEAD_DIM) for h in range(DN_HEADS)]

    def rows_of(chunk):
        if isinstance(chunk, int):
            return chunk * c, max(chunk - 1, 0) * c
        return pl.multiple_of(chunk * c, c), pl.multiple_of(jnp.maximum(chunk - 1, 0) * c, c)

    def prepare(chunks):
        units = []
        raw, gcum_all, beta_all = [], [], []
        for chunk in chunks:
            r0, p0 = rows_of(chunk)
            rows = pl.ds(r0, c)
            prow = pl.ds(p0, c)
            keep_prev = jnp.where(chunk > 0, 1.0, 0.0).astype(BF16)
            taps = [_dot(delays, jnp.concatenate([ref[prow, :] * keep_prev, ref[rows, :]], axis=0))
                    for ref in (q_ref, k_ref, v_ref)]
            sm = sm_ref[rows, :]
            bsig = _sigmoid(sm)
            g_parts = _bf16_split3(-jnp.exp(alog_ref[...]) * _softplus(sm + dtb_ref[...]))
            gsum = _dot(tri3, jnp.concatenate(g_parts, axis=0).astype(BF16))
            for h, cols in enumerate(heads()):
                units.append((chunk, rows, h, cols))
                raw.append([(taps[a], w_ref[:, cols], cols)
                            for a, w_ref in enumerate((wq_ref, wk_ref, wv_ref))])
                gcum_all.append(gsum)
                beta_all.append(bsig)
        hs = range(len(units))
        gc, beta, q, k, v, decay = [], [], [], [], [], []
        for u in hs:
            head = units[u][2]
            g1 = jnp.sum(jnp.where(lane == SM_A + head, gcum_all[u], 0.0), axis=-1, keepdims=True)
            b1 = jnp.sum(jnp.where(lane == SM_BETA + head, beta_all[u], 0.0), axis=-1, keepdims=True)
            gc.append(jnp.broadcast_to(g1, (c, DN_HEAD_DIM)))
            beta.append(jnp.broadcast_to(b1, (c, DN_HEAD_DIM)))
            q.append(l2norm(conv_silu(*raw[u][0])) * (DN_HEAD_DIM ** -0.5))
            k.append(l2norm(conv_silu(*raw[u][1])))
            v.append(conv_silu(*raw[u][2]))
            gc_sq = gc[u][:, :c]
            g_row = jnp.sum(jnp.where(eye, gc_sq, 0.0), axis=0, keepdims=True)
            decay.append(jnp.exp(jnp.where(incl, gc_sq - g_row, -jnp.inf)))
        kb = [k[h] * beta[h] for h in hs]
        k16 = [k[h].astype(BF16) for h in hs]
        kk = [_dot_nt(kb[h].astype(BF16), k16[h]) for h in hs]
        qk = [_dot_nt(q[h].astype(BF16), k16[h]) for h in hs]
        yield
        neg = [jnp.where(strict, -(kk[h] * decay[h]), 0.0) for h in hs]
        qsum = [eye_f + neg[h] for h in hs]
        neg16 = [neg[h].astype(BF16) for h in hs]
        power = [_dot(neg16[h], neg16[h]) for h in hs]
        yield
        for _ in range(4):
            both = [_dot(power[h].astype(BF16),
                         jnp.concatenate([qsum[h], power[h]], axis=1).astype(BF16)) for h in hs]
            yield
            qsum = [qsum[h] + both[h][:, :c] for h in hs]
            power = [both[h][:, c:] for h in hs]
        last = [_dot(power[h].astype(BF16), qsum[h].astype(BF16)) for h in hs]
        yield
        eg = [jnp.exp(gc[h]) for h in hs]
        rhs = [jnp.concatenate([v[h] * beta[h], kb[h] * eg[h]], axis=1) for h in hs]
        sol = [rhs[h] + _dot((qsum[h] + last[h] - eye_f).astype(BF16), rhs[h].astype(BF16)) for h in hs]
        yield
        for u, (chunk, rows, head, cols) in enumerate(units):
            g_last = gc[u][c - 1:c, :]
            u_ref[rows, cols] = sol[u][:, :DN_HEAD_DIM]
            kcum_ref[rows, cols] = sol[u][:, DN_HEAD_DIM:].astype(BF16)
            qdec_ref[rows, cols] = (q[u] * eg[u]).astype(BF16)
            kdec_ref[rows, cols] = (k[u] * jnp.exp(g_last - gc[u])).astype(BF16)
            qk_ref[rows, head * c:(head + 1) * c] = jnp.where(incl, qk[u] * decay[u], 0.0).astype(BF16)
            gt_ref[chunk, :, cols] = jnp.broadcast_to(jnp.exp(g_last), (SUBLANES, DN_HEAD_DIM))

    def scan(chunks):
        hs = range(DN_HEADS)
        cols = heads()
        state = [st_ref[h] for h in hs]
        for chunk in chunks:
            rows = pl.ds(rows_of(chunk)[0], c)
            state16 = [state[h].astype(BF16) for h in hs]
            v16 = [(u_ref[rows, cols[h]] - _dot(kcum_ref[rows, cols[h]], state16[h])).astype(BF16)
                   for h in hs]
            yield
            upd = [_dot_tn(kdec_ref[rows, cols[h]], v16[h]) for h in hs]
            out = [_dot(qdec_ref[rows, cols[h]], state16[h])
                   + _dot(qk_ref[rows, h * c:(h + 1) * c], v16[h]) for h in hs]
            yield
            state = [state[h] * gt_ref[chunk, 0:1, cols[h]] + upd[h] for h in hs]
            for h in hs:
                o_ref[rows, cols[h]] = (_rmsnorm(out[h], ng_ref[...])
                                        * _silu(z_ref[rows, cols[h]].astype(F32))).astype(BF16)
        for h in hs:
            st_ref[h] = state[h]

    def interleave(*stages):
        active = list(stages)
        while active:
            for gen in list(active):
                if next(gen, StopIteration) is StopIteration:
                    active.remove(gen)

    def pair(n):
        return [DN_STREAMS * n + t for t in range(DN_STREAMS)]

    n_iter = N_CHUNKS // DN_STREAMS
    st_ref[...] = jnp.zeros_like(st_ref)
    interleave(prepare(pair(0)))

    def body(n, carry):
        interleave(scan(pair(n - 1)), prepare(pair(n)))
        return carry

    lax.fori_loop(1, n_iter, body, 0)
    interleave(scan(pair(n_iter - 1)))


def _deltanet(proj, small, conv_w, alog_row, dtb_row, norm_gain, layer):
    qb = COL_DN // BRANCH_WIDTH
    zb = COL_DNZ // BRANCH_WIDTH

    def colspec(blk):
        return pl.BlockSpec((SEQ, BRANCH_WIDTH), lambda b: (b, blk))

    def wspec(blk):
        return _of_layer(layer, (4, BRANCH_WIDTH), lambda b: (0, blk))

    row = _of_layer(layer, (1, LANES))
    return pl.pallas_call(
        _dn_kernel,
        grid=(BATCH,),
        in_specs=[
            colspec(qb), colspec(qb + 1), colspec(qb + 2), colspec(zb),
            pl.BlockSpec((SEQ, LANES), lambda b: (b, 0)),
            wspec(0), wspec(1), wspec(2),
            row, row, row,
        ],
        out_specs=pl.BlockSpec((SEQ, BRANCH_WIDTH), lambda b: (b, 0)),
        out_shape=jax.ShapeDtypeStruct((TOKENS, BRANCH_WIDTH), BF16),
        scratch_shapes=[
            pltpu.VMEM((SEQ, BRANCH_WIDTH), F32),
            pltpu.VMEM((SEQ, BRANCH_WIDTH), BF16),
            pltpu.VMEM((SEQ, BRANCH_WIDTH), BF16),
            pltpu.VMEM((SEQ, BRANCH_WIDTH), BF16),
            pltpu.VMEM((SEQ, DN_HEADS * DN_CHUNK), BF16),
            pltpu.VMEM((N_CHUNKS, SUBLANES, BRANCH_WIDTH), F32),
            pltpu.VMEM((DN_HEADS, DN_HEAD_DIM, DN_HEAD_DIM), F32),
        ],
        compiler_params=_cparams(("parallel",)),
        name="deltanet",
    )(proj, proj, proj, proj, small, conv_w, conv_w, conv_w, alog_row, dtb_row, norm_gain)


def _merge_kernel(yf_ref, sb_ref, sc_ref, sv_ref, scp_ref, svp_ref, yd_ref, g0_ref, g1_ref, g2_ref,
                  x_ref, wb_ref, wo_ref, cw_ref, o_ref):
    i = pl.program_id(0)
    keep_prev = jnp.where(i % (SEQ // MERGE_TM) != 0, 1.0, 0.0)
    cur = sc_ref[...].astype(F32) * sv_ref[...].astype(F32)
    prev = scp_ref[...].astype(F32) * svp_ref[...].astype(F32) * keep_prev
    cat = jnp.concatenate([prev, cur], axis=0)
    cw = cw_ref[...]
    conv = cw[2:3] * cur + cw[1:2] * _shifted(cat, 1) + cw[0:1] * _shifted(cat, 2)
    y_sc = (sb_ref[...].astype(F32) * conv).astype(BF16)
    sub = MERGE_TM // MERGE_SPLIT
    blocks = [slice(r * sub, (r + 1) * sub) for r in range(MERGE_SPLIT)]
    branch = [(_dot(yf_ref[rows, :], wb_ref[0]), _dot(y_sc[rows], wb_ref[1]), _dot(yd_ref[rows, :], wb_ref[2]))
              for rows in blocks]
    for rows, (d0, d1, d2) in zip(blocks, branch):
        merged = (_sigmoid(g0_ref[rows, :].astype(F32)) * d0 + _sigmoid(g1_ref[rows, :].astype(F32)) * d1
                  + _sigmoid(g2_ref[rows, :].astype(F32)) * d2)
        o_ref[rows, :] = x_ref[rows, :] + _dot(merged.astype(BF16), wo_ref[...])


def _merge(x, proj, y_fox, y_dn, w_branch, w_o, sc_conv_w, layer):
    tm = MERGE_TM
    scb = COL_SC // BRANCH_WIDTH
    gb = COL_GATE // D_MODEL
    hpb = tm // HALO

    def prev_rows(i):
        return jnp.maximum(i * hpb - 1, 0)

    return pl.pallas_call(
        _merge_kernel,
        grid=(TOKENS // tm,),
        in_specs=[
            pl.BlockSpec((tm, BRANCH_WIDTH), lambda i: (i, 0)),
            pl.BlockSpec((tm, BRANCH_WIDTH), lambda i: (i, scb)),
            pl.BlockSpec((tm, BRANCH_WIDTH), lambda i: (i, scb + 1)),
            pl.BlockSpec((tm, BRANCH_WIDTH), lambda i: (i, scb + 2)),
            pl.BlockSpec((HALO, BRANCH_WIDTH), lambda i: (prev_rows(i), scb + 1)),
            pl.BlockSpec((HALO, BRANCH_WIDTH), lambda i: (prev_rows(i), scb + 2)),
            pl.BlockSpec((tm, BRANCH_WIDTH), lambda i: (i, 0)),
            pl.BlockSpec((tm, D_MODEL), lambda i: (i, gb)),
            pl.BlockSpec((tm, D_MODEL), lambda i: (i, gb + 1)),
            pl.BlockSpec((tm, D_MODEL), lambda i: (i, gb + 2)),
            pl.BlockSpec((tm, D_MODEL), lambda i: (i, 0)),
            _of_layer(layer, (3, BRANCH_WIDTH, D_MODEL), resident=True),
            _of_layer(layer, (D_MODEL, D_MODEL), resident=True),
            _of_layer(layer, (3, BRANCH_WIDTH)),
        ],
        out_specs=pl.BlockSpec((tm, D_MODEL), lambda i: (i, 0)),
        out_shape=jax.ShapeDtypeStruct((TOKENS, D_MODEL), F32),
        compiler_params=_cparams(("parallel",)),
        name="merge",
    )(y_fox, proj, proj, proj, proj, proj, y_dn, proj, proj, proj, x, w_branch, w_o, sc_conv_w)


def _ffn_kernel(x_ref, xp_ref, gf_ref, wup_ref, cw_ref, wd_ref, gp_ref, wpg_ref, p_ref, wple_ref, o_ref,
                act_ref):
    i = pl.program_id(0)
    keep_prev = jnp.where(i % (SEQ // FFN_TM) != 0, 1.0, 0.0)
    xx = jnp.concatenate([xp_ref[...] * keep_prev, x_ref[...]], axis=0)
    hn = _rmsnorm(xx, gf_ref[...]).astype(BF16)

    def branch(cols):
        u = _dot(hn, wup_ref[:, cols])
        cw = cw_ref[:, cols]
        return cw[2:3] * u[HALO:] + cw[1:2] * _shifted(u, 1) + cw[0:1] * _shifted(u, 2)

    for j in range(FFN_NF):
        gate = branch(slice(j * FFN_TF, (j + 1) * FFN_TF))
        val = branch(slice(D_FF + j * FFN_TF, D_FF + (j + 1) * FFN_TF))
        act_ref[:, j * FFN_TF:(j + 1) * FFN_TF] = (_silu(gate) * val).astype(BF16)

    x1 = x_ref[...] + _dot(act_ref[...], wd_ref[...])
    h2 = _rmsnorm(x1, gp_ref[...]).astype(BF16)
    pgate = _sigmoid(_dot(h2, wpg_ref[...]))
    emb = _dot(p_ref[...].astype(BF16), wple_ref[...])
    o_ref[...] = x1 + pgate * emb


def _ffn(x, p, g_ffn, w_up, conv_w, w_down, g_ple, w_pg, w_ple, layer):
    tm = FFN_TM
    hpb = tm // HALO

    def whole(shape):
        return _of_layer(layer, shape, resident=True)

    return pl.pallas_call(
        _ffn_kernel,
        grid=(TOKENS // tm,),
        in_specs=[
            pl.BlockSpec((tm, D_MODEL), lambda i: (i, 0)),
            pl.BlockSpec((HALO, D_MODEL), lambda i: (jnp.maximum(i * hpb - 1, 0), 0)),
            whole((1, D_MODEL)),
            whole((D_MODEL, 2 * D_FF)),
            whole((3, 2 * D_FF)),
            whole((D_FF, D_MODEL)),
            whole((1, D_MODEL)),
            whole((D_MODEL, D_MODEL)),
            _of_layer(layer, (tm, PLE_DIM), lambda i: (i, 0)),
            whole((PLE_DIM, D_MODEL)),
        ],
        out_specs=pl.BlockSpec((tm, D_MODEL), lambda i: (i, 0)),
        out_shape=jax.ShapeDtypeStruct((TOKENS, D_MODEL), F32),
        scratch_shapes=[pltpu.VMEM((tm, D_FF), BF16)],
        compiler_params=_cparams(("parallel",)),
        name="ffn",
    )(x, x, g_ffn, w_up, conv_w, w_down, g_ple, w_pg, p, w_ple)


def _lane_rows(values, offset):
    depth, n = values.shape
    return jnp.pad(values.astype(F32), ((0, 0), (offset, LANES - offset - n)))[:, None, :]


def kernel(x, p, g_mix, w_in, b_fox_f, fox_q_gain, fox_k_gain, sc_conv_w, dn_conv_w, dn_a_log,
           dn_dt_bias, dn_norm_gain, w_branch, w_o, g_ffn, w_up, ffn_conv_w, w_down, g_ple,
           w_ple_gate, w_ple):
    depth = p.shape[0]
    x = x.reshape(TOKENS, D_MODEL)
    p = p.reshape(depth, TOKENS, PLE_DIM)

    w_big, w_small = _regroup(w_in)
    w_branch16, w_o16, w_up16, w_down16 = (w.astype(BF16) for w in (w_branch, w_o, w_up, w_down))
    w_pg16, w_ple16 = w_ple_gate.astype(BF16), w_ple.astype(BF16)
    g_mix3, g_ffn3, g_ple3, dn_gain3 = (g[:, None, :] for g in (g_mix, g_ffn, g_ple, dn_norm_gain))
    q_gain3 = jnp.tile(fox_q_gain, (1, 2))[:, None, :]
    k_gain3 = jnp.tile(fox_k_gain, (1, 2))[:, None, :]
    fox_bias = _lane_rows(b_fox_f, SM_FOXF)
    a_log = _lane_rows(dn_a_log, SM_A)
    dt_bias = _lane_rows(dn_dt_bias, SM_A)

    for layer in range(depth):
        proj, small = _proj(x, g_mix3, w_big, w_small, layer)
        cf = _cumf(small, fox_bias, layer)
        y_fox = _fox(proj, cf, q_gain3, k_gain3, layer)
        y_dn = _deltanet(proj, small, dn_conv_w, a_log, dt_bias, dn_gain3, layer)
        x = _merge(x, proj, y_fox, y_dn, w_branch16, w_o16, sc_conv_w, layer)
        x = _ffn(x, p, g_ffn3, w_up16, ffn_conv_w, w_down16, g_ple3, w_pg16, w_ple16, layer)
    return x.reshape(BATCH, SEQ, D_MODEL)
```
